```python
import math
import jax, jax.numpy as jnp
from jax import lax
import numpy as np

D_MODEL = 1024
BATCH = 32
SEQ = 2048
DEPTH = 1

HEAD_DIM = 64
N_HEADS = 8
N_KV_HEADS = 2
GQA_GROUP = N_HEADS // N_KV_HEADS
WINDOW = 128
BLOCK = 128
ROT_DIM = HEAD_DIM // 4
ROPE_THETA = 500000.0
ATTN_WIDTH = N_HEADS * HEAD_DIM
KV_WIDTH = N_KV_HEADS * HEAD_DIM
CHUNK = 128
G_GROUPS = 8
G_GROUP_DIM = 64
G_WIDTH = G_GROUPS * G_GROUP_DIM
Q_END = ATTN_WIDTH
K_END = Q_END + KV_WIDTH
V_END = K_END + KV_WIDTH
Z_END = V_END + 2 * G_WIDTH
GA_END = Z_END + D_MODEL
IN_COLS = GA_END + D_MODEL
N_GROUPS = 4
EXPERTS_PER_GROUP = 8
N_EXPERTS = N_GROUPS * EXPERTS_PER_GROUP
TOP_K = 2
D_EXPERT = 512
EPS = 1e-5
NEG_INF = -1e30

kernel_name = "hybrid_swa_sink_chunked_gmlp_hier_moe"


def rms_norm(x, g):
    xf = x.astype(jnp.float32)
    y = xf * lax.rsqrt(jnp.mean(xf * xf, axis=-1, keepdims=True) + EPS)
    return (y * g.astype(jnp.float32)).astype(x.dtype)


def layer_norm(x, g, b):
    xf = x.astype(jnp.float32)
    mu = jnp.mean(xf, axis=-1, keepdims=True)
    var = jnp.mean(jnp.square(xf - mu), axis=-1, keepdims=True)
    y = (xf - mu) * lax.rsqrt(var + EPS)
    return (y * g.astype(jnp.float32) + b.astype(jnp.float32)).astype(x.dtype)


def rope_tables(seq):
    inv_freq = ROPE_THETA ** (-jnp.arange(0, ROT_DIM, 2, dtype=jnp.float32) / ROT_DIM)
    pos = jnp.arange(seq, dtype=jnp.float32)
    ang = pos[:, None] * inv_freq[None, :]
    return jnp.cos(ang), jnp.sin(ang)


def partial_rope(t, cos, sin):
    tr, tp = t[..., :ROT_DIM], t[..., ROT_DIM:]
    t1 = tr[..., :ROT_DIM // 2].astype(jnp.float32)
    t2 = tr[..., ROT_DIM // 2:].astype(jnp.float32)
    c = cos[None, :, None, :]
    s = sin[None, :, None, :]
    rot = jnp.concatenate([t1 * c - t2 * s, t2 * c + t1 * s], axis=-1)
    return jnp.concatenate([rot.astype(t.dtype), tp], axis=-1)


def sliding_window_sink_attention(q, k, v, sinks):
    b, s = q.shape[0], q.shape[1]
    nb = s // BLOCK
    q = q.reshape(b, nb, BLOCK, N_KV_HEADS, GQA_GROUP, HEAD_DIM)
    k = k.reshape(b, nb, BLOCK, N_KV_HEADS, HEAD_DIM)
    v = v.reshape(b, nb, BLOCK, N_KV_HEADS, HEAD_DIM)
    prev = lambda t: jnp.concatenate([jnp.zeros_like(t[:, :1]), t[:, :-1]], axis=1)
    kk = jnp.concatenate([prev(k), k], axis=2)
    vv = jnp.concatenate([prev(v), v], axis=2)
    scores = jnp.einsum("bnqhgd,bnkhd->bnhgqk", q, kk,
                        preferred_element_type=jnp.float32) * (1.0 / math.sqrt(HEAD_DIM))
    qi = jnp.arange(BLOCK)[:, None]
    kj = jnp.arange(2 * BLOCK)[None, :]
    rel = qi + BLOCK - kj
    band = (rel >= 0) & (rel < WINDOW)
    has_prev = (jnp.arange(nb) > 0)[:, None, None] | (kj >= BLOCK)[None]
    valid = band[None] & has_prev
    scores = jnp.where(valid[None, :, None, None], scores, NEG_INF)
    sink = jnp.broadcast_to(
        sinks.astype(jnp.float32).reshape(N_KV_HEADS, GQA_GROUP)[None, None, :, :, None, None],
        scores.shape[:-1] + (1,))
    probs = jax.nn.softmax(jnp.concatenate([scores, sink], axis=-1), axis=-1)[..., :-1]
    out = jnp.einsum("bnhgqk,bnkhd->bnqhgd", probs.astype(vv.dtype), vv)
    return out.reshape(b, s, ATTN_WIDTH)


def chunked_spatial_gating(z, ln_g, ln_b, w_s, b_s):
    b, s = z.shape[0], z.shape[1]
    nc = s // CHUNK
    u, v = z[..., :G_WIDTH], z[..., G_WIDTH:]
    v = layer_norm(v, ln_g, ln_b).reshape(b, nc, CHUNK, G_GROUPS, G_GROUP_DIM)
    causal = jnp.tril(jnp.ones((CHUNK, CHUNK), dtype=w_s.dtype))
    w = w_s * causal[None]
    mixed = jnp.einsum("gqp,bcpgd->bcqgd", w, v) + b_s.T[None, None, :, :, None]
    return u * mixed.reshape(b, s, G_WIDTH)


def hierarchical_moe(h, wg, bg, we, be, w_gate, w_up, w_down):
    b, s, d = h.shape
    t = h.reshape(b * s, d)
    g_logits = (t @ wg).astype(jnp.float32) + bg.astype(jnp.float32)
    g_prob = jax.nn.softmax(g_logits, axis=-1)
    g_sel = jnp.argmax(g_prob, axis=-1)
    g_w = jnp.take_along_axis(g_prob, g_sel[:, None], axis=-1)
    e_logits = ((t @ we).astype(jnp.float32) + be.astype(jnp.float32)).reshape(-1, N_GROUPS, EXPERTS_PER_GROUP)
    e_in_group = jnp.take_along_axis(e_logits, g_sel[:, None, None], axis=1)[:, 0]
    top_logit, top_idx = lax.top_k(e_in_group, TOP_K)
    weights = g_w * jax.nn.softmax(top_logit, axis=-1)
    expert_id = (g_sel[:, None] * EXPERTS_PER_GROUP + top_idx).reshape(-1)
    order = jnp.argsort(expert_id)
    tok = order // TOP_K
    xs = t[tok]
    ws = weights.reshape(-1)[order]
    sizes = jnp.bincount(expert_id, length=N_EXPERTS).astype(jnp.int32)
    hid = jax.nn.silu(lax.ragged_dot(xs, w_gate, sizes)) * lax.ragged_dot(xs, w_up, sizes)
    ys = lax.ragged_dot(hid, w_down, sizes) * ws[:, None].astype(hid.dtype)
    out = jnp.zeros_like(t).at[tok].add(ys.astype(t.dtype))
    return out.reshape(b, s, d)


def setup_inputs(seed: int = 0) -> dict:
    key = jax.random.key(seed)
    ks = jax.random.split(key, 24)
    n = lambda k, shape, scale: jax.random.normal(k, shape, jnp.float32) * scale
    return {
        "x": n(ks[0], (BATCH, SEQ, D_MODEL), 1.0),
        "norm1_g": 1.0 + n(ks[1], (DEPTH, D_MODEL), 0.05),
        "w_in": n(ks[2], (DEPTH, D_MODEL, IN_COLS), D_MODEL ** -0.5),
        "attn_sinks": n(ks[3], (DEPTH, N_HEADS), 1.0),
        "gmlp_ln_g": 1.0 + n(ks[4], (DEPTH, G_WIDTH), 0.05),
        "gmlp_ln_b": n(ks[5], (DEPTH, G_WIDTH), 0.05),
        "gmlp_ws": n(ks[6], (DEPTH, G_GROUPS, CHUNK, CHUNK), CHUNK ** -0.5),
        "gmlp_bs": 1.0 + n(ks[7], (DEPTH, G_GROUPS, CHUNK), 0.1),
        "w_attn_branch": n(ks[8], (DEPTH, ATTN_WIDTH, D_MODEL), ATTN_WIDTH ** -0.5),
        "w_gmlp_branch": n(ks[9], (DEPTH, G_WIDTH, D_MODEL), G_WIDTH ** -0.5),
        "w_out": n(ks[10], (DEPTH, D_MODEL, D_MODEL), D_MODEL ** -0.5),
        "norm2_g": 1.0 + n(ks[11], (DEPTH, D_MODEL), 0.05),
        "router_group_w": n(ks[12], (DEPTH, D_MODEL, N_GROUPS), D_MODEL ** -0.5),
        "router_group_b": n(ks[13], (DEPTH, N_GROUPS), 0.01),
        "router_expert_w": n(ks[14], (DEPTH, D_MODEL, N_EXPERTS), D_MODEL ** -0.5),
        "router_expert_b": n(ks[15], (DEPTH, N_EXPERTS), 0.01),
        "expert_w_gate": n(ks[16], (DEPTH, N_EXPERTS, D_MODEL, D_EXPERT), D_MODEL ** -0.5),
        "expert_w_up": n(ks[17], (DEPTH, N_EXPERTS, D_MODEL, D_EXPERT), D_MODEL ** -0.5),
        "expert_w_down": n(ks[18], (DEPTH, N_EXPERTS, D_EXPERT, D_MODEL), D_EXPERT ** -0.5),
        "final_norm_g": 1.0 + n(ks[19], (D_MODEL,), 0.05),
    }


def reference(x, norm1_g, w_in, attn_sinks, gmlp_ln_g, gmlp_ln_b, gmlp_ws, gmlp_bs,
              w_attn_branch, w_gmlp_branch, w_out, norm2_g, router_group_w, router_group_b,
              router_expert_w, router_expert_b, expert_w_gate, expert_w_up, expert_w_down,
              final_norm_g):
    b, s, _ = x.shape
    cos, sin = rope_tables(s)
    for l in range(DEPTH):
        h = rms_norm(x, norm1_g[l])
        proj = h @ w_in[l]
        q, k, v, z, ga, gg = jnp.split(proj, [Q_END, K_END, V_END, Z_END, GA_END], axis=-1)
        q = partial_rope(q.reshape(b, s, N_HEADS, HEAD_DIM), cos, sin)
        k = partial_rope(k.reshape(b, s, N_KV_HEADS, HEAD_DIM), cos, sin)
        v = v.reshape(b, s, N_KV_HEADS, HEAD_DIM)
        a = sliding_window_sink_attention(q, k, v, attn_sinks[l]) @ w_attn_branch[l]
        g = chunked_spatial_gating(jax.nn.gelu(z), gmlp_ln_g[l], gmlp_ln_b[l],
                                   gmlp_ws[l], gmlp_bs[l]) @ w_gmlp_branch[l]
        mixed = jax.nn.sigmoid(ga) * a + jax.nn.sigmoid(gg) * g
        x = x + mixed @ w_out[l]
        x = x + hierarchical_moe(rms_norm(x, norm2_g[l]), router_group_w[l], router_group_b[l],
                                 router_expert_w[l], router_expert_b[l], expert_w_gate[l],
                                 expert_w_up[l], expert_w_down[l])
    return rms_norm(x, final_norm_g)
```

```python
import functools
import math

import jax
import jax.numpy as jnp
from jax import lax
from jax.experimental import pallas as pl
from jax.experimental.pallas import tpu as pltpu

D_MODEL = 1024
HEAD_DIM = 64
N_HEADS = 8
N_KV_HEADS = 2
BLOCK = 128
ROT_DIM = HEAD_DIM // 4
ROPE_THETA = 500000.0
ATTN_WIDTH = N_HEADS * HEAD_DIM
KV_WIDTH = N_KV_HEADS * HEAD_DIM
G_GROUPS = 8
G_GROUP_DIM = 64
G_WIDTH = G_GROUPS * G_GROUP_DIM
Q_END = ATTN_WIDTH
K_END = Q_END + KV_WIDTH
V_END = K_END + KV_WIDTH
Z_END = V_END + 2 * G_WIDTH
GA_END = Z_END + D_MODEL
IN_COLS = GA_END + D_MODEL
N_GROUPS = 4
EXPERTS_PER_GROUP = 8
N_EXPERTS = N_GROUPS * EXPERTS_PER_GROUP
TOP_K = 2
D_EXPERT = 512
EPS = 1e-5
NEG_INF = -1e30

LANES = 128
ROUTER_ROWS = 8 + N_EXPERTS

TS = 512
TM = 256
TB = 512
VMEM_LIMIT = 56 * 1024 * 1024


def _rms(x, g):
    return x * lax.rsqrt(jnp.mean(x * x, axis=-1, keepdims=True) + EPS) * g


def _mixer_kernel(x_ref, g1_ref, win_ref, rc_ref, rs1_ref, rs2_ref, sink_ref, lng_ref, lnb_ref,
                  wcat_ref, bsf_ref, wab_ref, wgb_ref, wout_ref, g2_ref, wrt_ref, brt_ref, tri_ref,
                  x1_ref, ri_ref, rw_ref, cnt_ref,
                  kl_ref, vt_ref, attnt_ref, gated_ref, base_ref, *, blocks_per_seq):
    step = pl.program_id(0)
    s_blk = step % blocks_per_seq
    ts = x_ref.shape[0]
    nb = ts // BLOCK
    f32, bf16 = jnp.float32, jnp.bfloat16

    @pl.when(step == 0)
    def _():
        base_ref[...] = jnp.zeros_like(base_ref)

    @pl.when(s_blk == 0)
    def _():
        kl_ref[:, 0:BLOCK, :] = jnp.zeros((4, BLOCK, LANES), bf16)
        vt_ref[:, 0:BLOCK] = jnp.zeros((LANES, BLOCK), bf16)

    x = x_ref[...]
    hb = _rms(x, g1_ref[...]).astype(bf16)

    qkv = jnp.dot(hb, win_ref[:, 0:V_END], preferred_element_type=f32)
    rc, rs1, rs2 = rc_ref[...], rs1_ref[...], rs2_ref[...]

    def rope(t):
        return t * rc + pltpu.roll(t, 8, 1) * rs1 + pltpu.roll(t, LANES - 8, 1) * rs2

    scale = 1.0 / math.sqrt(HEAD_DIM)
    qb = [(rope(qkv[:, j * LANES:(j + 1) * LANES]) * scale).astype(bf16) for j in range(4)]
    k = rope(qkv[:, Q_END:K_END])
    v = qkv[:, K_END:V_END]
    k_sw = pltpu.roll(k, HEAD_DIM, 1)
    lo = lax.broadcasted_iota(jnp.int32, (ts, LANES), 1) < HEAD_DIM
    zero = jnp.zeros_like(k)
    kl_ref[0, BLOCK:BLOCK + ts, :] = jnp.where(lo, k, zero).astype(bf16)
    kl_ref[1, BLOCK:BLOCK + ts, :] = jnp.where(lo, zero, k_sw).astype(bf16)
    kl_ref[2, BLOCK:BLOCK + ts, :] = jnp.where(lo, k_sw, zero).astype(bf16)
    kl_ref[3, BLOCK:BLOCK + ts, :] = jnp.where(lo, zero, k).astype(bf16)
    vt_ref[:, BLOCK:BLOCK + ts] = v.T.astype(bf16)

    kj = lax.broadcasted_iota(jnp.int32, (2 * BLOCK, 2 * BLOCK), 0)
    qi = lax.broadcasted_iota(jnp.int32, (2 * BLOCK, 2 * BLOCK), 1) % BLOCK
    band = (kj > qi) & (kj <= qi + BLOCK)
    kmin = jnp.where(s_blk == 0, BLOCK, 0)
    bias_first = jnp.where(band & (kj >= kmin), 0.0, NEG_INF).astype(f32)
    bias_rest = jnp.where(band, 0.0, NEG_INF).astype(f32)
    for i in range(nb):
        bias = bias_first if i == 0 else bias_rest
        rows = slice(i * BLOCK, (i + 1) * BLOCK)
        win = slice(i * BLOCK, (i + 2) * BLOCK)
        for g in range(N_KV_HEADS):
            qs = jnp.concatenate([qb[2 * g][rows], qb[2 * g + 1][rows]], axis=0)
            for p in range(2):
                st = lax.dot_general(kl_ref[2 * g + p, win, :], qs, (((1,), (1,)), ((), ())),
                                     preferred_element_type=f32) + bias
                sink = sink_ref[2 * g + p:2 * g + p + 1, :]
                m = jnp.maximum(jnp.max(st, axis=0, keepdims=True), sink)
                e = jnp.exp(st - m)
                den = jnp.sum(e, axis=0, keepdims=True) + jnp.exp(sink - m)
                ot = jnp.dot(vt_ref[g * HEAD_DIM:(g + 1) * HEAD_DIM, win], e.astype(bf16),
                             preferred_element_type=f32) * (1.0 / den)
                h0 = 4 * g + p
                attnt_ref[h0 * HEAD_DIM:(h0 + 1) * HEAD_DIM, rows] = ot[:, 0:BLOCK]
                attnt_ref[(h0 + 2) * HEAD_DIM:(h0 + 3) * HEAD_DIM, rows] = ot[:, BLOCK:2 * BLOCK]
    kl_ref[:, 0:BLOCK, :] = kl_ref[:, ts:ts + BLOCK, :]
    vt_ref[:, 0:BLOCK] = vt_ref[:, ts:ts + BLOCK]
    a = jnp.dot(attnt_ref[...].T.astype(bf16), wab_ref[...], preferred_element_type=f32)

    z = jax.nn.gelu(jnp.dot(hb, win_ref[:, V_END:Z_END], preferred_element_type=f32))
    u, v2 = z[:, 0:G_WIDTH], z[:, G_WIDTH:2 * G_WIDTH]
    mu = jnp.mean(v2, axis=-1, keepdims=True)
    vc = v2 - mu
    var = jnp.mean(vc * vc, axis=-1, keepdims=True)
    vn = (vc * lax.rsqrt(var + EPS) * lng_ref[...] + lnb_ref[...]).astype(bf16)
    wq = lax.broadcasted_iota(jnp.int32, (BLOCK, 2 * BLOCK), 0)
    wp = lax.broadcasted_iota(jnp.int32, (BLOCK, 2 * BLOCK), 1) % BLOCK
    causal = wp <= wq
    lo_c = lax.broadcasted_iota(jnp.int32, (BLOCK, LANES), 1) < G_GROUP_DIM
    for j in range(G_GROUPS // 2):
        wj = jnp.where(causal, wcat_ref[j], 0.0).astype(bf16)
        cols = slice(j * LANES, (j + 1) * LANES)
        for c in range(nb):
            rows = slice(c * BLOCK, (c + 1) * BLOCK)
            vp = vn[rows, cols]
            zb = jnp.zeros_like(vp)
            rhs = jnp.concatenate([jnp.where(lo_c, vp, zb), jnp.where(lo_c, zb, vp)], axis=0)
            mixed = jnp.dot(wj, rhs, preferred_element_type=f32) + bsf_ref[:, cols]
            gated_ref[rows, cols] = u[rows, cols] * mixed
    gm = jnp.dot(gated_ref[...].astype(bf16), wgb_ref[...], preferred_element_type=f32)

    ga = jnp.dot(hb, win_ref[:, Z_END:GA_END], preferred_element_type=f32)
    mix = jax.nn.sigmoid(ga) * a
    gg = jnp.dot(hb, win_ref[:, GA_END:IN_COLS], preferred_element_type=f32)
    mix = mix + jax.nn.sigmoid(gg) * gm
    x1 = x + jnp.dot(mix.astype(bf16), wout_ref[...], preferred_element_type=f32)
    x1_ref[...] = x1

    xnb = _rms(x1, g2_ref[...]).astype(bf16)
    lt = lax.dot_general(wrt_ref[...], xnb, (((1,), (1,)), ((), ())),
                         preferred_element_type=f32) + brt_ref[...]
    r = [lt[i:i + 1, :] for i in range(N_GROUPS)]
    gm_ = jnp.maximum(jnp.maximum(r[0], r[1]), jnp.maximum(r[2], r[3]))
    gsel = jnp.where(r[0] == gm_, 0.0, jnp.where(r[1] == gm_, 1.0, jnp.where(r[2] == gm_, 2.0, 3.0)))
    g_w = 1.0 / (jnp.exp(r[0] - gm_) + jnp.exp(r[1] - gm_) + jnp.exp(r[2] - gm_) + jnp.exp(r[3] - gm_))
    eg = [lt[8 + 8 * kk:16 + 8 * kk, :] for kk in range(N_GROUPS)]
    ein = jnp.where(gsel == 0.0, eg[0], jnp.where(gsel == 1.0, eg[1], jnp.where(gsel == 2.0, eg[2], eg[3])))
    io8 = lax.broadcasted_iota(jnp.int32, (EXPERTS_PER_GROUP, ts), 0).astype(f32)
    m1 = jnp.max(ein, axis=0, keepdims=True)
    i1 = jnp.min(jnp.where(ein == m1, io8, 8.0), axis=0, keepdims=True)
    e2 = jnp.where(io8 == i1, NEG_INF, ein)
    m2 = jnp.max(e2, axis=0, keepdims=True)
    i2 = jnp.min(jnp.where(e2 == m2, io8, 8.0), axis=0, keepdims=True)
    t2 = jnp.exp(m2 - m1)
    w1 = 1.0 / (1.0 + t2)
    wt0, wt1 = g_w * w1, g_w * (t2 * w1)
    eid0, eid1 = gsel * 8.0 + i1, gsel * 8.0 + i2

    io32 = lax.broadcasted_iota(jnp.int32, (N_EXPERTS, ts), 0).astype(f32)
    oh0, oh1 = io32 == eid0, io32 == eid1
    both = jnp.where(oh0, 1.0, jnp.where(oh1, 1.0, 0.0))
    pref = jnp.dot(both.astype(bf16), tri_ref[...], preferred_element_type=f32)
    tot = base_ref[...] + pref
    rank0 = jnp.sum(jnp.where(oh0, tot, 0.0), axis=0, keepdims=True)
    rank1 = jnp.sum(jnp.where(oh1, tot, 0.0), axis=0, keepdims=True)
    new_base = base_ref[...] + jnp.sum(both, axis=1, keepdims=True)
    base_ref[...] = new_base
    cnt_ref[...] = new_base[:, 0:LANES]

    ri = jnp.where(io8 == 0.0, eid0, jnp.where(io8 == 1.0, eid1,
                   jnp.where(io8 == 2.0, rank0, jnp.where(io8 == 3.0, rank1, 0.0))))
    ri_ref[...] = ri.astype(jnp.int32)
    io128 = lax.broadcasted_iota(jnp.int32, (LANES, ts), 0)
    rw_ref[...] = jnp.where(io128 == 0, wt0, jnp.where(io128 == 1, wt1, 0.0)).T


def _const_spec(shape):
    nd = len(shape)
    return pl.BlockSpec(shape, lambda i: (0,) * nd)


def _mixer(x2, g1, win, rc, rs1, rs2, sinks, lng, lnb, wcat, bsf, wab, wgb, wout, g2, wrt, brt, tri, seq):
    t = x2.shape[0]
    blocks_per_seq = seq // TS
    tok = lambda i: (i, 0)
    pos = lambda i: (i % blocks_per_seq, 0)
    in_specs = [
        pl.BlockSpec((TS, D_MODEL), tok),
        _const_spec(g1.shape), _const_spec(win.shape),
        pl.BlockSpec((TS, LANES), pos), pl.BlockSpec((TS, LANES), pos), pl.BlockSpec((TS, LANES), pos),
        _const_spec(sinks.shape), _const_spec(lng.shape), _const_spec(lnb.shape),
        _const_spec(wcat.shape), _const_spec(bsf.shape), _const_spec(wab.shape), _const_spec(wgb.shape),
        _const_spec(wout.shape), _const_spec(g2.shape), _const_spec(wrt.shape), _const_spec(brt.shape),
        _const_spec(tri.shape),
    ]
    out_shape = (
        jax.ShapeDtypeStruct((t, D_MODEL), jnp.float32),
        jax.ShapeDtypeStruct((8, t), jnp.int32),
        jax.ShapeDtypeStruct((t, LANES), jnp.float32),
        jax.ShapeDtypeStruct((N_EXPERTS, LANES), jnp.float32),
    )
    out_specs = (
        pl.BlockSpec((TS, D_MODEL), tok),
        pl.BlockSpec((8, TS), lambda i: (0, i)),
        pl.BlockSpec((TS, LANES), tok),
        pl.BlockSpec((N_EXPERTS, LANES), lambda i: (0, 0)),
    )
    scratch = [
        pltpu.VMEM((4, BLOCK + TS, LANES), jnp.bfloat16),
        pltpu.VMEM((LANES, BLOCK + TS), jnp.bfloat16),
        pltpu.VMEM((ATTN_WIDTH, TS), jnp.float32),
        pltpu.VMEM((TS, G_WIDTH), jnp.float32),
        pltpu.VMEM((N_EXPERTS, TS), jnp.float32),
    ]
    return pl.pallas_call(
        functools.partial(_mixer_kernel, blocks_per_seq=blocks_per_seq),
        grid=(t // TS,),
        in_specs=in_specs,
        out_specs=out_specs,
        out_shape=out_shape,
        scratch_shapes=scratch,
        compiler_params=pltpu.CompilerParams(dimension_semantics=("arbitrary",),
                                             vmem_limit_bytes=VMEM_LIMIT),
        name="mixer",
    )(x2, g1, win, rc, rs1, rs2, sinks, lng, lnb, wcat, bsf, wab, wgb, wout, g2, wrt, brt, tri)


def _dispatch_kernel(cnt_ref, off_ref, pos_ref, x1_ref, g2_ref, xs_ref, xn_ref, zero_ref, sem, zsem, tsem):
    step = pl.program_id(0)
    tb = x1_ref.shape[0]
    n_tiles_max = xs_ref.shape[0] // TM

    def row_copy(r, dst):
        return pltpu.make_async_copy(xn_ref.at[pl.ds(r, 1)], xs_ref.at[pl.ds(dst, 1)], sem)

    def zero_copy(dst):
        return pltpu.make_async_copy(zero_ref.at[pl.ds(0, 1)], xs_ref.at[pl.ds(dst, 1)], zsem)

    def zero_tile_copy(tile):
        return pltpu.make_async_copy(zero_ref, xs_ref.at[pl.ds(tile * TM, TM)], tsem)

    @pl.when(step == 0)
    def _():
        zero_ref[...] = jnp.zeros_like(zero_ref)

        def per_expert(e, carry):
            cnt = cnt_ref[e]
            pad = (TM - cnt % TM) % TM
            start = off_ref[e] + cnt

            def issue(r, c):
                zero_copy(start + r).start()
                return c

            lax.fori_loop(0, pad, issue, 0)

            def drain(r, c):
                zero_copy(0).wait()
                return c

            lax.fori_loop(0, pad, drain, 0)
            return carry + (cnt + pad) // TM

        used_tiles = lax.fori_loop(0, N_EXPERTS, per_expert, 0)

        def issue_tail(j, c):
            zero_tile_copy(j).start()
            return c

        lax.fori_loop(used_tiles, n_tiles_max, issue_tail, 0)

        def drain_tail(j, c):
            zero_tile_copy(0).wait()
            return c

        lax.fori_loop(used_tiles, n_tiles_max, drain_tail, 0)

    xn_ref[...] = _rms(x1_ref[...], g2_ref[...])

    def issue(r, c):
        row_copy(r, pos_ref[0, 0, r]).start()
        row_copy(r, pos_ref[0, 1, r]).start()
        return c

    lax.fori_loop(0, tb, issue, 0, unroll=8)
    for _ in range(TOP_K):
        pltpu.make_async_copy(xn_ref, xs_ref.at[pl.ds(0, tb)], sem).wait()


def _dispatch(counts, offsets, pos3, x1, g2, n_rows):
    t = x1.shape[0]
    grid_spec = pltpu.PrefetchScalarGridSpec(
        num_scalar_prefetch=2,
        grid=(t // TB,),
        in_specs=[
            pl.BlockSpec((1, TOP_K, TB), lambda i, *_: (i, 0, 0), memory_space=pltpu.SMEM),
            pl.BlockSpec((TB, D_MODEL), lambda i, *_: (i, 0)),
            pl.BlockSpec((1, D_MODEL), lambda i, *_: (0, 0)),
        ],
        out_specs=pl.BlockSpec(memory_space=pl.ANY),
        scratch_shapes=[
            pltpu.VMEM((TB, D_MODEL), jnp.float32),
            pltpu.VMEM((TM, D_MODEL), jnp.float32),
            pltpu.SemaphoreType.DMA,
            pltpu.SemaphoreType.DMA,
            pltpu.SemaphoreType.DMA,
        ],
    )
    return pl.pallas_call(
        _dispatch_kernel,
        grid_spec=grid_spec,
        out_shape=jax.ShapeDtypeStruct((n_rows, D_MODEL), jnp.float32),
        compiler_params=pltpu.CompilerParams(dimension_semantics=("arbitrary",),
                                             vmem_limit_bytes=VMEM_LIMIT),
        name="dispatch",
    )(counts, offsets, pos3, x1, g2)


def _expert_kernel(te_ref, nt_ref, xs_ref, wg_ref, wu_ref, wd_ref, ys_ref):
    active = pl.program_id(0) < nt_ref[0]

    @pl.when(active)
    def _():
        xb = xs_ref[...].astype(jnp.bfloat16)
        g = jnp.dot(xb, wg_ref[0], preferred_element_type=jnp.float32)
        u = jnp.dot(xb, wu_ref[0], preferred_element_type=jnp.float32)
        hid = (jax.nn.silu(g) * u).astype(jnp.bfloat16)
        ys_ref[...] = jnp.dot(hid, wd_ref[0], preferred_element_type=jnp.float32)

    @pl.when(jnp.logical_not(active))
    def _():
        ys_ref[...] = jnp.zeros_like(ys_ref)


def _experts(tile_expert, n_tiles, xs, wg, wu, wd):
    n_rows = xs.shape[0]
    max_tiles = n_rows // TM
    row = lambda j, te, nt: (jnp.minimum(j, nt[0] - 1), 0)
    wsel = lambda j, te, nt: (te[j], 0, 0)
    grid_spec = pltpu.PrefetchScalarGridSpec(
        num_scalar_prefetch=2,
        grid=(max_tiles,),
        in_specs=[
            pl.BlockSpec((TM, D_MODEL), row),
            pl.BlockSpec((1, D_MODEL, D_EXPERT), wsel),
            pl.BlockSpec((1, D_MODEL, D_EXPERT), wsel),
            pl.BlockSpec((1, D_EXPERT, D_MODEL), wsel),
        ],
        out_specs=pl.BlockSpec((TM, D_MODEL), lambda j, te, nt: (j, 0)),
    )
    return pl.pallas_call(
        _expert_kernel,
        grid_spec=grid_spec,
        out_shape=jax.ShapeDtypeStruct((n_rows, D_MODEL), jnp.float32),
        compiler_params=pltpu.CompilerParams(dimension_semantics=("arbitrary",),
                                             vmem_limit_bytes=VMEM_LIMIT),
        name="experts",
    )(tile_expert, n_tiles, xs, wg, wu, wd)


def _combine_kernel(pos_ref, x1_ref, rw_ref, gf_ref, ys_ref, o_ref, y_ref, sem):
    tb = x1_ref.shape[0]

    def row_copy(k, r, src):
        return pltpu.make_async_copy(ys_ref.at[pl.ds(src, 1)], y_ref.at[k, pl.ds(r, 1)], sem)

    def issue(r, c):
        row_copy(0, r, pos_ref[0, 0, r]).start()
        row_copy(1, r, pos_ref[0, 1, r]).start()
        return c

    lax.fori_loop(0, tb, issue, 0, unroll=8)
    for k in range(TOP_K):
        pltpu.make_async_copy(ys_ref.at[pl.ds(0, tb)], y_ref.at[k], sem).wait()
    rw = rw_ref[...]
    x2 = x1_ref[...] + rw[:, 0:1] * y_ref[0] + rw[:, 1:2] * y_ref[1]
    o_ref[...] = _rms(x2, gf_ref[...])


def _combine(pos3, x1, rw, gf, ys):
    t = x1.shape[0]
    grid_spec = pltpu.PrefetchScalarGridSpec(
        num_scalar_prefetch=0,
        grid=(t // TB,),
        in_specs=[
            pl.BlockSpec((1, TOP_K, TB), lambda i: (i, 0, 0), memory_space=pltpu.SMEM),
            pl.BlockSpec((TB, D_MODEL), lambda i: (i, 0)),
            pl.BlockSpec((TB, LANES), lambda i: (i, 0)),
            pl.BlockSpec((1, D_MODEL), lambda i: (0, 0)),
            pl.BlockSpec(memory_space=pl.ANY),
        ],
        out_specs=pl.BlockSpec((TB, D_MODEL), lambda i: (i, 0)),
        scratch_shapes=[
            pltpu.VMEM((TOP_K, TB, D_MODEL), jnp.float32),
            pltpu.SemaphoreType.DMA,
        ],
    )
    return pl.pallas_call(
        _combine_kernel,
        grid_spec=grid_spec,
        out_shape=jax.ShapeDtypeStruct((t, D_MODEL), jnp.float32),
        compiler_params=pltpu.CompilerParams(dimension_semantics=("arbitrary",),
                                             vmem_limit_bytes=VMEM_LIMIT),
        name="combine",
    )(pos3, x1, rw, gf, ys)


def _rope_lane_tables(seq):
    inv_freq = ROPE_THETA ** (-jnp.arange(0, ROT_DIM, 2, dtype=jnp.float32) / ROT_DIM)
    ang = jnp.arange(seq, dtype=jnp.float32)[:, None] * inv_freq[None, :]
    cos, sin = jnp.cos(ang), jnp.sin(ang)
    half = ROT_DIM // 2
    ones = jnp.ones((seq, HEAD_DIM - ROT_DIM), jnp.float32)
    zeros = jnp.zeros((seq, HEAD_DIM - ROT_DIM), jnp.float32)
    zh = jnp.zeros((seq, half), jnp.float32)
    c = jnp.concatenate([cos, cos, ones], axis=1)
    s1 = jnp.concatenate([zh, sin, zeros], axis=1)
    s2 = jnp.concatenate([-sin, zh, zeros], axis=1)
    rep = LANES // HEAD_DIM
    return jnp.tile(c, (1, rep)), jnp.tile(s1, (1, rep)), jnp.tile(s2, (1, rep))


def kernel(x, norm1_g, w_in, attn_sinks, gmlp_ln_g, gmlp_ln_b, gmlp_ws, gmlp_bs, w_attn_branch,
           w_gmlp_branch, w_out, norm2_g, router_group_w, router_group_b, router_expert_w,
           router_expert_b, expert_w_gate, expert_w_up, expert_w_down, final_norm_g):
    b, s, d = x.shape
    assert d == D_MODEL and s % TS == 0 and (b * s) % TB == 0 and norm1_g.shape[0] == 1
    t = b * s
    bf16, f32 = jnp.bfloat16, jnp.float32
    x2 = x.reshape(t, d)

    rc, rs1, rs2 = _rope_lane_tables(s)
    sk = attn_sinks[0].astype(f32)
    sinks = jnp.stack([
        jnp.concatenate([jnp.full((BLOCK,), 1.0, f32) * sk[4 * g + p], jnp.full((BLOCK,), 1.0, f32) * sk[4 * g + 2 + p]])
        for g in range(N_KV_HEADS) for p in range(2)])
    ws = gmlp_ws[0]
    wcat = jnp.stack([jnp.concatenate([ws[2 * j], ws[2 * j + 1]], axis=1) for j in range(G_GROUPS // 2)])
    bsf = jnp.repeat(gmlp_bs[0].T, G_GROUP_DIM, axis=1)
    wrt = jnp.concatenate([router_group_w[0].T, jnp.zeros((8 - N_GROUPS, d), f32), router_expert_w[0].T], axis=0)
    brt = jnp.concatenate([router_group_b[0], jnp.zeros((8 - N_GROUPS,), f32), router_expert_b[0]])
    brt = jnp.broadcast_to(brt[:, None], (ROUTER_ROWS, TS))
    tri = (jnp.arange(TS)[:, None] < jnp.arange(TS)[None, :]).astype(bf16)

    x1, ri, rw, cnt = _mixer(
        x2, norm1_g, w_in[0].astype(bf16), rc, rs1, rs2, sinks, gmlp_ln_g, gmlp_ln_b, wcat, bsf,
        w_attn_branch[0].astype(bf16), w_gmlp_branch[0].astype(bf16), w_out[0].astype(bf16),
        norm2_g, wrt.astype(bf16), brt, tri, s)

    counts = cnt[:, 0].astype(jnp.int32)
    padded = ((counts + TM - 1) // TM) * TM
    ends = jnp.cumsum(padded)
    offsets = ends - padded
    pos = offsets[ri[0:TOP_K]] + ri[TOP_K:2 * TOP_K]
    pos3 = pos.reshape(TOP_K, t // TB, TB).transpose(1, 0, 2)
    n_rows = t * TOP_K + N_EXPERTS * TM
    max_tiles = n_rows // TM
    n_tiles = (ends[-1] // TM).astype(jnp.int32)
    tile_start = jnp.arange(max_tiles, dtype=jnp.int32) * TM
    tile_expert = jnp.minimum(jnp.searchsorted(ends, tile_start, side="right"), N_EXPERTS - 1).astype(jnp.int32)
    last_expert = tile_expert[jnp.maximum(n_tiles - 1, 0)]
    tile_expert = jnp.where(jnp.arange(max_tiles) < n_tiles, tile_expert, last_expert)

    xs = _dispatch(counts, offsets.astype(jnp.int32), pos3, x1, norm2_g, n_rows)
    ys = _experts(tile_expert, n_tiles.reshape(1), xs, expert_w_gate[0].astype(bf16),
                  expert_w_up[0].astype(bf16), expert_w_down[0].astype(bf16))
    out = _combine(pos3, x1, rw, final_norm_g.reshape(1, d), ys)
    return out.reshape(b, s, d)
```

```python
import functools
import math

import jax
import jax.numpy as jnp
from jax import lax
from jax.experimental import pallas as pl
from jax.experimental.pallas import tpu as pltpu

D_MODEL = 1024
HEAD_DIM = 64
N_HEADS = 8
N_KV_HEADS = 2
BLOCK = 128
ROT_DIM = HEAD_DIM // 4
ROPE_THETA = 500000.0
ATTN_WIDTH = N_HEADS * HEAD_DIM
KV_WIDTH = N_KV_HEADS * HEAD_DIM
G_GROUPS = 8
G_GROUP_DIM = 64
G_WIDTH = G_GROUPS * G_GROUP_DIM
Q_END = ATTN_WIDTH
K_END = Q_END + KV_WIDTH
V_END = K_END + KV_WIDTH
Z_END = V_END + 2 * G_WIDTH
GA_END = Z_END + D_MODEL
IN_COLS = GA_END + D_MODEL
N_GROUPS = 4
EXPERTS_PER_GROUP = 8
N_EXPERTS = N_GROUPS * EXPERTS_PER_GROUP
TOP_K = 2
D_EXPERT = 512
EPS = 1e-5
NEG_INF = -1e30

LANES = 128
ROUTER_ROWS = 8 + N_EXPERTS

TS = 512
TM = 256
TB = 512
VMEM_LIMIT = 56 * 1024 * 1024


def _rms(x, g):
    return x * lax.rsqrt(jnp.mean(x * x, axis=-1, keepdims=True) + EPS) * g


def _mixer_kernel(x_ref, g1_ref, win_ref, rc_ref, rs1_ref, rs2_ref, sink_ref, lng_ref, lnb_ref,
                  wcat_ref, bsf_ref, wab_ref, wgb_ref, wout_ref, g2_ref, wrt_ref, brt_ref, tri_ref,
                  x1_ref, ri_ref, rw_ref, cnt_ref,
                  kl_ref, vt_ref, attnt_ref, gated_ref, base_ref, *, blocks_per_seq):
    step = pl.program_id(0)
    s_blk = step % blocks_per_seq
    ts = x_ref.shape[0]
    nb = ts // BLOCK
    f32, bf16 = jnp.float32, jnp.bfloat16

    @pl.when(step == 0)
    def _():
        base_ref[...] = jnp.zeros_like(base_ref)

    @pl.when(s_blk == 0)
    def _():
        kl_ref[:, 0:BLOCK, :] = jnp.zeros((4, BLOCK, LANES), bf16)
        vt_ref[:, 0:BLOCK] = jnp.zeros((LANES, BLOCK), bf16)

    x = x_ref[...]
    hb = _rms(x, g1_ref[...]).astype(bf16)

    qkv = jnp.dot(hb, win_ref[:, 0:V_END], preferred_element_type=f32)
    rc, rs1, rs2 = rc_ref[...], rs1_ref[...], rs2_ref[...]

    def rope(t):
        return t * rc + pltpu.roll(t, 8, 1) * rs1 + pltpu.roll(t, LANES - 8, 1) * rs2

    scale = 1.0 / math.sqrt(HEAD_DIM)
    qb = [(rope(qkv[:, j * LANES:(j + 1) * LANES]) * scale).astype(bf16) for j in range(4)]
    k = rope(qkv[:, Q_END:K_END])
    v = qkv[:, K_END:V_END]
    k_sw = pltpu.roll(k, HEAD_DIM, 1)
    lo = lax.broadcasted_iota(jnp.int32, (ts, LANES), 1) < HEAD_DIM
    zero = jnp.zeros_like(k)
    kl_ref[0, BLOCK:BLOCK + ts, :] = jnp.where(lo, k, zero).astype(bf16)
    kl_ref[1, BLOCK:BLOCK + ts, :] = jnp.where(lo, zero, k_sw).astype(bf16)
    kl_ref[2, BLOCK:BLOCK + ts, :] = jnp.where(lo, k_sw, zero).astype(bf16)
    kl_ref[3, BLOCK:BLOCK + ts, :] = jnp.where(lo, zero, k).astype(bf16)
    vt_ref[:, BLOCK:BLOCK + ts] = v.T.astype(bf16)

    kj = lax.broadcasted_iota(jnp.int32, (2 * BLOCK, 2 * BLOCK), 0)
    qi = lax.broadcasted_iota(jnp.int32, (2 * BLOCK, 2 * BLOCK), 1) % BLOCK
    band = (kj > qi) & (kj <= qi + BLOCK)
    kmin = jnp.where(s_blk == 0, BLOCK, 0)
    bias_first = jnp.where(band & (kj >= kmin), 0.0, NEG_INF).astype(f32)
    bias_rest = jnp.where(band, 0.0, NEG_INF).astype(f32)
    for i in range(nb):
        bias = bias_first if i == 0 else bias_rest
        rows = slice(i * BLOCK, (i + 1) * BLOCK)
        win = slice(i * BLOCK, (i + 2) * BLOCK)
        for g in range(N_KV_HEADS):
            qs = jnp.concatenate([qb[2 * g][rows], qb[2 * g + 1][rows]], axis=0)
            for p in range(2):
                st = lax.dot_general(kl_ref[2 * g + p, win, :], qs, (((1,), (1,)), ((), ())),
                                     preferred_element_type=f32) + bias
                sink = sink_ref[2 * g + p:2 * g + p + 1, :]
                m = jnp.maximum(jnp.max(st, axis=0, keepdims=True), sink)
                e = jnp.exp(st - m)
                den = jnp.sum(e, axis=0, keepdims=True) + jnp.exp(sink - m)
                ot = jnp.dot(vt_ref[g * HEAD_DIM:(g + 1) * HEAD_DIM, win], e.astype(bf16),
                             preferred_element_type=f32) * (1.0 / den)
                h0 = 4 * g + p
                attnt_ref[h0 * HEAD_DIM:(h0 + 1) * HEAD_DIM, rows] = ot[:, 0:BLOCK]
                attnt_ref[(h0 + 2) * HEAD_DIM:(h0 + 3) * HEAD_DIM, rows] = ot[:, BLOCK:2 * BLOCK]
    kl_ref[:, 0:BLOCK, :] = kl_ref[:, ts:ts + BLOCK, :]
    vt_ref[:, 0:BLOCK] = vt_ref[:, ts:ts + BLOCK]
    a = jnp.dot(attnt_ref[...].T.astype(bf16), wab_ref[...], preferred_element_type=f32)

    z = jax.nn.gelu(jnp.dot(hb, win_ref[:, V_END:Z_END], preferred_element_type=f32))
    u, v2 = z[:, 0:G_WIDTH], z[:, G_WIDTH:2 * G_WIDTH]
    mu = jnp.mean(v2, axis=-1, keepdims=True)
    vc = v2 - mu
    var = jnp.mean(vc * vc, axis=-1, keepdims=True)
    vn = (vc * lax.rsqrt(var + EPS) * lng_ref[...] + lnb_ref[...]).astype(bf16)
    wq = lax.broadcasted_iota(jnp.int32, (BLOCK, 2 * BLOCK), 0)
    wp = lax.broadcasted_iota(jnp.int32, (BLOCK, 2 * BLOCK), 1) % BLOCK
    causal = wp <= wq
    lo_c = lax.broadcasted_iota(jnp.int32, (BLOCK, LANES), 1) < G_GROUP_DIM
    for j in range(G_GROUPS // 2):
        wj = jnp.where(causal, wcat_ref[j], 0.0).astype(bf16)
        cols = slice(j * LANES, (j + 1) * LANES)
        for c in range(nb):
            rows = slice(c * BLOCK, (c + 1) * BLOCK)
            vp = vn[rows, cols]
            zb = jnp.zeros_like(vp)
            rhs = jnp.concatenate([jnp.where(lo_c, vp, zb), jnp.where(lo_c, zb, vp)], axis=0)
            mixed = jnp.dot(wj, rhs, preferred_element_type=f32) + bsf_ref[:, cols]
            gated_ref[rows, cols] = u[rows, cols] * mixed
    gm = jnp.dot(gated_ref[...].astype(bf16), wgb_ref[...], preferred_element_type=f32)

    ga = jnp.dot(hb, win_ref[:, Z_END:GA_END], preferred_element_type=f32)
    mix = jax.nn.sigmoid(ga) * a
    gg = jnp.dot(hb, win_ref[:, GA_END:IN_COLS], preferred_element_type=f32)
    mix = mix + jax.nn.sigmoid(gg) * gm
    x1 = x + jnp.dot(mix.astype(bf16), wout_ref[...], preferred_element_type=f32)
    x1_ref[...] = x1

    xnb = _rms(x1, g2_ref[...]).astype(bf16)
    lt = lax.dot_general(wrt_ref[...], xnb, (((1,), (1,)), ((), ())),
                         preferred_element_type=f32) + brt_ref[...]
    r = [lt[i:i + 1, :] for i in range(N_GROUPS)]
    gm_ = jnp.maximum(jnp.maximum(r[0], r[1]), jnp.maximum(r[2], r[3]))
    gsel = jnp.where(r[0] == gm_, 0.0, jnp.where(r[1] == gm_, 1.0, jnp.where(r[2] == gm_, 2.0, 3.0)))
    g_w = 1.0 / (jnp.exp(r[0] - gm_) + jnp.exp(r[1] - gm_) + jnp.exp(r[2] - gm_) + jnp.exp(r[3] - gm_))
    eg = [lt[8 + 8 * kk:16 + 8 * kk, :] for kk in range(N_GROUPS)]
    ein = jnp.where(gsel == 0.0, eg[0], jnp.where(gsel == 1.0, eg[1], jnp.where(gsel == 2.0, eg[2], eg[3])))
    io8 = lax.broadcasted_iota(jnp.int32, (EXPERTS_PER_GROUP, ts), 0).astype(f32)
    m1 = jnp.max(ein, axis=0, keepdims=True)
    i1 = jnp.min(jnp.where(ein == m1, io8, 8.0), axis=0, keepdims=True)
    e2 = jnp.where(io8 == i1, NEG_INF, ein)
    m2 = jnp.max(e2, axis=0, keepdims=True)
    i2 = jnp.min(jnp.where(e2 == m2, io8, 8.0), axis=0, keepdims=True)
    t2 = jnp.exp(m2 - m1)
    w1 = 1.0 / (1.0 + t2)
    wt0, wt1 = g_w * w1, g_w * (t2 * w1)
    eid0, eid1 = gsel * 8.0 + i1, gsel * 8.0 + i2

    io32 = lax.broadcasted_iota(jnp.int32, (N_EXPERTS, ts), 0).astype(f32)
    oh0, oh1 = io32 == eid0, io32 == eid1
    both = jnp.where(oh0, 1.0, jnp.where(oh1, 1.0, 0.0))
    pref = jnp.dot(both.astype(bf16), tri_ref[...], preferred_element_type=f32)
    tot = base_ref[...] + pref
    rank0 = jnp.sum(jnp.where(oh0, tot, 0.0), axis=0, keepdims=True)
    rank1 = jnp.sum(jnp.where(oh1, tot, 0.0), axis=0, keepdims=True)
    new_base = base_ref[...] + jnp.sum(both, axis=1, keepdims=True)
    base_ref[...] = new_base
    cnt_ref[...] = new_base[:, 0:LANES]

    ri = jnp.where(io8 == 0.0, eid0, jnp.where(io8 == 1.0, eid1,
                   jnp.where(io8 == 2.0, rank0, jnp.where(io8 == 3.0, rank1, 0.0))))
    ri_ref[...] = ri.astype(jnp.int32)
    io128 = lax.broadcasted_iota(jnp.int32, (LANES, ts), 0)
    rw_ref[...] = jnp.where(io128 == 0, wt0, jnp.where(io128 == 1, wt1, 0.0)).T


def _const_spec(shape):
    nd = len(shape)
    return pl.BlockSpec(shape, lambda i: (0,) * nd)


def _mixer(x2, g1, win, rc, rs1, rs2, sinks, lng, lnb, wcat, bsf, wab, wgb, wout, g2, wrt, brt, tri, seq):
    t = x2.shape[0]
    blocks_per_seq = seq // TS
    tok = lambda i: (i, 0)
    pos = lambda i: (i % blocks_per_seq, 0)
    in_specs = [
        pl.BlockSpec((TS, D_MODEL), tok),
        _const_spec(g1.shape), _const_spec(win.shape),
        pl.BlockSpec((TS, LANES), pos), pl.BlockSpec((TS, LANES), pos), pl.BlockSpec((TS, LANES), pos),
        _const_spec(sinks.shape), _const_spec(lng.shape), _const_spec(lnb.shape),
        _const_spec(wcat.shape), _const_spec(bsf.shape), _const_spec(wab.shape), _const_spec(wgb.shape),
        _const_spec(wout.shape), _const_spec(g2.shape), _const_spec(wrt.shape), _const_spec(brt.shape),
        _const_spec(tri.shape),
    ]
    out_shape = (
        jax.ShapeDtypeStruct((t, D_MODEL), jnp.float32),
        jax.ShapeDtypeStruct((8, t), jnp.int32),
        jax.ShapeDtypeStruct((t, LANES), jnp.float32),
        jax.ShapeDtypeStruct((N_EXPERTS, LANES), jnp.float32),
    )
    out_specs = (
        pl.BlockSpec((TS, D_MODEL), tok),
        pl.BlockSpec((8, TS), lambda i: (0, i)),
        pl.BlockSpec((TS, LANES), tok),
        pl.BlockSpec((N_EXPERTS, LANES), lambda i: (0, 0)),
    )
    scratch = [
        pltpu.VMEM((4, BLOCK + TS, LANES), jnp.bfloat16),
        pltpu.VMEM((LANES, BLOCK + TS), jnp.bfloat16),
        pltpu.VMEM((ATTN_WIDTH, TS), jnp.float32),
        pltpu.VMEM((TS, G_WIDTH), jnp.float32),
        pltpu.VMEM((N_EXPERTS, TS), jnp.float32),
    ]
    return pl.pallas_call(
        functools.partial(_mixer_kernel, blocks_per_seq=blocks_per_seq),
        grid=(t // TS,),
        in_specs=in_specs,
        out_specs=out_specs,
        out_shape=out_shape,
        scratch_shapes=scratch,
        compiler_params=pltpu.CompilerParams(dimension_semantics=("arbitrary",),
                                             vmem_limit_bytes=VMEM_LIMIT),
        name="mixer",
    )(x2, g1, win, rc, rs1, rs2, sinks, lng, lnb, wcat, bsf, wab, wgb, wout, g2, wrt, brt, tri)


def _dispatch_kernel(cnt_ref, off_ref, pos_ref, x1_ref, g2_ref, xs_ref, xn_ref, zero_ref, sem, zsem, tsem):
    step = pl.program_id(0)
    tb = x1_ref.shape[0]
    n_tiles_max = xs_ref.shape[0] // TM

    def row_copy(r, dst):
        return pltpu.make_async_copy(xn_ref.at[pl.ds(r, 1)], xs_ref.at[pl.ds(dst, 1)], sem)

    def zero_copy(dst):
        return pltpu.make_async_copy(zero_ref.at[pl.ds(0, 1)], xs_ref.at[pl.ds(dst, 1)], zsem)

    def zero_tile_copy(tile):
        return pltpu.make_async_copy(zero_ref, xs_ref.at[pl.ds(tile * TM, TM)], tsem)

    @pl.when(step == 0)
    def _():
        zero_ref[...] = jnp.zeros_like(zero_ref)

        def per_expert(e, carry):
            cnt = cnt_ref[e]
            pad = (TM - cnt % TM) % TM
            start = off_ref[e] + cnt

            def issue(r, c):
                zero_copy(start + r).start()
                return c

            lax.fori_loop(0, pad, issue, 0)

            def drain(r, c):
                zero_copy(0).wait()
                return c

            lax.fori_loop(0, pad, drain, 0)
            return carry + (cnt + pad) // TM

        used_tiles = lax.fori_loop(0, N_EXPERTS, per_expert, 0)

        def issue_tail(j, c):
            zero_tile_copy(j).start()
            return c

        lax.fori_loop(used_tiles, n_tiles_max, issue_tail, 0)

        def drain_tail(j, c):
            zero_tile_copy(0).wait()
            return c

        lax.fori_loop(used_tiles, n_tiles_max, drain_tail, 0)

    xn_ref[...] = _rms(x1_ref[...], g2_ref[...])

    def issue(r, c):
        row_copy(r, pos_ref[0, 0, r]).start()
        row_copy(r, pos_ref[0, 1, r]).start()
        return c

    lax.fori_loop(0, tb, issue, 0, unroll=8)
    for _ in range(TOP_K):
        pltpu.make_async_copy(xn_ref, xs_ref.at[pl.ds(0, tb)], sem).wait()


def _dispatch(counts, offsets, pos3, x1, g2, n_rows):
    t = x1.shape[0]
    grid_spec = pltpu.PrefetchScalarGridSpec(
        num_scalar_prefetch=2,
        grid=(t // TB,),
        in_specs=[
            pl.BlockSpec((1, TOP_K, TB), lambda i, *_: (i, 0, 0), memory_space=pltpu.SMEM),
            pl.BlockSpec((TB, D_MODEL), lambda i, *_: (i, 0)),
            pl.BlockSpec((1, D_MODEL), lambda i, *_: (0, 0)),
        ],
        out_specs=pl.BlockSpec(memory_space=pl.ANY),
        scratch_shapes=[
            pltpu.VMEM((TB, D_MODEL), jnp.float32),
            pltpu.VMEM((TM, D_MODEL), jnp.float32),
            pltpu.SemaphoreType.DMA,
            pltpu.SemaphoreType.DMA,
            pltpu.SemaphoreType.DMA,
        ],
    )
    return pl.pallas_call(
        _dispatch_kernel,
        grid_spec=grid_spec,
        out_shape=jax.ShapeDtypeStruct((n_rows, D_MODEL), jnp.float32),
        compiler_params=pltpu.CompilerParams(dimension_semantics=("arbitrary",),
                                             vmem_limit_bytes=VMEM_LIMIT),
        name="dispatch",
    )(counts, offsets, pos3, x1, g2)


def _expert_kernel(te_ref, nt_ref, xs_ref, wg_ref, wu_ref, wd_ref, ys_ref, wgb_ref, wub_ref, wdb_ref):
    j = pl.program_id(0)
    active = j < nt_ref[0]

    @pl.when((j == 0) | (te_ref[j] != te_ref[jnp.maximum(j - 1, 0)]))
    def _():
        wgb_ref[...] = wg_ref[0].astype(jnp.bfloat16)
        wub_ref[...] = wu_ref[0].astype(jnp.bfloat16)
        wdb_ref[...] = wd_ref[0].astype(jnp.bfloat16)

    @pl.when(active)
    def _():
        xb = xs_ref[...].astype(jnp.bfloat16)
        g = jnp.dot(xb, wgb_ref[...], preferred_element_type=jnp.float32)
        u = jnp.dot(xb, wub_ref[...], preferred_element_type=jnp.float32)
        hid = (jax.nn.silu(g) * u).astype(jnp.bfloat16)
        ys_ref[...] = jnp.dot(hid, wdb_ref[...], preferred_element_type=jnp.float32)

    @pl.when(jnp.logical_not(active))
    def _():
        ys_ref[...] = jnp.zeros_like(ys_ref)


def _experts(tile_expert, n_tiles, xs, wg, wu, wd):
    n_rows = xs.shape[0]
    max_tiles = n_rows // TM
    row = lambda j, te, nt: (jnp.minimum(j, nt[0] - 1), 0)
    wsel = lambda j, te, nt: (te[j], 0, 0)
    grid_spec = pltpu.PrefetchScalarGridSpec(
        num_scalar_prefetch=2,
        grid=(max_tiles,),
        in_specs=[
            pl.BlockSpec((TM, D_MODEL), row),
            pl.BlockSpec((1, D_MODEL, D_EXPERT), wsel),
            pl.BlockSpec((1, D_MODEL, D_EXPERT), wsel),
            pl.BlockSpec((1, D_EXPERT, D_MODEL), wsel),
        ],
        out_specs=pl.BlockSpec((TM, D_MODEL), lambda j, te, nt: (j, 0)),
        scratch_shapes=[
            pltpu.VMEM((D_MODEL, D_EXPERT), jnp.bfloat16),
            pltpu.VMEM((D_MODEL, D_EXPERT), jnp.bfloat16),
            pltpu.VMEM((D_EXPERT, D_MODEL), jnp.bfloat16),
        ],
    )
    return pl.pallas_call(
        _expert_kernel,
        grid_spec=grid_spec,
        out_shape=jax.ShapeDtypeStruct((n_rows, D_MODEL), jnp.float32),
        compiler_params=pltpu.CompilerParams(dimension_semantics=("arbitrary",),
                                             vmem_limit_bytes=VMEM_LIMIT),
        name="experts",
    )(tile_expert, n_tiles, xs, wg, wu, wd)


def _combine_kernel(pos_ref, x1_ref, rw_ref, gf_ref, ys_ref, o_ref, y_ref, sem):
    tb = x1_ref.shape[0]

    def row_copy(k, r, src):
        return pltpu.make_async_copy(ys_ref.at[pl.ds(src, 1)], y_ref.at[k, pl.ds(r, 1)], sem)

    def issue(r, c):
        row_copy(0, r, pos_ref[0, 0, r]).start()
        row_copy(1, r, pos_ref[0, 1, r]).start()
        return c

    lax.fori_loop(0, tb, issue, 0, unroll=8)
    for k in range(TOP_K):
        pltpu.make_async_copy(ys_ref.at[pl.ds(0, tb)], y_ref.at[k], sem).wait()
    rw = rw_ref[...]
    x2 = x1_ref[...] + rw[:, 0:1] * y_ref[0] + rw[:, 1:2] * y_ref[1]
    o_ref[...] = _rms(x2, gf_ref[...])


def _combine(pos3, x1, rw, gf, ys):
    t = x1.shape[0]
    grid_spec = pltpu.PrefetchScalarGridSpec(
        num_scalar_prefetch=0,
        grid=(t // TB,),
        in_specs=[
            pl.BlockSpec((1, TOP_K, TB), lambda i: (i, 0, 0), memory_space=pltpu.SMEM),
            pl.BlockSpec((TB, D_MODEL), lambda i: (i, 0)),
            pl.BlockSpec((TB, LANES), lambda i: (i, 0)),
            pl.BlockSpec((1, D_MODEL), lambda i: (0, 0)),
            pl.BlockSpec(memory_space=pl.ANY),
        ],
        out_specs=pl.BlockSpec((TB, D_MODEL), lambda i: (i, 0)),
        scratch_shapes=[
            pltpu.VMEM((TOP_K, TB, D_MODEL), jnp.float32),
            pltpu.SemaphoreType.DMA,
        ],
    )
    return pl.pallas_call(
        _combine_kernel,
        grid_spec=grid_spec,
        out_shape=jax.ShapeDtypeStruct((t, D_MODEL), jnp.float32),
        compiler_params=pltpu.CompilerParams(dimension_semantics=("arbitrary",),
                                             vmem_limit_bytes=VMEM_LIMIT),
        name="combine",
    )(pos3, x1, rw, gf, ys)


def _rope_lane_tables(seq):
    inv_freq = ROPE_THETA ** (-jnp.arange(0, ROT_DIM, 2, dtype=jnp.float32) / ROT_DIM)
    ang = jnp.arange(seq, dtype=jnp.float32)[:, None] * inv_freq[None, :]
    cos, sin = jnp.cos(ang), jnp.sin(ang)
    half = ROT_DIM // 2
    ones = jnp.ones((seq, HEAD_DIM - ROT_DIM), jnp.float32)
    zeros = jnp.zeros((seq, HEAD_DIM - ROT_DIM), jnp.float32)
    zh = jnp.zeros((seq, half), jnp.float32)
    c = jnp.concatenate([cos, cos, ones], axis=1)
    s1 = jnp.concatenate([zh, sin, zeros], axis=1)
    s2 = jnp.concatenate([-sin, zh, zeros], axis=1)
    rep = LANES // HEAD_DIM
    return jnp.tile(c, (1, rep)), jnp.tile(s1, (1, rep)), jnp.tile(s2, (1, rep))


def kernel(x, norm1_g, w_in, attn_sinks, gmlp_ln_g, gmlp_ln_b, gmlp_ws, gmlp_bs, w_attn_branch,
           w_gmlp_branch, w_out, norm2_g, router_group_w, router_group_b, router_expert_w,
           router_expert_b, expert_w_gate, expert_w_up, expert_w_down, final_norm_g):
    b, s, d = x.shape
    assert d == D_MODEL and s % TS == 0 and (b * s) % TB == 0 and norm1_g.shape[0] == 1
    t = b * s
    bf16, f32 = jnp.bfloat16, jnp.float32
    x2 = x.reshape(t, d)

    rc, rs1, rs2 = _rope_lane_tables(s)
    sk = attn_sinks[0].astype(f32)
    sinks = jnp.stack([
        jnp.concatenate([jnp.full((BLOCK,), 1.0, f32) * sk[4 * g + p], jnp.full((BLOCK,), 1.0, f32) * sk[4 * g + 2 + p]])
        for g in range(N_KV_HEADS) for p in range(2)])
    ws = gmlp_ws[0]
    wcat = jnp.stack([jnp.concatenate([ws[2 * j], ws[2 * j + 1]], axis=1) for j in range(G_GROUPS // 2)])
    bsf = jnp.repeat(gmlp_bs[0].T, G_GROUP_DIM, axis=1)
    wrt = jnp.concatenate([router_group_w[0].T, jnp.zeros((8 - N_GROUPS, d), f32), router_expert_w[0].T], axis=0)
    brt = jnp.concatenate([router_group_b[0], jnp.zeros((8 - N_GROUPS,), f32), router_expert_b[0]])
    brt = jnp.broadcast_to(brt[:, None], (ROUTER_ROWS, TS))
    tri = (jnp.arange(TS)[:, None] < jnp.arange(TS)[None, :]).astype(bf16)

    x1, ri, rw, cnt = _mixer(
        x2, norm1_g, w_in[0].astype(bf16), rc, rs1, rs2, sinks, gmlp_ln_g, gmlp_ln_b, wcat, bsf,
        w_attn_branch[0].astype(bf16), w_gmlp_branch[0].astype(bf16), w_out[0].astype(bf16),
        norm2_g, wrt.astype(bf16), brt, tri, s)

    counts = cnt[:, 0].astype(jnp.int32)
    padded = ((counts + TM - 1) // TM) * TM
    ends = jnp.cumsum(padded)
    offsets = ends - padded
    expert_ids = jnp.arange(N_EXPERTS, dtype=jnp.int32)
    is_e = ri[0:TOP_K][None] == expert_ids[:, None, None]
    pos = jnp.sum(jnp.where(is_e, offsets[:, None, None], 0), axis=0) + ri[TOP_K:2 * TOP_K]
    pos3 = pos.reshape(TOP_K, t // TB, TB).transpose(1, 0, 2)
    n_rows = t * TOP_K + N_EXPERTS * TM
    max_tiles = n_rows // TM
    n_tiles = (ends[-1] // TM).astype(jnp.int32)
    tile_start = jnp.arange(max_tiles, dtype=jnp.int32) * TM
    tile_expert = jnp.sum(ends[None, :] <= tile_start[:, None], axis=1)
    tile_expert = jnp.minimum(tile_expert, N_EXPERTS - 1).astype(jnp.int32)
    last_expert = tile_expert[jnp.maximum(n_tiles - 1, 0)]
    tile_expert = jnp.where(jnp.arange(max_tiles) < n_tiles, tile_expert, last_expert)

    xs = _dispatch(counts, offsets.astype(jnp.int32), pos3, x1, norm2_g, n_rows)
    ys = _experts(tile_expert, n_tiles.reshape(1), xs, expert_w_gate[0], expert_w_up[0], expert_w_down[0])
    out = _combine(pos3, x1, rw, final_norm_g.reshape(1, d), ys)
    return out.reshape(b, s, d)
```

```python
import functools
import math

import jax
import jax.numpy as jnp
from jax import lax
from jax.experimental import pallas as pl
from jax.experimental.pallas import tpu as pltpu

D_MODEL = 1024
HEAD_DIM = 64
N_HEADS = 8
N_KV_HEADS = 2
BLOCK = 128
ROT_DIM = HEAD_DIM // 4
ROPE_THETA = 500000.0
ATTN_WIDTH = N_HEADS * HEAD_DIM
KV_WIDTH = N_KV_HEADS * HEAD_DIM
G_GROUPS = 8
G_GROUP_DIM = 64
G_WIDTH = G_GROUPS * G_GROUP_DIM
Q_END = ATTN_WIDTH
K_END = Q_END + KV_WIDTH
V_END = K_END + KV_WIDTH
Z_END = V_END + 2 * G_WIDTH
GA_END = Z_END + D_MODEL
IN_COLS = GA_END + D_MODEL
N_GROUPS = 4
EXPERTS_PER_GROUP = 8
N_EXPERTS = N_GROUPS * EXPERTS_PER_GROUP
TOP_K = 2
D_EXPERT = 512
EPS = 1e-5
NEG_INF = -1e30

LANES = 128
ROW_ALIGN = 8
ROUTER_ROWS = 8 + N_EXPERTS

TS = 512
TM = 256
COMPACT_ROWS = 256
BLOCK_ROWS = -(-(TOP_K * TS + N_EXPERTS * (ROW_ALIGN - 1)) // COMPACT_ROWS) * COMPACT_ROWS
VMEM_LIMIT = 58 * 1024 * 1024


def _rms(x, g):
    return x * lax.rsqrt(jnp.mean(x * x, axis=-1, keepdims=True) + EPS) * g


def _mixer_kernel(x_ref, g1_ref, win_ref, rc_ref, rs1_ref, rs2_ref, sink_ref, lng_ref, lnb_ref,
                  wcat_ref, bsf_ref, wab_ref, wgb_ref, wout_ref, g2_ref, wrt_ref, brt_ref, tri_ref, ltri_ref,
                  x1_ref, xs_ref, rw_ref, cnt_ref,
                  kl_ref, vt_ref, attnt_ref, gated_ref, *, blocks_per_seq):
    step = pl.program_id(0)
    s_blk = step % blocks_per_seq
    ts = x_ref.shape[0]
    nb = ts // BLOCK
    f32, bf16 = jnp.float32, jnp.bfloat16

    @pl.when(s_blk == 0)
    def _():
        kl_ref[:, 0:BLOCK, :] = jnp.zeros((4, BLOCK, LANES), bf16)
        vt_ref[:, 0:BLOCK] = jnp.zeros((LANES, BLOCK), bf16)

    x = x_ref[...]
    hb = _rms(x, g1_ref[...]).astype(bf16)

    qkv = jnp.dot(hb, win_ref[:, 0:V_END], preferred_element_type=f32)
    rc, rs1, rs2 = rc_ref[...], rs1_ref[...], rs2_ref[...]

    def rope(t):
        return t * rc + pltpu.roll(t, 8, 1) * rs1 + pltpu.roll(t, LANES - 8, 1) * rs2

    scale = 1.0 / math.sqrt(HEAD_DIM)
    qb = [(rope(qkv[:, j * LANES:(j + 1) * LANES]) * scale).astype(bf16) for j in range(4)]
    k = rope(qkv[:, Q_END:K_END])
    v = qkv[:, K_END:V_END]
    k_sw = pltpu.roll(k, HEAD_DIM, 1)
    lo = lax.broadcasted_iota(jnp.int32, (ts, LANES), 1) < HEAD_DIM
    zero = jnp.zeros_like(k)
    kl_ref[0, BLOCK:BLOCK + ts, :] = jnp.where(lo, k, zero).astype(bf16)
    kl_ref[1, BLOCK:BLOCK + ts, :] = jnp.where(lo, zero, k_sw).astype(bf16)
    kl_ref[2, BLOCK:BLOCK + ts, :] = jnp.where(lo, k_sw, zero).astype(bf16)
    kl_ref[3, BLOCK:BLOCK + ts, :] = jnp.where(lo, zero, k).astype(bf16)
    vt_ref[:, BLOCK:BLOCK + ts] = v.T.astype(bf16)

    kj = lax.broadcasted_iota(jnp.int32, (2 * BLOCK, 2 * BLOCK), 0)
    qi = lax.broadcasted_iota(jnp.int32, (2 * BLOCK, 2 * BLOCK), 1) % BLOCK
    band = (kj > qi) & (kj <= qi + BLOCK)
    kmin = jnp.where(s_blk == 0, BLOCK, 0)
    bias_first = jnp.where(band & (kj >= kmin), 0.0, NEG_INF).astype(f32)
    bias_rest = jnp.where(band, 0.0, NEG_INF).astype(f32)
    for i in range(nb):
        bias = bias_first if i == 0 else bias_rest
        rows = slice(i * BLOCK, (i + 1) * BLOCK)
        win = slice(i * BLOCK, (i + 2) * BLOCK)
        for g in range(N_KV_HEADS):
            qs = jnp.concatenate([qb[2 * g][rows], qb[2 * g + 1][rows]], axis=0)
            for p in range(2):
                st = lax.dot_general(kl_ref[2 * g + p, win, :], qs, (((1,), (1,)), ((), ())),
                                     preferred_element_type=f32) + bias
                sink = sink_ref[2 * g + p:2 * g + p + 1, :]
                m = jnp.maximum(jnp.max(st, axis=0, keepdims=True), sink)
                e = jnp.exp(st - m)
                den = jnp.sum(e, axis=0, keepdims=True) + jnp.exp(sink - m)
                ot = jnp.dot(vt_ref[g * HEAD_DIM:(g + 1) * HEAD_DIM, win], e.astype(bf16),
                             preferred_element_type=f32) * (1.0 / den)
                h0 = 4 * g + p
                attnt_ref[h0 * HEAD_DIM:(h0 + 1) * HEAD_DIM, rows] = ot[:, 0:BLOCK]
                attnt_ref[(h0 + 2) * HEAD_DIM:(h0 + 3) * HEAD_DIM, rows] = ot[:, BLOCK:2 * BLOCK]
    kl_ref[:, 0:BLOCK, :] = kl_ref[:, ts:ts + BLOCK, :]
    vt_ref[:, 0:BLOCK] = vt_ref[:, ts:ts + BLOCK]
    a = jnp.dot(attnt_ref[...].T.astype(bf16), wab_ref[...], preferred_element_type=f32)

    z = jax.nn.gelu(jnp.dot(hb, win_ref[:, V_END:Z_END], preferred_element_type=f32))
    u, v2 = z[:, 0:G_WIDTH], z[:, G_WIDTH:2 * G_WIDTH]
    mu = jnp.mean(v2, axis=-1, keepdims=True)
    vc = v2 - mu
    var = jnp.mean(vc * vc, axis=-1, keepdims=True)
    vn = (vc * lax.rsqrt(var + EPS) * lng_ref[...] + lnb_ref[...]).astype(bf16)
    wq = lax.broadcasted_iota(jnp.int32, (BLOCK, 2 * BLOCK), 0)
    wp = lax.broadcasted_iota(jnp.int32, (BLOCK, 2 * BLOCK), 1) % BLOCK
    causal = wp <= wq
    lo_c = lax.broadcasted_iota(jnp.int32, (BLOCK, LANES), 1) < G_GROUP_DIM
    for j in range(G_GROUPS // 2):
        wj = jnp.where(causal, wcat_ref[j], 0.0).astype(bf16)
        cols = slice(j * LANES, (j + 1) * LANES)
        for c in range(nb):
            rows = slice(c * BLOCK, (c + 1) * BLOCK)
            vp = vn[rows, cols]
            zb = jnp.zeros_like(vp)
            rhs = jnp.concatenate([jnp.where(lo_c, vp, zb), jnp.where(lo_c, zb, vp)], axis=0)
            mixed = jnp.dot(wj, rhs, preferred_element_type=f32) + bsf_ref[:, cols]
            gated_ref[rows, cols] = u[rows, cols] * mixed
    gm = jnp.dot(gated_ref[...].astype(bf16), wgb_ref[...], preferred_element_type=f32)

    ga = jnp.dot(hb, win_ref[:, Z_END:GA_END], preferred_element_type=f32)
    mix = jax.nn.sigmoid(ga) * a
    gg = jnp.dot(hb, win_ref[:, GA_END:IN_COLS], preferred_element_type=f32)
    mix = mix + jax.nn.sigmoid(gg) * gm
    x1 = x + jnp.dot(mix.astype(bf16), wout_ref[...], preferred_element_type=f32)
    x1_ref[...] = x1

    xnb = _rms(x1, g2_ref[...]).astype(bf16)
    lt = lax.dot_general(wrt_ref[...], xnb, (((1,), (1,)), ((), ())),
                         preferred_element_type=f32) + brt_ref[...]
    r = [lt[i:i + 1, :] for i in range(N_GROUPS)]
    gm_ = jnp.maximum(jnp.maximum(r[0], r[1]), jnp.maximum(r[2], r[3]))
    gsel = jnp.where(r[0] == gm_, 0.0, jnp.where(r[1] == gm_, 1.0, jnp.where(r[2] == gm_, 2.0, 3.0)))
    g_w = 1.0 / (jnp.exp(r[0] - gm_) + jnp.exp(r[1] - gm_) + jnp.exp(r[2] - gm_) + jnp.exp(r[3] - gm_))
    eg = [lt[8 + 8 * kk:16 + 8 * kk, :] for kk in range(N_GROUPS)]
    ein = jnp.where(gsel == 0.0, eg[0], jnp.where(gsel == 1.0, eg[1], jnp.where(gsel == 2.0, eg[2], eg[3])))
    io8 = lax.broadcasted_iota(jnp.int32, (EXPERTS_PER_GROUP, ts), 0).astype(f32)
    m1 = jnp.max(ein, axis=0, keepdims=True)
    i1 = jnp.min(jnp.where(ein == m1, io8, 8.0), axis=0, keepdims=True)
    e2 = jnp.where(io8 == i1, NEG_INF, ein)
    m2 = jnp.max(e2, axis=0, keepdims=True)
    i2 = jnp.min(jnp.where(e2 == m2, io8, 8.0), axis=0, keepdims=True)
    t2 = jnp.exp(m2 - m1)
    w1 = 1.0 / (1.0 + t2)
    wt0, wt1 = g_w * w1, g_w * (t2 * w1)
    eid0, eid1 = gsel * 8.0 + i1, gsel * 8.0 + i2

    io32 = lax.broadcasted_iota(jnp.int32, (N_EXPERTS, ts), 0).astype(f32)
    oh0, oh1 = io32 == eid0, io32 == eid1
    both = jnp.where(oh0, 1.0, jnp.where(oh1, 1.0, 0.0))
    pref = jnp.dot(both.astype(bf16), tri_ref[...], preferred_element_type=f32)
    n_e = jnp.sum(both, axis=1, keepdims=True)
    cnt_ref[0] = jnp.broadcast_to(n_e, (N_EXPERTS, LANES))
    units = jnp.broadcast_to(jnp.floor((n_e + (ROW_ALIGN - 1.0)) * (1.0 / ROW_ALIGN)), (N_EXPERTS, ts))
    units = jnp.concatenate([units, jnp.zeros((LANES - N_EXPERTS, ts), f32)], axis=0).astype(bf16)
    tot = ROW_ALIGN * jnp.dot(ltri_ref[...], units, preferred_element_type=f32) + pref
    slot0 = jnp.sum(jnp.where(oh0, tot, 0.0), axis=0, keepdims=True)
    slot1 = jnp.sum(jnp.where(oh1, tot, 0.0), axis=0, keepdims=True)

    io128 = lax.broadcasted_iota(jnp.int32, (LANES, ts), 0)
    rw_ref[...] = jnp.where(io128 == 0, wt0, jnp.where(io128 == 1, wt1,
                            jnp.where(io128 == 2, slot0, jnp.where(io128 == 3, slot1, 0.0)))).T

    s0i, s1i = slot0.astype(jnp.int32), slot1.astype(jnp.int32)
    for c in range(xs_ref.shape[0] // COMPACT_ROWS):
        io = lax.broadcasted_iota(jnp.int32, (COMPACT_ROWS, ts), 0) + c * COMPACT_ROWS
        onehot = jnp.where(io == s0i, 1.0, jnp.where(io == s1i, 1.0, 0.0)).astype(bf16)
        xs_ref[c * COMPACT_ROWS:(c + 1) * COMPACT_ROWS, :] = jnp.dot(onehot, xnb, preferred_element_type=f32)


def _const_spec(shape):
    nd = len(shape)
    return pl.BlockSpec(shape, lambda i: (0,) * nd)


def _mixer(x2, g1, win, rc, rs1, rs2, sinks, lng, lnb, wcat, bsf, wab, wgb, wout, g2, wrt, brt, tri, ltri, seq):
    t = x2.shape[0]
    n_blocks = t // TS
    blocks_per_seq = seq // TS
    tok = lambda i: (i, 0)
    pos = lambda i: (i % blocks_per_seq, 0)
    consts = (sinks, lng, lnb, wcat, bsf, wab, wgb, wout, g2, wrt, brt, tri, ltri)
    in_specs = [
        pl.BlockSpec((TS, D_MODEL), tok),
        _const_spec(g1.shape), _const_spec(win.shape),
        pl.BlockSpec((TS, LANES), pos), pl.BlockSpec((TS, LANES), pos), pl.BlockSpec((TS, LANES), pos),
    ] + [_const_spec(c.shape) for c in consts]
    out_shape = (
        jax.ShapeDtypeStruct((t, D_MODEL), jnp.float32),
        jax.ShapeDtypeStruct((n_blocks * BLOCK_ROWS, D_MODEL), jnp.float32),
        jax.ShapeDtypeStruct((t, LANES), jnp.float32),
        jax.ShapeDtypeStruct((n_blocks, N_EXPERTS, LANES), jnp.float32),
    )
    out_specs = (
        pl.BlockSpec((TS, D_MODEL), tok),
        pl.BlockSpec((BLOCK_ROWS, D_MODEL), tok),
        pl.BlockSpec((TS, LANES), tok),
        pl.BlockSpec((1, N_EXPERTS, LANES), lambda i: (i, 0, 0)),
    )
    scratch = [
        pltpu.VMEM((4, BLOCK + TS, LANES), jnp.bfloat16),
        pltpu.VMEM((LANES, BLOCK + TS), jnp.bfloat16),
        pltpu.VMEM((ATTN_WIDTH, TS), jnp.float32),
        pltpu.VMEM((TS, G_WIDTH), jnp.float32),
    ]
    return pl.pallas_call(
        functools.partial(_mixer_kernel, blocks_per_seq=blocks_per_seq),
        grid=(n_blocks,),
        in_specs=in_specs,
        out_specs=out_specs,
        out_shape=out_shape,
        scratch_shapes=scratch,
        compiler_params=pltpu.CompilerParams(dimension_semantics=("arbitrary",),
                                             vmem_limit_bytes=VMEM_LIMIT),
        name="mixer",
    )(x2, g1, win, rc, rs1, rs2, *consts)


def _expert_kernel(te_ref, tr0_ref, tb0_ref, tb1_ref, trows_ref, nt_ref, cum_ref, src_ref, used_ref,
                   xs_ref, wg_ref, wu_ref, wd_ref, ys_ref,
                   xbuf_ref, ybuf_ref, wgb_ref, wub_ref, wdb_ref, zero_ref, isem, osem, zsem,
                   *, n_blocks):
    j = pl.program_id(0)
    n_tiles = nt_ref[0]
    slot = j % 2

    def for_each_run(tile, fn):
        e, r0 = te_ref[tile], tr0_ref[tile]

        def body(b, c):
            start = cum_ref[e * (n_blocks + 1) + b]
            lo = jnp.maximum(start, r0)
            n = jnp.minimum(cum_ref[e * (n_blocks + 1) + b + 1], r0 + TM) - lo

            @pl.when(n > 0)
            def _():
                fn(pl.multiple_of(src_ref[e * n_blocks + b] + (lo - start), ROW_ALIGN),
                   pl.multiple_of(lo - r0, ROW_ALIGN), pl.multiple_of(n, ROW_ALIGN))
            return c

        lax.fori_loop(tb0_ref[tile], tb1_ref[tile], body, 0)

    def gather(tile, s):
        for_each_run(tile, lambda src, dst, n: pltpu.make_async_copy(
            xs_ref.at[pl.ds(src, n)], xbuf_ref.at[s, pl.ds(dst, n)], isem.at[s]).start())

    def scatter(tile, s):
        for_each_run(tile, lambda src, dst, n: pltpu.make_async_copy(
            ybuf_ref.at[s, pl.ds(dst, n)], ys_ref.at[pl.ds(src, n)], osem.at[s]).start())

    def wait_rows(sem, n):
        @pl.when(n > 0)
        def _():
            rows = pl.multiple_of(n, ROW_ALIGN)
            pltpu.make_async_copy(xs_ref.at[pl.ds(0, rows)], xbuf_ref.at[0, pl.ds(0, rows)], sem).wait()

    def zero_tail_copy(b, c):
        first = used_ref[b] + c * TM
        n = jnp.minimum(BLOCK_ROWS - first, TM)
        return pltpu.make_async_copy(
            zero_ref.at[pl.ds(0, pl.multiple_of(n, ROW_ALIGN))],
            ys_ref.at[pl.ds(pl.multiple_of(b * BLOCK_ROWS + first, ROW_ALIGN), pl.multiple_of(n, ROW_ALIGN))], zsem)

    @pl.when(j == 0)
    def _():
        xbuf_ref[...] = jnp.zeros_like(xbuf_ref)
        zero_ref[...] = jnp.zeros_like(zero_ref)

        def per_block(b, carry):
            chunks = (BLOCK_ROWS - used_ref[b] + TM - 1) // TM
            lax.fori_loop(0, chunks, lambda c, x: (zero_tail_copy(b, c).start(), x)[1], 0)
            lax.fori_loop(0, chunks, lambda c, x: (zero_tail_copy(b, c).wait(), x)[1], 0)
            return carry

        lax.fori_loop(0, n_blocks, per_block, 0)

        @pl.when(n_tiles > 0)
        def _():
            gather(0, 0)

    @pl.when(j + 1 < n_tiles)
    def _():
        gather(j + 1, 1 - slot)

    @pl.when((j >= 2) & (j - 2 < n_tiles))
    def _():
        wait_rows(osem.at[slot], trows_ref[jnp.maximum(j - 2, 0)])

    @pl.when(j < n_tiles)
    def _():
        @pl.when((j == 0) | (te_ref[j] != te_ref[jnp.maximum(j - 1, 0)]))
        def _():
            wgb_ref[...] = wg_ref[0].astype(jnp.bfloat16)
            wub_ref[...] = wu_ref[0].astype(jnp.bfloat16)
            wdb_ref[...] = wd_ref[0].astype(jnp.bfloat16)

        wait_rows(isem.at[slot], trows_ref[j])
        xb = xbuf_ref[slot].astype(jnp.bfloat16)
        g = jnp.dot(xb, wgb_ref[...], preferred_element_type=jnp.float32)
        u = jnp.dot(xb, wub_ref[...], preferred_element_type=jnp.float32)
        hid = (jax.nn.silu(g) * u).astype(jnp.bfloat16)
        ybuf_ref[slot] = jnp.dot(hid, wdb_ref[...], preferred_element_type=jnp.float32)
        scatter(j, slot)


def _experts(tables, xs, wg, wu, wd, n_blocks, grid_tiles):
    wsel = lambda j, te, *_: (te[j], 0, 0)
    grid_spec = pltpu.PrefetchScalarGridSpec(
        num_scalar_prefetch=len(tables),
        grid=(grid_tiles,),
        in_specs=[
            pl.BlockSpec(memory_space=pl.ANY),
            pl.BlockSpec((1, D_MODEL, D_EXPERT), wsel),
            pl.BlockSpec((1, D_MODEL, D_EXPERT), wsel),
            pl.BlockSpec((1, D_EXPERT, D_MODEL), wsel),
        ],
        out_specs=pl.BlockSpec(memory_space=pl.ANY),
        scratch_shapes=[
            pltpu.VMEM((2, TM, D_MODEL), jnp.float32),
            pltpu.VMEM((2, TM, D_MODEL), jnp.float32),
            pltpu.VMEM((D_MODEL, D_EXPERT), jnp.bfloat16),
            pltpu.VMEM((D_MODEL, D_EXPERT), jnp.bfloat16),
            pltpu.VMEM((D_EXPERT, D_MODEL), jnp.bfloat16),
            pltpu.VMEM((TM, D_MODEL), jnp.float32),
            pltpu.SemaphoreType.DMA((2,)),
            pltpu.SemaphoreType.DMA((2,)),
            pltpu.SemaphoreType.DMA,
        ],
    )
    return pl.pallas_call(
        functools.partial(_expert_kernel, n_blocks=n_blocks),
        grid_spec=grid_spec,
        out_shape=jax.ShapeDtypeStruct(xs.shape, jnp.float32),
        compiler_params=pltpu.CompilerParams(dimension_semantics=("arbitrary",),
                                             vmem_limit_bytes=VMEM_LIMIT),
        name="experts",
    )(*tables, xs, wg, wu, wd)


def _combine_kernel(x1_ref, rw_ref, gf_ref, ys_ref, o_ref):
    ts = x1_ref.shape[0]
    rw = rw_ref[...]
    io = lax.broadcasted_iota(jnp.int32, (ts, BLOCK_ROWS), 1).astype(jnp.float32)
    wsel = jnp.where(io == rw[:, 2:3], rw[:, 0:1], 0.0) + jnp.where(io == rw[:, 3:4], rw[:, 1:2], 0.0)
    y = jnp.dot(wsel.astype(jnp.bfloat16), ys_ref[...].astype(jnp.bfloat16), preferred_element_type=jnp.float32)
    o_ref[...] = _rms(x1_ref[...] + y, gf_ref[...])


def _combine(x1, rw, gf, ys):
    t = x1.shape[0]
    tok = lambda i: (i, 0)
    return pl.pallas_call(
        _combine_kernel,
        grid=(t // TS,),
        in_specs=[
            pl.BlockSpec((TS, D_MODEL), tok),
            pl.BlockSpec((TS, LANES), tok),
            pl.BlockSpec((1, D_MODEL), lambda i: (0, 0)),
            pl.BlockSpec((BLOCK_ROWS, D_MODEL), tok),
        ],
        out_specs=pl.BlockSpec((TS, D_MODEL), tok),
        out_shape=jax.ShapeDtypeStruct((t, D_MODEL), jnp.float32),
        compiler_params=pltpu.CompilerParams(dimension_semantics=("arbitrary",),
                                             vmem_limit_bytes=VMEM_LIMIT),
        name="combine",
    )(x1, rw, gf, ys)


def _rope_lane_tables(seq):
    inv_freq = ROPE_THETA ** (-jnp.arange(0, ROT_DIM, 2, dtype=jnp.float32) / ROT_DIM)
    ang = jnp.arange(seq, dtype=jnp.float32)[:, None] * inv_freq[None, :]
    cos, sin = jnp.cos(ang), jnp.sin(ang)
    half = ROT_DIM // 2
    ones = jnp.ones((seq, HEAD_DIM - ROT_DIM), jnp.float32)
    zeros = jnp.zeros((seq, HEAD_DIM - ROT_DIM), jnp.float32)
    zh = jnp.zeros((seq, half), jnp.float32)
    c = jnp.concatenate([cos, cos, ones], axis=1)
    s1 = jnp.concatenate([zh, sin, zeros], axis=1)
    s2 = jnp.concatenate([-sin, zh, zeros], axis=1)
    rep = LANES // HEAD_DIM
    return jnp.tile(c, (1, rep)), jnp.tile(s1, (1, rep)), jnp.tile(s2, (1, rep))


def _pick(table, idx):
    return jnp.sum(jnp.where(idx[:, None] == jnp.arange(table.shape[0])[None, :], table[None, :], 0), axis=1)


def _tile_tables(cnt, n_blocks, grid_tiles):
    i32 = jnp.int32
    n = cnt[:, :, 0].astype(i32)
    n8 = (n + ROW_ALIGN - 1) // ROW_ALIGN * ROW_ALIGN
    loc = jnp.cumsum(n8, axis=1) - n8
    used = jnp.sum(n8, axis=1)
    cum = jnp.concatenate([jnp.zeros((1, N_EXPERTS), i32), jnp.cumsum(n8, axis=0)], axis=0).T
    total = cum[:, -1]
    src = (jnp.arange(n_blocks, dtype=i32)[:, None] * BLOCK_ROWS + loc).T
    tiles_e = (total + TM - 1) // TM
    tile_end = jnp.cumsum(tiles_e)
    n_tiles = tile_end[-1]
    j = jnp.arange(grid_tiles, dtype=i32)
    te = jnp.minimum(jnp.sum(tile_end[None, :] <= j[:, None], axis=1), N_EXPERTS - 1).astype(i32)
    live = j < n_tiles
    te = jnp.where(live, te, _pick(te, jnp.maximum(n_tiles - 1, 0)[None])[0])
    r0 = (j - _pick(tile_end - tiles_e, te)) * TM
    cum_t = cum[te]
    b0 = jnp.sum(cum_t[:, 1:] <= r0[:, None], axis=1)
    b1 = jnp.sum(cum_t[:, :-1] < (r0 + TM)[:, None], axis=1)
    rows = jnp.clip(_pick(total, te) - r0, 0, TM)
    zero = jnp.zeros_like(j)
    return (te, jnp.where(live, r0, zero), jnp.where(live, b0, zero).astype(i32),
            jnp.where(live, b1, zero).astype(i32), jnp.where(live, rows, zero).astype(i32),
            n_tiles.reshape(1).astype(i32), cum.reshape(-1), src.reshape(-1).astype(i32), used.astype(i32))


def kernel(x, norm1_g, w_in, attn_sinks, gmlp_ln_g, gmlp_ln_b, gmlp_ws, gmlp_bs, w_attn_branch,
           w_gmlp_branch, w_out, norm2_g, router_group_w, router_group_b, router_expert_w,
           router_expert_b, expert_w_gate, expert_w_up, expert_w_down, final_norm_g):
    b, s, d = x.shape
    assert d == D_MODEL and s % TS == 0 and norm1_g.shape[0] == 1
    t = b * s
    n_blocks = t // TS
    bf16, f32 = jnp.bfloat16, jnp.float32
    x2 = x.reshape(t, d)

    rc, rs1, rs2 = _rope_lane_tables(s)
    sk = attn_sinks[0].astype(f32)
    sinks = jnp.stack([
        jnp.concatenate([jnp.full((BLOCK,), 1.0, f32) * sk[4 * g + p], jnp.full((BLOCK,), 1.0, f32) * sk[4 * g + 2 + p]])
        for g in range(N_KV_HEADS) for p in range(2)])
    ws = gmlp_ws[0]
    wcat = jnp.stack([jnp.concatenate([ws[2 * j], ws[2 * j + 1]], axis=1) for j in range(G_GROUPS // 2)])
    bsf = jnp.repeat(gmlp_bs[0].T, G_GROUP_DIM, axis=1)
    wrt = jnp.concatenate([router_group_w[0].T, jnp.zeros((8 - N_GROUPS, d), f32), router_expert_w[0].T], axis=0)
    brt = jnp.concatenate([router_group_b[0], jnp.zeros((8 - N_GROUPS,), f32), router_expert_b[0]])
    brt = jnp.broadcast_to(brt[:, None], (ROUTER_ROWS, TS))
    tri = (jnp.arange(TS)[:, None] < jnp.arange(TS)[None, :]).astype(bf16)
    ltri = (jnp.arange(LANES)[None, :] < jnp.arange(N_EXPERTS)[:, None]).astype(bf16)

    x1, xs, rw, cnt = _mixer(
        x2, norm1_g, w_in[0].astype(bf16), rc, rs1, rs2, sinks, gmlp_ln_g, gmlp_ln_b, wcat, bsf,
        w_attn_branch[0].astype(bf16), w_gmlp_branch[0].astype(bf16), w_out[0].astype(bf16),
        norm2_g, wrt.astype(bf16), brt, tri, ltri, s)

    max_tiles = (TOP_K * t + n_blocks * N_EXPERTS * (ROW_ALIGN - 1)) // TM + N_EXPERTS
    grid_tiles = max_tiles + 2
    tables = _tile_tables(cnt, n_blocks, grid_tiles)
    ys = _experts(tables, xs, expert_w_gate[0], expert_w_up[0], expert_w_down[0], n_blocks, grid_tiles)
    out = _combine(x1, rw, final_norm_g.reshape(1, d), ys)
    return out.reshape(b, s, d)
```

```python
import functools
import math

import jax
import jax.numpy as jnp
from jax import lax
from jax.experimental import pallas as pl
from jax.experimental.pallas import tpu as pltpu

D_MODEL = 1024
HEAD_DIM = 64
N_HEADS = 8
N_KV_HEADS = 2
BLOCK = 128
ROT_DIM = HEAD_DIM // 4
ROPE_THETA = 500000.0
ATTN_WIDTH = N_HEADS * HEAD_DIM
KV_WIDTH = N_KV_HEADS * HEAD_DIM
G_GROUPS = 8
G_GROUP_DIM = 64
G_WIDTH = G_GROUPS * G_GROUP_DIM
Q_END = ATTN_WIDTH
K_END = Q_END + KV_WIDTH
V_END = K_END + KV_WIDTH
Z_END = V_END + 2 * G_WIDTH
GA_END = Z_END + D_MODEL
IN_COLS = GA_END + D_MODEL
N_GROUPS = 4
EXPERTS_PER_GROUP = 8
N_EXPERTS = N_GROUPS * EXPERTS_PER_GROUP
TOP_K = 2
D_EXPERT = 512
EPS = 1e-5
NEG_INF = -1e30

LANES = 128
ROW_ALIGN = 8
ROUTER_ROWS = 8 + N_EXPERTS
PACK_COLS = D_MODEL // 2

TS = 512
TM = 512
COMPACT_ROWS = 256
CHUNK_COLS = 256
LOOKAHEAD = 3
BLOCK_ROWS = -(-(TOP_K * TS + N_EXPERTS * (ROW_ALIGN - 1)) // COMPACT_ROWS) * COMPACT_ROWS
VMEM_LIMIT = 58 * 1024 * 1024


def _rms(x, g):
    return x * lax.rsqrt(jnp.mean(x * x, axis=-1, keepdims=True) + EPS) * g


def _pack_bf16_pair(a, b):
    ua = lax.bitcast_convert_type(a.astype(jnp.bfloat16).astype(jnp.float32), jnp.uint32)
    ub = lax.bitcast_convert_type(b.astype(jnp.bfloat16).astype(jnp.float32), jnp.uint32)
    return ub | (ua >> 16)


def _unpack_bf16_rows(w):
    lo = lax.bitcast_convert_type(w << 16, jnp.float32)
    hi = lax.bitcast_convert_type(w & jnp.uint32(0xFFFF0000), jnp.float32)
    return jnp.concatenate([lo.astype(jnp.bfloat16), hi.astype(jnp.bfloat16)], axis=1)


def _mixer_kernel(x_ref, g1_ref, win_ref, rc_ref, rs1_ref, rs2_ref, sink_ref, lng_ref, lnb_ref,
                  wcat_ref, bsf_ref, wab_ref, wgb_ref, wout_ref, g2_ref, wrt_ref, brt_ref, tri_ref, ltri_ref,
                  x1_ref, xs_ref, rw_ref, cnt_ref,
                  kl_ref, vt_ref, attnt_ref, gated_ref, *, blocks_per_seq):
    step = pl.program_id(0)
    s_blk = step % blocks_per_seq
    ts = x_ref.shape[0]
    nb = ts // BLOCK
    f32, bf16 = jnp.float32, jnp.bfloat16

    @pl.when(s_blk == 0)
    def _():
        kl_ref[:, 0:BLOCK, :] = jnp.zeros((4, BLOCK, LANES), bf16)
        vt_ref[:, 0:BLOCK] = jnp.zeros((LANES, BLOCK), bf16)

    x = x_ref[...]
    hb = _rms(x, g1_ref[...]).astype(bf16)

    qkv = jnp.dot(hb, win_ref[:, 0:V_END], preferred_element_type=f32)
    rc, rs1, rs2 = rc_ref[...], rs1_ref[...], rs2_ref[...]

    def rope(t):
        return t * rc + pltpu.roll(t, 8, 1) * rs1 + pltpu.roll(t, LANES - 8, 1) * rs2

    scale = 1.0 / math.sqrt(HEAD_DIM)
    qb = [(rope(qkv[:, j * LANES:(j + 1) * LANES]) * scale).astype(bf16) for j in range(4)]
    k = rope(qkv[:, Q_END:K_END])
    v = qkv[:, K_END:V_END]
    k_sw = pltpu.roll(k, HEAD_DIM, 1)
    lo = lax.broadcasted_iota(jnp.int32, (ts, LANES), 1) < HEAD_DIM
    zero = jnp.zeros_like(k)
    kl_ref[0, BLOCK:BLOCK + ts, :] = jnp.where(lo, k, zero).astype(bf16)
    kl_ref[1, BLOCK:BLOCK + ts, :] = jnp.where(lo, zero, k_sw).astype(bf16)
    kl_ref[2, BLOCK:BLOCK + ts, :] = jnp.where(lo, k_sw, zero).astype(bf16)
    kl_ref[3, BLOCK:BLOCK + ts, :] = jnp.where(lo, zero, k).astype(bf16)
    vt_ref[:, BLOCK:BLOCK + ts] = v.T.astype(bf16)

    kj = lax.broadcasted_iota(jnp.int32, (2 * BLOCK, 2 * BLOCK), 0)
    qi = lax.broadcasted_iota(jnp.int32, (2 * BLOCK, 2 * BLOCK), 1) % BLOCK
    band = (kj > qi) & (kj <= qi + BLOCK)
    kmin = jnp.where(s_blk == 0, BLOCK, 0)
    bias_first = jnp.where(band & (kj >= kmin), 0.0, NEG_INF).astype(f32)
    bias_rest = jnp.where(band, 0.0, NEG_INF).astype(f32)
    items = [(i, g, p) for i in range(nb) for g in range(N_KV_HEADS) for p in range(2)]

    def scores(n):
        i, g, p = items[n]
        rows = slice(i * BLOCK, (i + 1) * BLOCK)
        qs = jnp.concatenate([qb[2 * g][rows], qb[2 * g + 1][rows]], axis=0)
        st = lax.dot_general(kl_ref[2 * g + p, i * BLOCK:(i + 2) * BLOCK, :], qs, (((1,), (1,)), ((), ())),
                             preferred_element_type=f32)
        return st + (bias_first if i == 0 else bias_rest)

    def finish(n, st):
        i, g, p = items[n]
        rows = slice(i * BLOCK, (i + 1) * BLOCK)
        sink = sink_ref[2 * g + p:2 * g + p + 1, :]
        m = jnp.maximum(jnp.max(st, axis=0, keepdims=True), sink)
        e = jnp.exp(st - m)
        den = jnp.sum(e, axis=0, keepdims=True) + jnp.exp(sink - m)
        ot = jnp.dot(vt_ref[g * HEAD_DIM:(g + 1) * HEAD_DIM, i * BLOCK:(i + 2) * BLOCK], e.astype(bf16),
                     preferred_element_type=f32) * (1.0 / den)
        h0 = 4 * g + p
        attnt_ref[h0 * HEAD_DIM:(h0 + 1) * HEAD_DIM, rows] = ot[:, 0:BLOCK]
        attnt_ref[(h0 + 2) * HEAD_DIM:(h0 + 3) * HEAD_DIM, rows] = ot[:, BLOCK:2 * BLOCK]

    n_chunks = D_MODEL // CHUNK_COLS
    zc, sga, sgg, gmc = [None] * n_chunks, [None] * n_chunks, [None] * n_chunks, [None] * n_chunks
    vn_box = []

    def proj(lo_col, c):
        cols = slice(lo_col + c * CHUNK_COLS, lo_col + (c + 1) * CHUNK_COLS)
        return jnp.dot(hb, win_ref[:, cols], preferred_element_type=f32)

    def z_chunk(c):
        zc[c] = jax.nn.gelu(proj(V_END, c))
        if c == n_chunks - 1:
            v2 = jnp.concatenate(zc[n_chunks // 2:], axis=1)
            mu = jnp.mean(v2, axis=-1, keepdims=True)
            vc = v2 - mu
            var = jnp.mean(vc * vc, axis=-1, keepdims=True)
            vn_box.append((vc * lax.rsqrt(var + EPS) * lng_ref[...] + lnb_ref[...]).astype(bf16))

    def ga_chunk(c):
        sga[c] = jax.nn.sigmoid(proj(Z_END, c))

    def gg_chunk(c):
        sgg[c] = jax.nn.sigmoid(proj(GA_END, c))

    wq = lax.broadcasted_iota(jnp.int32, (BLOCK, 2 * BLOCK), 0)
    wp = lax.broadcasted_iota(jnp.int32, (BLOCK, 2 * BLOCK), 1) % BLOCK
    causal = wp <= wq
    lo_c = lax.broadcasted_iota(jnp.int32, (BLOCK, LANES), 1) < G_GROUP_DIM
    wjs = {}

    def gmlp_dot(j, c):
        if j not in wjs:
            wjs[j] = jnp.where(causal, wcat_ref[j], 0.0).astype(bf16)
        cols = slice(j * LANES, (j + 1) * LANES)
        rows = slice(c * BLOCK, (c + 1) * BLOCK)
        vp = vn_box[0][rows, cols]
        zb = jnp.zeros_like(vp)
        rhs = jnp.concatenate([jnp.where(lo_c, vp, zb), jnp.where(lo_c, zb, vp)], axis=0)
        mixed = jnp.dot(wjs[j], rhs, preferred_element_type=f32) + bsf_ref[:, cols]
        u = zc[j // 2][:, (j % 2) * LANES:(j % 2 + 1) * LANES]
        gated_ref[rows, cols] = u[rows] * mixed

    def gm_chunk(c):
        cols = slice(c * CHUNK_COLS, (c + 1) * CHUNK_COLS)
        gmc[c] = jnp.dot(gated_ref[...].astype(bf16), wgb_ref[:, cols], preferred_element_type=f32)

    fillers = [[functools.partial(z_chunk, c)] for c in range(n_chunks)]
    fillers += [[functools.partial(ga_chunk, c)] for c in range(n_chunks)]
    fillers += [[functools.partial(gg_chunk, c)] for c in range(n_chunks)]
    fillers += [[functools.partial(gm_chunk, c)] for c in range(n_chunks)]
    gd = [functools.partial(gmlp_dot, j, c) for j in range(G_GROUPS // 2) for c in range(nb)]
    for q in range(len(gd)):
        fillers[n_chunks + q // 2].append(gd[q])
    assert len(fillers) == len(items)

    for f in fillers[0] + fillers[1]:
        f()
    fillers = fillers[2:] + [[], []]
    sts = {n: scores(n) for n in range(LOOKAHEAD)}
    for n in range(len(items)):
        finish(n, sts.pop(n))
        if n + LOOKAHEAD < len(items):
            sts[n + LOOKAHEAD] = scores(n + LOOKAHEAD)
        for f in fillers[n]:
            f()
    kl_ref[:, 0:BLOCK, :] = kl_ref[:, ts:ts + BLOCK, :]
    vt_ref[:, 0:BLOCK] = vt_ref[:, ts:ts + BLOCK]
    attn_b = attnt_ref[...].T.astype(bf16)

    mix = []
    for c in range(n_chunks):
        cols = slice(c * CHUNK_COLS, (c + 1) * CHUNK_COLS)
        a_c = jnp.dot(attn_b, wab_ref[:, cols], preferred_element_type=f32)
        mix.append((sga[c] * a_c + sgg[c] * gmc[c]).astype(bf16))
    x1 = x + jnp.dot(jnp.concatenate(mix, axis=1), wout_ref[...], preferred_element_type=f32)
    x1_ref[...] = x1

    xnb = _rms(x1, g2_ref[...]).astype(bf16)
    lt = lax.dot_general(wrt_ref[...], xnb, (((1,), (1,)), ((), ())),
                         preferred_element_type=f32) + brt_ref[...]
    r = [lt[i:i + 1, :] for i in range(N_GROUPS)]
    gm_ = jnp.maximum(jnp.maximum(r[0], r[1]), jnp.maximum(r[2], r[3]))
    gsel = jnp.where(r[0] == gm_, 0.0, jnp.where(r[1] == gm_, 1.0, jnp.where(r[2] == gm_, 2.0, 3.0)))
    g_w = 1.0 / (jnp.exp(r[0] - gm_) + jnp.exp(r[1] - gm_) + jnp.exp(r[2] - gm_) + jnp.exp(r[3] - gm_))
    eg = [lt[8 + 8 * kk:16 + 8 * kk, :] for kk in range(N_GROUPS)]
    ein = jnp.where(gsel == 0.0, eg[0], jnp.where(gsel == 1.0, eg[1], jnp.where(gsel == 2.0, eg[2], eg[3])))
    io8 = lax.broadcasted_iota(jnp.int32, (EXPERTS_PER_GROUP, ts), 0).astype(f32)
    m1 = jnp.max(ein, axis=0, keepdims=True)
    i1 = jnp.min(jnp.where(ein == m1, io8, 8.0), axis=0, keepdims=True)
    e2 = jnp.where(io8 == i1, NEG_INF, ein)
    m2 = jnp.max(e2, axis=0, keepdims=True)
    i2 = jnp.min(jnp.where(e2 == m2, io8, 8.0), axis=0, keepdims=True)
    t2 = jnp.exp(m2 - m1)
    w1 = 1.0 / (1.0 + t2)
    wt0, wt1 = g_w * w1, g_w * (t2 * w1)
    eid0, eid1 = gsel * 8.0 + i1, gsel * 8.0 + i2

    io32 = lax.broadcasted_iota(jnp.int32, (N_EXPERTS, ts), 0).astype(f32)
    oh0, oh1 = io32 == eid0, io32 == eid1
    both = jnp.where(oh0, 1.0, jnp.where(oh1, 1.0, 0.0))
    pref = jnp.dot(both.astype(bf16), tri_ref[...], preferred_element_type=f32)
    n_e = jnp.sum(both, axis=1, keepdims=True)
    cnt_ref[0] = jnp.broadcast_to(n_e, (N_EXPERTS, LANES))
    units = jnp.broadcast_to(jnp.floor((n_e + (ROW_ALIGN - 1.0)) * (1.0 / ROW_ALIGN)), (N_EXPERTS, ts))
    units = jnp.concatenate([units, jnp.zeros((LANES - N_EXPERTS, ts), f32)], axis=0).astype(bf16)
    tot = ROW_ALIGN * jnp.dot(ltri_ref[...], units, preferred_element_type=f32) + pref
    slot0 = jnp.sum(jnp.where(oh0, tot, 0.0), axis=0, keepdims=True)
    slot1 = jnp.sum(jnp.where(oh1, tot, 0.0), axis=0, keepdims=True)

    io128 = lax.broadcasted_iota(jnp.int32, (LANES, ts), 0)
    rw_ref[...] = jnp.where(io128 == 0, wt0, jnp.where(io128 == 1, wt1,
                            jnp.where(io128 == 2, slot0, jnp.where(io128 == 3, slot1, 0.0)))).T

    s0i, s1i = slot0.astype(jnp.int32), slot1.astype(jnp.int32)
    for c in range(xs_ref.shape[0] // COMPACT_ROWS):
        io = lax.broadcasted_iota(jnp.int32, (COMPACT_ROWS, ts), 0) + c * COMPACT_ROWS
        onehot = jnp.where(io == s0i, 1.0, jnp.where(io == s1i, 1.0, 0.0)).astype(bf16)
        rows = jnp.dot(onehot, xnb, preferred_element_type=f32)
        xs_ref[c * COMPACT_ROWS:(c + 1) * COMPACT_ROWS, :] = _pack_bf16_pair(rows[:, 0:PACK_COLS], rows[:, PACK_COLS:])


def _const_spec(shape):
    nd = len(shape)
    return pl.BlockSpec(shape, lambda i: (0,) * nd)


def _mixer(x2, g1, win, rc, rs1, rs2, sinks, lng, lnb, wcat, bsf, wab, wgb, wout, g2, wrt, brt, tri, ltri, seq):
    t = x2.shape[0]
    n_blocks = t // TS
    blocks_per_seq = seq // TS
    tok = lambda i: (i, 0)
    pos = lambda i: (i % blocks_per_seq, 0)
    consts = (sinks, lng, lnb, wcat, bsf, wab, wgb, wout, g2, wrt, brt, tri, ltri)
    in_specs = [
        pl.BlockSpec((TS, D_MODEL), tok),
        _const_spec(g1.shape), _const_spec(win.shape),
        pl.BlockSpec((TS, LANES), pos), pl.BlockSpec((TS, LANES), pos), pl.BlockSpec((TS, LANES), pos),
    ] + [_const_spec(c.shape) for c in consts]
    out_shape = (
        jax.ShapeDtypeStruct((t, D_MODEL), jnp.float32),
        jax.ShapeDtypeStruct((n_blocks * BLOCK_ROWS, PACK_COLS), jnp.uint32),
        jax.ShapeDtypeStruct((t, LANES), jnp.float32),
        jax.ShapeDtypeStruct((n_blocks, N_EXPERTS, LANES), jnp.float32),
    )
    out_specs = (
        pl.BlockSpec((TS, D_MODEL), tok),
        pl.BlockSpec((BLOCK_ROWS, PACK_COLS), tok),
        pl.BlockSpec((TS, LANES), tok),
        pl.BlockSpec((1, N_EXPERTS, LANES), lambda i: (i, 0, 0)),
    )
    scratch = [
        pltpu.VMEM((4, BLOCK + TS, LANES), jnp.bfloat16),
        pltpu.VMEM((LANES, BLOCK + TS), jnp.bfloat16),
        pltpu.VMEM((ATTN_WIDTH, TS), jnp.float32),
        pltpu.VMEM((TS, G_WIDTH), jnp.float32),
    ]
    return pl.pallas_call(
        functools.partial(_mixer_kernel, blocks_per_seq=blocks_per_seq),
        grid=(n_blocks,),
        in_specs=in_specs,
        out_specs=out_specs,
        out_shape=out_shape,
        scratch_shapes=scratch,
        compiler_params=pltpu.CompilerParams(dimension_semantics=("arbitrary",),
                                             vmem_limit_bytes=VMEM_LIMIT),
        name="mixer",
    )(x2, g1, win, rc, rs1, rs2, *consts)


def _expert_kernel(te_ref, tr0_ref, tb0_ref, tb1_ref, trows_ref, nt_ref, cum_ref, src_ref, used_ref,
                   xs_ref, wg_ref, wu_ref, wd_ref, ys_ref,
                   xbuf_ref, ybuf_ref, wgb_ref, wub_ref, wdb_ref, zero_ref, isem, osem, zsem,
                   *, n_blocks):
    j = pl.program_id(0)
    n_tiles = nt_ref[0]
    slot = j % 2
    f32, bf16 = jnp.float32, jnp.bfloat16

    def for_each_run(tile, fn):
        e, r0 = te_ref[tile], tr0_ref[tile]

        def body(b, c):
            start = cum_ref[e * (n_blocks + 1) + b]
            lo = jnp.maximum(start, r0)
            n = jnp.minimum(cum_ref[e * (n_blocks + 1) + b + 1], r0 + TM) - lo

            @pl.when(n > 0)
            def _():
                fn(pl.multiple_of(src_ref[e * n_blocks + b] + (lo - start), ROW_ALIGN),
                   pl.multiple_of(lo - r0, ROW_ALIGN), pl.multiple_of(n, ROW_ALIGN))
            return c

        lax.fori_loop(tb0_ref[tile], tb1_ref[tile], body, 0)

    def gather(tile, s):
        for_each_run(tile, lambda src, dst, n: pltpu.make_async_copy(
            xs_ref.at[pl.ds(src, n)], xbuf_ref.at[s, pl.ds(dst, n)], isem.at[s]).start())

    def scatter(tile, s):
        for_each_run(tile, lambda src, dst, n: pltpu.make_async_copy(
            ybuf_ref.at[s, pl.ds(dst, n)], ys_ref.at[pl.ds(src, n)], osem.at[s]).start())

    def wait_rows(sem, n):
        @pl.when(n > 0)
        def _():
            rows = pl.multiple_of(n, ROW_ALIGN)
            pltpu.make_async_copy(xs_ref.at[pl.ds(0, rows)], xbuf_ref.at[0, pl.ds(0, rows)], sem).wait()

    def zero_tail_copy(b, c):
        first = used_ref[b] + c * TM
        n = pl.multiple_of(jnp.minimum(BLOCK_ROWS - first, TM), ROW_ALIGN)
        return pltpu.make_async_copy(
            zero_ref.at[pl.ds(0, n)], ys_ref.at[pl.ds(pl.multiple_of(b * BLOCK_ROWS + first, ROW_ALIGN), n)], zsem)

    @pl.when(j == 0)
    def _():
        xbuf_ref[...] = jnp.zeros_like(xbuf_ref)
        zero_ref[...] = jnp.zeros_like(zero_ref)

        def per_block(b, carry):
            chunks = (BLOCK_ROWS - used_ref[b] + TM - 1) // TM
            lax.fori_loop(0, chunks, lambda c, x: (zero_tail_copy(b, c).start(), x)[1], 0)
            lax.fori_loop(0, chunks, lambda c, x: (zero_tail_copy(b, c).wait(), x)[1], 0)
            return carry

        lax.fori_loop(0, n_blocks, per_block, 0)

        @pl.when(n_tiles > 0)
        def _():
            gather(0, 0)

    @pl.when(j + 1 < n_tiles)
    def _():
        gather(j + 1, 1 - slot)

    @pl.when((j >= 2) & (j - 2 < n_tiles))
    def _():
        wait_rows(osem.at[slot], trows_ref[jnp.maximum(j - 2, 0)])

    @pl.when(j < n_tiles)
    def _():
        @pl.when((j == 0) | (te_ref[j] != te_ref[jnp.maximum(j - 1, 0)]))
        def _():
            wgb_ref[...] = wg_ref[0].astype(bf16)
            wub_ref[...] = wu_ref[0].astype(bf16)
            wdb_ref[...] = wd_ref[0].astype(bf16)

        wait_rows(isem.at[slot], trows_ref[j])
        xb = _unpack_bf16_rows(xbuf_ref[slot])
        half = D_EXPERT // 2
        g0 = jnp.dot(xb, wgb_ref[:, 0:half], preferred_element_type=f32)
        u0 = jnp.dot(xb, wub_ref[:, 0:half], preferred_element_type=f32)
        g1 = jnp.dot(xb, wgb_ref[:, half:], preferred_element_type=f32)
        u1 = jnp.dot(xb, wub_ref[:, half:], preferred_element_type=f32)
        y = jnp.dot((jax.nn.silu(g0) * u0).astype(bf16), wdb_ref[0:half, :], preferred_element_type=f32)
        y = y + jnp.dot((jax.nn.silu(g1) * u1).astype(bf16), wdb_ref[half:, :], preferred_element_type=f32)
        ybuf_ref[slot] = _pack_bf16_pair(y[:, 0:PACK_COLS], y[:, PACK_COLS:])
        scatter(j, slot)


def _experts(tables, xs, wg, wu, wd, n_blocks, grid_tiles):
    wsel = lambda j, te, *_: (te[j], 0, 0)
    grid_spec = pltpu.PrefetchScalarGridSpec(
        num_scalar_prefetch=len(tables),
        grid=(grid_tiles,),
        in_specs=[
            pl.BlockSpec(memory_space=pl.ANY),
            pl.BlockSpec((1, D_MODEL, D_EXPERT), wsel),
            pl.BlockSpec((1, D_MODEL, D_EXPERT), wsel),
            pl.BlockSpec((1, D_EXPERT, D_MODEL), wsel),
        ],
        out_specs=pl.BlockSpec(memory_space=pl.ANY),
        scratch_shapes=[
            pltpu.VMEM((2, TM, PACK_COLS), jnp.uint32),
            pltpu.VMEM((2, TM, PACK_COLS), jnp.uint32),
            pltpu.VMEM((D_MODEL, D_EXPERT), jnp.bfloat16),
            pltpu.VMEM((D_MODEL, D_EXPERT), jnp.bfloat16),
            pltpu.VMEM((D_EXPERT, D_MODEL), jnp.bfloat16),
            pltpu.VMEM((TM, PACK_COLS), jnp.uint32),
            pltpu.SemaphoreType.DMA((2,)),
            pltpu.SemaphoreType.DMA((2,)),
            pltpu.SemaphoreType.DMA,
        ],
    )
    return pl.pallas_call(
        functools.partial(_expert_kernel, n_blocks=n_blocks),
        grid_spec=grid_spec,
        out_shape=jax.ShapeDtypeStruct(xs.shape, jnp.uint32),
        compiler_params=pltpu.CompilerParams(dimension_semantics=("arbitrary",),
                                             vmem_limit_bytes=VMEM_LIMIT),
        name="experts",
    )(*tables, xs, wg, wu, wd)


def _combine_kernel(x1_ref, rw_ref, gf_ref, ys_ref, o_ref):
    ts = x1_ref.shape[0]
    rw = rw_ref[...]
    io = lax.broadcasted_iota(jnp.int32, (ts, BLOCK_ROWS), 1).astype(jnp.float32)
    wsel = jnp.where(io == rw[:, 2:3], rw[:, 0:1], 0.0) + jnp.where(io == rw[:, 3:4], rw[:, 1:2], 0.0)
    y = jnp.dot(wsel.astype(jnp.bfloat16), _unpack_bf16_rows(ys_ref[...]), preferred_element_type=jnp.float32)
    o_ref[...] = _rms(x1_ref[...] + y, gf_ref[...])


def _combine(x1, rw, gf, ys):
    t = x1.shape[0]
    tok = lambda i: (i, 0)
    return pl.pallas_call(
        _combine_kernel,
        grid=(t // TS,),
        in_specs=[
            pl.BlockSpec((TS, D_MODEL), tok),
            pl.BlockSpec((TS, LANES), tok),
            pl.BlockSpec((1, D_MODEL), lambda i: (0, 0)),
            pl.BlockSpec((BLOCK_ROWS, PACK_COLS), tok),
        ],
        out_specs=pl.BlockSpec((TS, D_MODEL), tok),
        out_shape=jax.ShapeDtypeStruct((t, D_MODEL), jnp.float32),
        compiler_params=pltpu.CompilerParams(dimension_semantics=("arbitrary",),
                                             vmem_limit_bytes=VMEM_LIMIT),
        name="combine",
    )(x1, rw, gf, ys)


def _rope_lane_tables(seq):
    inv_freq = ROPE_THETA ** (-jnp.arange(0, ROT_DIM, 2, dtype=jnp.float32) / ROT_DIM)
    ang = jnp.arange(seq, dtype=jnp.float32)[:, None] * inv_freq[None, :]
    cos, sin = jnp.cos(ang), jnp.sin(ang)
    half = ROT_DIM // 2
    ones = jnp.ones((seq, HEAD_DIM - ROT_DIM), jnp.float32)
    zeros = jnp.zeros((seq, HEAD_DIM - ROT_DIM), jnp.float32)
    zh = jnp.zeros((seq, half), jnp.float32)
    c = jnp.concatenate([cos, cos, ones], axis=1)
    s1 = jnp.concatenate([zh, sin, zeros], axis=1)
    s2 = jnp.concatenate([-sin, zh, zeros], axis=1)
    rep = LANES // HEAD_DIM
    return jnp.tile(c, (1, rep)), jnp.tile(s1, (1, rep)), jnp.tile(s2, (1, rep))


def _pick(table, idx):
    return jnp.sum(jnp.where(idx[:, None] == jnp.arange(table.shape[0])[None, :], table[None, :], 0), axis=1)


def _tile_tables(cnt, n_blocks, grid_tiles):
    i32 = jnp.int32
    n = cnt[:, :, 0].astype(i32)
    n8 = (n + ROW_ALIGN - 1) // ROW_ALIGN * ROW_ALIGN
    loc = jnp.cumsum(n8, axis=1) - n8
    used = jnp.sum(n8, axis=1)
    cum = jnp.concatenate([jnp.zeros((1, N_EXPERTS), i32), jnp.cumsum(n8, axis=0)], axis=0).T
    total = cum[:, -1]
    src = (jnp.arange(n_blocks, dtype=i32)[:, None] * BLOCK_ROWS + loc).T
    tiles_e = (total + TM - 1) // TM
    tile_end = jnp.cumsum(tiles_e)
    n_tiles = tile_end[-1]
    j = jnp.arange(grid_tiles, dtype=i32)
    te = jnp.minimum(jnp.sum(tile_end[None, :] <= j[:, None], axis=1), N_EXPERTS - 1).astype(i32)
    live = j < n_tiles
    te = jnp.where(live, te, _pick(te, jnp.maximum(n_tiles - 1, 0)[None])[0])
    r0 = (j - _pick(tile_end - tiles_e, te)) * TM
    cum_t = cum[te]
    b0 = jnp.sum(cum_t[:, 1:] <= r0[:, None], axis=1)
    b1 = jnp.sum(cum_t[:, :-1] < (r0 + TM)[:, None], axis=1)
    rows = jnp.clip(_pick(total, te) - r0, 0, TM)
    zero = jnp.zeros_like(j)
    return (te, jnp.where(live, r0, zero), jnp.where(live, b0, zero).astype(i32),
            jnp.where(live, b1, zero).astype(i32), jnp.where(live, rows, zero).astype(i32),
            n_tiles.reshape(1).astype(i32), cum.reshape(-1), src.reshape(-1).astype(i32), used.astype(i32))


def kernel(x, norm1_g, w_in, attn_sinks, gmlp_ln_g, gmlp_ln_b, gmlp_ws, gmlp_bs, w_attn_branch,
           w_gmlp_branch, w_out, norm2_g, router_group_w, router_group_b, router_expert_w,
           router_expert_b, expert_w_gate, expert_w_up, expert_w_down, final_norm_g):
    b, s, d = x.shape
    assert d == D_MODEL and s % TS == 0 and norm1_g.shape[0] == 1
    t = b * s
    n_blocks = t // TS
    bf16, f32 = jnp.bfloat16, jnp.float32
    x2 = x.reshape(t, d)

    rc, rs1, rs2 = _rope_lane_tables(s)
    sk = attn_sinks[0].astype(f32)
    sinks = jnp.stack([
        jnp.concatenate([jnp.full((BLOCK,), 1.0, f32) * sk[4 * g + p], jnp.full((BLOCK,), 1.0, f32) * sk[4 * g + 2 + p]])
        for g in range(N_KV_HEADS) for p in range(2)])
    ws = gmlp_ws[0]
    wcat = jnp.stack([jnp.concatenate([ws[2 * j], ws[2 * j + 1]], axis=1) for j in range(G_GROUPS // 2)])
    bsf = jnp.repeat(gmlp_bs[0].T, G_GROUP_DIM, axis=1)
    wrt = jnp.concatenate([router_group_w[0].T, jnp.zeros((8 - N_GROUPS, d), f32), router_expert_w[0].T], axis=0)
    brt = jnp.concatenate([router_group_b[0], jnp.zeros((8 - N_GROUPS,), f32), router_expert_b[0]])
    brt = jnp.broadcast_to(brt[:, None], (ROUTER_ROWS, TS))
    tri = (jnp.arange(TS)[:, None] < jnp.arange(TS)[None, :]).astype(bf16)
    ltri = (jnp.arange(LANES)[None, :] < jnp.arange(N_EXPERTS)[:, None]).astype(bf16)

    x1, xs, rw, cnt = _mixer(
        x2, norm1_g, w_in[0].astype(bf16), rc, rs1, rs2, sinks, gmlp_ln_g, gmlp_ln_b, wcat, bsf,
        w_attn_branch[0].astype(bf16), w_gmlp_branch[0].astype(bf16), w_out[0].astype(bf16),
        norm2_g, wrt.astype(bf16), brt, tri, ltri, s)

    max_tiles = (TOP_K * t + n_blocks * N_EXPERTS * (ROW_ALIGN - 1)) // TM + N_EXPERTS
    grid_tiles = max_tiles + 2
    tables = _tile_tables(cnt, n_blocks, grid_tiles)
    ys = _experts(tables, xs, expert_w_gate[0], expert_w_up[0], expert_w_down[0], n_blocks, grid_tiles)
    out = _combine(x1, rw, final_norm_g.reshape(1, d), ys)
    return out.reshape(b, s, d)
```

```python
import functools
import math

import jax
import jax.numpy as jnp
from jax import lax
from jax.experimental import pallas as pl
from jax.experimental.pallas import tpu as pltpu

D_MODEL = 1024
HEAD_DIM = 64
N_HEADS = 8
N_KV_HEADS = 2
BLOCK = 128
ROT_DIM = HEAD_DIM // 4
ROPE_THETA = 500000.0
ATTN_WIDTH = N_HEADS * HEAD_DIM
KV_WIDTH = N_KV_HEADS * HEAD_DIM
G_GROUPS = 8
G_GROUP_DIM = 64
G_WIDTH = G_GROUPS * G_GROUP_DIM
Q_END = ATTN_WIDTH
K_END = Q_END + KV_WIDTH
V_END = K_END + KV_WIDTH
Z_END = V_END + 2 * G_WIDTH
GA_END = Z_END + D_MODEL
IN_COLS = GA_END + D_MODEL
N_GROUPS = 4
EXPERTS_PER_GROUP = 8
N_EXPERTS = N_GROUPS * EXPERTS_PER_GROUP
TOP_K = 2
D_EXPERT = 512
EPS = 1e-5
NEG_INF = -1e30

LANES = 128
ROW_ALIGN = 8
ROUTER_ROWS = 8 + N_EXPERTS
PACK_COLS = D_MODEL // 2

TS = 512
TM = 512
COMPACT_ROWS = 256
CHUNK_COLS = 256
LOOKAHEAD = 3
ROUTE_AT = (0, 1, 5, 14, 15)
BLOCK_ROWS = -(-(TOP_K * TS + N_EXPERTS * (ROW_ALIGN - 1)) // COMPACT_ROWS) * COMPACT_ROWS
VMEM_LIMIT = 58 * 1024 * 1024


def _rms(x, g):
    return x * lax.rsqrt(jnp.mean(x * x, axis=-1, keepdims=True) + EPS) * g


def _pack_bf16_pair(a, b):
    ua = lax.bitcast_convert_type(a.astype(jnp.bfloat16).astype(jnp.float32), jnp.uint32)
    ub = lax.bitcast_convert_type(b.astype(jnp.bfloat16).astype(jnp.float32), jnp.uint32)
    return ub | (ua >> 16)


def _unpack_bf16_rows(w):
    lo = lax.bitcast_convert_type(w << 16, jnp.float32)
    hi = lax.bitcast_convert_type(w & jnp.uint32(0xFFFF0000), jnp.float32)
    return jnp.concatenate([lo.astype(jnp.bfloat16), hi.astype(jnp.bfloat16)], axis=1)


def _route_and_compact(xnb, wrt_ref, brt_ref, tri_ref, ltri_ref, xs_ref, rw_ref, cnt_ref):
    ts = xnb.shape[0]
    f32, bf16 = jnp.float32, jnp.bfloat16
    lt = lax.dot_general(wrt_ref[...], xnb, (((1,), (1,)), ((), ())),
                         preferred_element_type=f32) + brt_ref[...]
    yield
    r = [lt[i:i + 1, :] for i in range(N_GROUPS)]
    gm_ = jnp.maximum(jnp.maximum(r[0], r[1]), jnp.maximum(r[2], r[3]))
    gsel = jnp.where(r[0] == gm_, 0.0, jnp.where(r[1] == gm_, 1.0, jnp.where(r[2] == gm_, 2.0, 3.0)))
    g_w = 1.0 / (jnp.exp(r[0] - gm_) + jnp.exp(r[1] - gm_) + jnp.exp(r[2] - gm_) + jnp.exp(r[3] - gm_))
    eg = [lt[8 + 8 * kk:16 + 8 * kk, :] for kk in range(N_GROUPS)]
    ein = jnp.where(gsel == 0.0, eg[0], jnp.where(gsel == 1.0, eg[1], jnp.where(gsel == 2.0, eg[2], eg[3])))
    io8 = lax.broadcasted_iota(jnp.int32, (EXPERTS_PER_GROUP, ts), 0).astype(f32)
    m1 = jnp.max(ein, axis=0, keepdims=True)
    i1 = jnp.min(jnp.where(ein == m1, io8, 8.0), axis=0, keepdims=True)
    e2 = jnp.where(io8 == i1, NEG_INF, ein)
    m2 = jnp.max(e2, axis=0, keepdims=True)
    i2 = jnp.min(jnp.where(e2 == m2, io8, 8.0), axis=0, keepdims=True)
    t2 = jnp.exp(m2 - m1)
    w1 = 1.0 / (1.0 + t2)
    wt0, wt1 = g_w * w1, g_w * (t2 * w1)
    eid0, eid1 = gsel * 8.0 + i1, gsel * 8.0 + i2

    io32 = lax.broadcasted_iota(jnp.int32, (N_EXPERTS, ts), 0).astype(f32)
    oh0, oh1 = io32 == eid0, io32 == eid1
    both = jnp.where(oh0, 1.0, jnp.where(oh1, 1.0, 0.0))
    pref = jnp.dot(both.astype(bf16), tri_ref[...], preferred_element_type=f32)
    yield
    n_e = jnp.sum(both, axis=1, keepdims=True)
    cnt_ref[0] = jnp.broadcast_to(n_e, (N_EXPERTS, LANES))
    units = jnp.broadcast_to(jnp.floor((n_e + (ROW_ALIGN - 1.0)) * (1.0 / ROW_ALIGN)), (N_EXPERTS, ts))
    units = jnp.concatenate([units, jnp.zeros((LANES - N_EXPERTS, ts), f32)], axis=0).astype(bf16)
    tot = ROW_ALIGN * jnp.dot(ltri_ref[...], units, preferred_element_type=f32) + pref
    yield
    slot0 = jnp.sum(jnp.where(oh0, tot, 0.0), axis=0, keepdims=True)
    slot1 = jnp.sum(jnp.where(oh1, tot, 0.0), axis=0, keepdims=True)

    io128 = lax.broadcasted_iota(jnp.int32, (LANES, ts), 0)
    rw_ref[...] = jnp.where(io128 == 0, wt0, jnp.where(io128 == 1, wt1,
                            jnp.where(io128 == 2, slot0, jnp.where(io128 == 3, slot1, 0.0)))).T

    s0i, s1i = slot0.astype(jnp.int32), slot1.astype(jnp.int32)
    for c in range(xs_ref.shape[0] // COMPACT_ROWS):
        io = lax.broadcasted_iota(jnp.int32, (COMPACT_ROWS, ts), 0) + c * COMPACT_ROWS
        onehot = jnp.where(io == s0i, 1.0, jnp.where(io == s1i, 1.0, 0.0)).astype(bf16)
        rows = jnp.dot(onehot, xnb, preferred_element_type=f32)
        xs_ref[c * COMPACT_ROWS:(c + 1) * COMPACT_ROWS, :] = _pack_bf16_pair(rows[:, 0:PACK_COLS], rows[:, PACK_COLS:])
        yield


def _mixer_kernel(x_ref, xp_ref, g1_ref, win_ref, rc_ref, rs1_ref, rs2_ref, sink_ref, lng_ref, lnb_ref,
                  wcat_ref, bsf_ref, wab_ref, wgb_ref, wout_ref, g2_ref, wrt_ref, brt_ref, tri_ref, ltri_ref,
                  x1_ref, xs_ref, rw_ref, cnt_ref,
                  kl_ref, vt_ref, attnt_ref, gated_ref, mix_ref, *, blocks_per_seq, n_blocks):
    step = pl.program_id(0)
    s_blk = jnp.minimum(step, n_blocks - 1) % blocks_per_seq
    cur = step % 2
    ts = x_ref.shape[0]
    nb = ts // BLOCK
    f32, bf16 = jnp.float32, jnp.bfloat16

    @pl.when(step == 0)
    def _():
        mix_ref[1] = jnp.zeros((ts, D_MODEL), bf16)

    @pl.when(s_blk == 0)
    def _():
        kl_ref[:, 0:BLOCK, :] = jnp.zeros((4, BLOCK, LANES), bf16)
        vt_ref[:, 0:BLOCK] = jnp.zeros((LANES, BLOCK), bf16)

    x1 = xp_ref[...] + jnp.dot(mix_ref[1 - cur], wout_ref[...], preferred_element_type=f32)
    x1_ref[...] = x1
    route = _route_and_compact(_rms(x1, g2_ref[...]).astype(bf16), wrt_ref, brt_ref, tri_ref, ltri_ref,
                               xs_ref, rw_ref, cnt_ref)
    step_route = lambda: next(route, None)

    x = x_ref[...]
    hb = _rms(x, g1_ref[...]).astype(bf16)

    qkv = jnp.dot(hb, win_ref[:, 0:V_END], preferred_element_type=f32)
    rc, rs1, rs2 = rc_ref[...], rs1_ref[...], rs2_ref[...]

    def rope(t):
        return t * rc + pltpu.roll(t, 8, 1) * rs1 + pltpu.roll(t, LANES - 8, 1) * rs2

    scale = 1.0 / math.sqrt(HEAD_DIM)
    qb = [(rope(qkv[:, j * LANES:(j + 1) * LANES]) * scale).astype(bf16) for j in range(4)]
    k = rope(qkv[:, Q_END:K_END])
    v = qkv[:, K_END:V_END]
    k_sw = pltpu.roll(k, HEAD_DIM, 1)
    lo = lax.broadcasted_iota(jnp.int32, (ts, LANES), 1) < HEAD_DIM
    zero = jnp.zeros_like(k)
    kl_ref[0, BLOCK:BLOCK + ts, :] = jnp.where(lo, k, zero).astype(bf16)
    kl_ref[1, BLOCK:BLOCK + ts, :] = jnp.where(lo, zero, k_sw).astype(bf16)
    kl_ref[2, BLOCK:BLOCK + ts, :] = jnp.where(lo, k_sw, zero).astype(bf16)
    kl_ref[3, BLOCK:BLOCK + ts, :] = jnp.where(lo, zero, k).astype(bf16)
    vt_ref[:, BLOCK:BLOCK + ts] = v.T.astype(bf16)

    kj = lax.broadcasted_iota(jnp.int32, (2 * BLOCK, 2 * BLOCK), 0)
    qi = lax.broadcasted_iota(jnp.int32, (2 * BLOCK, 2 * BLOCK), 1) % BLOCK
    band = (kj > qi) & (kj <= qi + BLOCK)
    kmin = jnp.where(s_blk == 0, BLOCK, 0)
    bias_first = jnp.where(band & (kj >= kmin), 0.0, NEG_INF).astype(f32)
    bias_rest = jnp.where(band, 0.0, NEG_INF).astype(f32)
    items = [(i, g, p) for i in range(nb) for g in range(N_KV_HEADS) for p in range(2)]

    def scores(n):
        i, g, p = items[n]
        rows = slice(i * BLOCK, (i + 1) * BLOCK)
        qs = jnp.concatenate([qb[2 * g][rows], qb[2 * g + 1][rows]], axis=0)
        st = lax.dot_general(kl_ref[2 * g + p, i * BLOCK:(i + 2) * BLOCK, :], qs, (((1,), (1,)), ((), ())),
                             preferred_element_type=f32)
        return st + (bias_first if i == 0 else bias_rest)

    def finish(n, st):
        i, g, p = items[n]
        rows = slice(i * BLOCK, (i + 1) * BLOCK)
        sink = sink_ref[2 * g + p:2 * g + p + 1, :]
        m = jnp.maximum(jnp.max(st, axis=0, keepdims=True), sink)
        e = jnp.exp(st - m)
        den = jnp.sum(e, axis=0, keepdims=True) + jnp.exp(sink - m)
        ot = jnp.dot(vt_ref[g * HEAD_DIM:(g + 1) * HEAD_DIM, i * BLOCK:(i + 2) * BLOCK], e.astype(bf16),
                     preferred_element_type=f32) * (1.0 / den)
        h0 = 4 * g + p
        attnt_ref[h0 * HEAD_DIM:(h0 + 1) * HEAD_DIM, rows] = ot[:, 0:BLOCK]
        attnt_ref[(h0 + 2) * HEAD_DIM:(h0 + 3) * HEAD_DIM, rows] = ot[:, BLOCK:2 * BLOCK]

    n_chunks = D_MODEL // CHUNK_COLS
    zc, sga, sgg, gmc = [None] * n_chunks, [None] * n_chunks, [None] * n_chunks, [None] * n_chunks
    vn_box = []

    def proj(lo_col, c):
        cols = slice(lo_col + c * CHUNK_COLS, lo_col + (c + 1) * CHUNK_COLS)
        return jnp.dot(hb, win_ref[:, cols], preferred_element_type=f32)

    def z_chunk(c):
        zc[c] = jax.nn.gelu(proj(V_END, c))
        if c == n_chunks - 1:
            v2 = jnp.concatenate(zc[n_chunks // 2:], axis=1)
            mu = jnp.mean(v2, axis=-1, keepdims=True)
            vc = v2 - mu
            var = jnp.mean(vc * vc, axis=-1, keepdims=True)
            vn_box.append((vc * lax.rsqrt(var + EPS) * lng_ref[...] + lnb_ref[...]).astype(bf16))

    def ga_chunk(c):
        sga[c] = jax.nn.sigmoid(proj(Z_END, c))

    def gg_chunk(c):
        sgg[c] = jax.nn.sigmoid(proj(GA_END, c))

    wq = lax.broadcasted_iota(jnp.int32, (BLOCK, 2 * BLOCK), 0)
    wp = lax.broadcasted_iota(jnp.int32, (BLOCK, 2 * BLOCK), 1) % BLOCK
    causal = wp <= wq
    lo_c = lax.broadcasted_iota(jnp.int32, (BLOCK, LANES), 1) < G_GROUP_DIM
    wjs = {}

    def gmlp_dot(j, c):
        if j not in wjs:
            wjs[j] = jnp.where(causal, wcat_ref[j], 0.0).astype(bf16)
        cols = slice(j * LANES, (j + 1) * LANES)
        rhs = []
        for cc in (c, c + 1):
            vp = vn_box[0][cc * BLOCK:(cc + 1) * BLOCK, cols]
            zb = jnp.zeros_like(vp)
            rhs.append(jnp.concatenate([jnp.where(lo_c, vp, zb), jnp.where(lo_c, zb, vp)], axis=0))
        mixed = jnp.dot(wjs[j], jnp.concatenate(rhs, axis=1), preferred_element_type=f32)
        u = zc[j // 2][:, (j % 2) * LANES:(j % 2 + 1) * LANES]
        for q, cc in enumerate((c, c + 1)):
            rows = slice(cc * BLOCK, (cc + 1) * BLOCK)
            gated_ref[rows, cols] = u[rows] * (mixed[:, q * LANES:(q + 1) * LANES] + bsf_ref[:, cols])

    def gm_chunk(c):
        cols = slice(c * CHUNK_COLS, (c + 1) * CHUNK_COLS)
        gmc[c] = jnp.dot(gated_ref[...].astype(bf16), wgb_ref[:, cols], preferred_element_type=f32)

    fillers = [[functools.partial(z_chunk, c)] for c in range(n_chunks)]
    fillers += [[functools.partial(ga_chunk, c)] for c in range(n_chunks)]
    fillers += [[functools.partial(gg_chunk, c)] for c in range(n_chunks)]
    fillers += [[functools.partial(gm_chunk, c)] for c in range(n_chunks)]
    gd = [functools.partial(gmlp_dot, j, c) for j in range(G_GROUPS // 2) for c in range(0, nb, 2)]
    for q in range(len(gd)):
        fillers[n_chunks + q].append(gd[q])
    assert len(fillers) == len(items)

    fillers[0][0]()
    step_route()
    fillers[1][0]()
    step_route()
    step_route()
    fillers = fillers[2:] + [[], []]
    for n in ROUTE_AT:
        fillers[n].append(step_route)
    sts = {n: scores(n) for n in range(LOOKAHEAD)}
    for n in range(len(items)):
        finish(n, sts.pop(n))
        if n + LOOKAHEAD < len(items):
            sts[n + LOOKAHEAD] = scores(n + LOOKAHEAD)
        for f in fillers[n]:
            f()
    kl_ref[:, 0:BLOCK, :] = kl_ref[:, ts:ts + BLOCK, :]
    vt_ref[:, 0:BLOCK] = vt_ref[:, ts:ts + BLOCK]
    attn_b = attnt_ref[...].T.astype(bf16)

    mix = []
    for c in range(n_chunks):
        cols = slice(c * CHUNK_COLS, (c + 1) * CHUNK_COLS)
        a_c = jnp.dot(attn_b, wab_ref[:, cols], preferred_element_type=f32)
        mix.append((sga[c] * a_c + sgg[c] * gmc[c]).astype(bf16))
    mix_ref[cur] = jnp.concatenate(mix, axis=1)
    for _ in route:
        pass


def _const_spec(shape):
    nd = len(shape)
    return pl.BlockSpec(shape, lambda i: (0,) * nd)


def _mixer(x2, g1, win, rc, rs1, rs2, sinks, lng, lnb, wcat, bsf, wab, wgb, wout, g2, wrt, brt, tri, ltri, seq):
    t = x2.shape[0]
    n_blocks = t // TS
    blocks_per_seq = seq // TS
    cur_blk = lambda i: (jnp.minimum(i, n_blocks - 1), 0)
    prev_blk = lambda i: (jnp.maximum(i - 1, 0), 0)
    pos = lambda i: (jnp.minimum(i, n_blocks - 1) % blocks_per_seq, 0)
    consts = (sinks, lng, lnb, wcat, bsf, wab, wgb, wout, g2, wrt, brt, tri, ltri)
    in_specs = [
        pl.BlockSpec((TS, D_MODEL), cur_blk),
        pl.BlockSpec((TS, D_MODEL), prev_blk),
        _const_spec(g1.shape), _const_spec(win.shape),
        pl.BlockSpec((TS, LANES), pos), pl.BlockSpec((TS, LANES), pos), pl.BlockSpec((TS, LANES), pos),
    ] + [_const_spec(c.shape) for c in consts]
    out_shape = (
        jax.ShapeDtypeStruct((t, D_MODEL), jnp.float32),
        jax.ShapeDtypeStruct((n_blocks * BLOCK_ROWS, PACK_COLS), jnp.uint32),
        jax.ShapeDtypeStruct((t, LANES), jnp.float32),
        jax.ShapeDtypeStruct((n_blocks, N_EXPERTS, LANES), jnp.float32),
    )
    out_specs = (
        pl.BlockSpec((TS, D_MODEL), prev_blk),
        pl.BlockSpec((BLOCK_ROWS, PACK_COLS), prev_blk),
        pl.BlockSpec((TS, LANES), prev_blk),
        pl.BlockSpec((1, N_EXPERTS, LANES), lambda i: (jnp.maximum(i - 1, 0), 0, 0)),
    )
    scratch = [
        pltpu.VMEM((4, BLOCK + TS, LANES), jnp.bfloat16),
        pltpu.VMEM((LANES, BLOCK + TS), jnp.bfloat16),
        pltpu.VMEM((ATTN_WIDTH, TS), jnp.float32),
        pltpu.VMEM((TS, G_WIDTH), jnp.float32),
        pltpu.VMEM((2, TS, D_MODEL), jnp.bfloat16),
    ]
    return pl.pallas_call(
        functools.partial(_mixer_kernel, blocks_per_seq=blocks_per_seq, n_blocks=n_blocks),
        grid=(n_blocks + 1,),
        in_specs=in_specs,
        out_specs=out_specs,
        out_shape=out_shape,
        scratch_shapes=scratch,
        compiler_params=pltpu.CompilerParams(dimension_semantics=("arbitrary",),
                                             vmem_limit_bytes=VMEM_LIMIT),
        name="mixer",
    )(x2, x2, g1, win, rc, rs1, rs2, *consts)


def _expert_kernel(te_ref, tr0_ref, tb0_ref, tb1_ref, trows_ref, nt_ref, cum_ref, src_ref, used_ref,
                   xs_ref, wg_ref, wu_ref, wd_ref, ys_ref,
                   xbuf_ref, ybuf_ref, wgb_ref, wub_ref, wdb_ref, zero_ref, isem, osem, zsem,
                   *, n_blocks):
    j = pl.program_id(0)
    n_tiles = nt_ref[0]
    slot = j % 2
    f32, bf16 = jnp.float32, jnp.bfloat16

    def for_each_run(tile, fn):
        e, r0 = te_ref[tile], tr0_ref[tile]

        def body(b, c):
            start = cum_ref[e * (n_blocks + 1) + b]
            lo = jnp.maximum(start, r0)
            n = jnp.minimum(cum_ref[e * (n_blocks + 1) + b + 1], r0 + TM) - lo

            @pl.when(n > 0)
            def _():
                fn(pl.multiple_of(src_ref[e * n_blocks + b] + (lo - start), ROW_ALIGN),
                   pl.multiple_of(lo - r0, ROW_ALIGN), pl.multiple_of(n, ROW_ALIGN))
            return c

        lax.fori_loop(tb0_ref[tile], tb1_ref[tile], body, 0)

    def gather(tile, s):
        for_each_run(tile, lambda src, dst, n: pltpu.make_async_copy(
            xs_ref.at[pl.ds(src, n)], xbuf_ref.at[s, pl.ds(dst, n)], isem.at[s]).start())

    def scatter(tile, s):
        for_each_run(tile, lambda src, dst, n: pltpu.make_async_copy(
            ybuf_ref.at[s, pl.ds(dst, n)], ys_ref.at[pl.ds(src, n)], osem.at[s]).start())

    def wait_rows(sem, n):
        @pl.when(n > 0)
        def _():
            rows = pl.multiple_of(n, ROW_ALIGN)
            pltpu.make_async_copy(xs_ref.at[pl.ds(0, rows)], xbuf_ref.at[0, pl.ds(0, rows)], sem).wait()

    def zero_tail_copy(b, c):
        first = used_ref[b] + c * TM
        n = pl.multiple_of(jnp.minimum(BLOCK_ROWS - first, TM), ROW_ALIGN)
        return pltpu.make_async_copy(
            zero_ref.at[pl.ds(0, n)], ys_ref.at[pl.ds(pl.multiple_of(b * BLOCK_ROWS + first, ROW_ALIGN), n)], zsem)

    @pl.when(j == 0)
    def _():
        xbuf_ref[...] = jnp.zeros_like(xbuf_ref)
        zero_ref[...] = jnp.zeros_like(zero_ref)

        def per_block(b, carry):
            chunks = (BLOCK_ROWS - used_ref[b] + TM - 1) // TM
            lax.fori_loop(0, chunks, lambda c, x: (zero_tail_copy(b, c).start(), x)[1], 0)
            lax.fori_loop(0, chunks, lambda c, x: (zero_tail_copy(b, c).wait(), x)[1], 0)
            return carry

        lax.fori_loop(0, n_blocks, per_block, 0)

        @pl.when(n_tiles > 0)
        def _():
            gather(0, 0)

    @pl.when(j + 1 < n_tiles)
    def _():
        gather(j + 1, 1 - slot)

    @pl.when((j >= 2) & (j - 2 < n_tiles))
    def _():
        wait_rows(osem.at[slot], trows_ref[jnp.maximum(j - 2, 0)])

    @pl.when(j < n_tiles)
    def _():
        @pl.when((j == 0) | (te_ref[j] != te_ref[jnp.maximum(j - 1, 0)]))
        def _():
            wgb_ref[...] = wg_ref[0].astype(bf16)
            wub_ref[...] = wu_ref[0].astype(bf16)
            wdb_ref[...] = wd_ref[0].astype(bf16)

        wait_rows(isem.at[slot], trows_ref[j])
        xb = _unpack_bf16_rows(xbuf_ref[slot])
        half = D_EXPERT // 2
        g0 = jnp.dot(xb, wgb_ref[:, 0:half], preferred_element_type=f32)
        u0 = jnp.dot(xb, wub_ref[:, 0:half], preferred_element_type=f32)
        g1 = jnp.dot(xb, wgb_ref[:, half:], preferred_element_type=f32)
        u1 = jnp.dot(xb, wub_ref[:, half:], preferred_element_type=f32)
        y = jnp.dot((jax.nn.silu(g0) * u0).astype(bf16), wdb_ref[0:half, :], preferred_element_type=f32)
        y = y + jnp.dot((jax.nn.silu(g1) * u1).astype(bf16), wdb_ref[half:, :], preferred_element_type=f32)
        ybuf_ref[slot] = _pack_bf16_pair(y[:, 0:PACK_COLS], y[:, PACK_COLS:])
        scatter(j, slot)


def _experts(tables, xs, wg, wu, wd, n_blocks, grid_tiles):
    wsel = lambda j, te, *_: (te[j], 0, 0)
    grid_spec = pltpu.PrefetchScalarGridSpec(
        num_scalar_prefetch=len(tables),
        grid=(grid_tiles,),
        in_specs=[
            pl.BlockSpec(memory_space=pl.ANY),
            pl.BlockSpec((1, D_MODEL, D_EXPERT), wsel),
            pl.BlockSpec((1, D_MODEL, D_EXPERT), wsel),
            pl.BlockSpec((1, D_EXPERT, D_MODEL), wsel),
        ],
        out_specs=pl.BlockSpec(memory_space=pl.ANY),
        scratch_shapes=[
            pltpu.VMEM((2, TM, PACK_COLS), jnp.uint32),
            pltpu.VMEM((2, TM, PACK_COLS), jnp.uint32),
            pltpu.VMEM((D_MODEL, D_EXPERT), jnp.bfloat16),
            pltpu.VMEM((D_MODEL, D_EXPERT), jnp.bfloat16),
            pltpu.VMEM((D_EXPERT, D_MODEL), jnp.bfloat16),
            pltpu.VMEM((TM, PACK_COLS), jnp.uint32),
            pltpu.SemaphoreType.DMA((2,)),
            pltpu.SemaphoreType.DMA((2,)),
            pltpu.SemaphoreType.DMA,
        ],
    )
    return pl.pallas_call(
        functools.partial(_expert_kernel, n_blocks=n_blocks),
        grid_spec=grid_spec,
        out_shape=jax.ShapeDtypeStruct(xs.shape, jnp.uint32),
        compiler_params=pltpu.CompilerParams(dimension_semantics=("arbitrary",),
                                             vmem_limit_bytes=VMEM_LIMIT),
        name="experts",
    )(*tables, xs, wg, wu, wd)


def _combine_kernel(x1_ref, rw_ref, gf_ref, ys_ref, o_ref):
    ts = x1_ref.shape[0]
    rw = rw_ref[...]
    io = lax.broadcasted_iota(jnp.int32, (ts, BLOCK_ROWS), 1).astype(jnp.float32)
    wsel = jnp.where(io == rw[:, 2:3], rw[:, 0:1], 0.0) + jnp.where(io == rw[:, 3:4], rw[:, 1:2], 0.0)
    y = jnp.dot(wsel.astype(jnp.bfloat16), _unpack_bf16_rows(ys_ref[...]), preferred_element_type=jnp.float32)
    o_ref[...] = _rms(x1_ref[...] + y, gf_ref[...])


def _combine(x1, rw, gf, ys):
    t = x1.shape[0]
    tok = lambda i: (i, 0)
    return pl.pallas_call(
        _combine_kernel,
        grid=(t // TS,),
        in_specs=[
            pl.BlockSpec((TS, D_MODEL), tok),
            pl.BlockSpec((TS, LANES), tok),
            pl.BlockSpec((1, D_MODEL), lambda i: (0, 0)),
            pl.BlockSpec((BLOCK_ROWS, PACK_COLS), tok),
        ],
        out_specs=pl.BlockSpec((TS, D_MODEL), tok),
        out_shape=jax.ShapeDtypeStruct((t, D_MODEL), jnp.float32),
        compiler_params=pltpu.CompilerParams(dimension_semantics=("arbitrary",),
                                             vmem_limit_bytes=VMEM_LIMIT),
        name="combine",
    )(x1, rw, gf, ys)


def _rope_lane_tables(seq):
    inv_freq = ROPE_THETA ** (-jnp.arange(0, ROT_DIM, 2, dtype=jnp.float32) / ROT_DIM)
    ang = jnp.arange(seq, dtype=jnp.float32)[:, None] * inv_freq[None, :]
    cos, sin = jnp.cos(ang), jnp.sin(ang)
    half = ROT_DIM // 2
    ones = jnp.ones((seq, HEAD_DIM - ROT_DIM), jnp.float32)
    zeros = jnp.zeros((seq, HEAD_DIM - ROT_DIM), jnp.float32)
    zh = jnp.zeros((seq, half), jnp.float32)
    c = jnp.concatenate([cos, cos, ones], axis=1)
    s1 = jnp.concatenate([zh, sin, zeros], axis=1)
    s2 = jnp.concatenate([-sin, zh, zeros], axis=1)
    rep = LANES // HEAD_DIM
    return jnp.tile(c, (1, rep)), jnp.tile(s1, (1, rep)), jnp.tile(s2, (1, rep))


def _pick(table, idx):
    return jnp.sum(jnp.where(idx[:, None] == jnp.arange(table.shape[0])[None, :], table[None, :], 0), axis=1)


def _tile_tables(cnt, n_blocks, grid_tiles):
    i32 = jnp.int32
    n = cnt[:, :, 0].astype(i32)
    n8 = (n + ROW_ALIGN - 1) // ROW_ALIGN * ROW_ALIGN
    loc = jnp.cumsum(n8, axis=1) - n8
    used = jnp.sum(n8, axis=1)
    cum = jnp.concatenate([jnp.zeros((1, N_EXPERTS), i32), jnp.cumsum(n8, axis=0)], axis=0).T
    total = cum[:, -1]
    src = (jnp.arange(n_blocks, dtype=i32)[:, None] * BLOCK_ROWS + loc).T
    tiles_e = (total + TM - 1) // TM
    tile_end = jnp.cumsum(tiles_e)
    n_tiles = tile_end[-1]
    j = jnp.arange(grid_tiles, dtype=i32)
    te = jnp.minimum(jnp.sum(tile_end[None, :] <= j[:, None], axis=1), N_EXPERTS - 1).astype(i32)
    live = j < n_tiles
    te = jnp.where(live, te, _pick(te, jnp.maximum(n_tiles - 1, 0)[None])[0])
    r0 = (j - _pick(tile_end - tiles_e, te)) * TM
    cum_t = cum[te]
    b0 = jnp.sum(cum_t[:, 1:] <= r0[:, None], axis=1)
    b1 = jnp.sum(cum_t[:, :-1] < (r0 + TM)[:, None], axis=1)
    rows = jnp.clip(_pick(total, te) - r0, 0, TM)
    zero = jnp.zeros_like(j)
    return (te, jnp.where(live, r0, zero), jnp.where(live, b0, zero).astype(i32),
            jnp.where(live, b1, zero).astype(i32), jnp.where(live, rows, zero).astype(i32),
            n_tiles.reshape(1).astype(i32), cum.reshape(-1), src.reshape(-1).astype(i32), used.astype(i32))


def kernel(x, norm1_g, w_in, attn_sinks, gmlp_ln_g, gmlp_ln_b, gmlp_ws, gmlp_bs, w_attn_branch,
           w_gmlp_branch, w_out, norm2_g, router_group_w, router_group_b, router_expert_w,
           router_expert_b, expert_w_gate, expert_w_up, expert_w_down, final_norm_g):
    b, s, d = x.shape
    assert d == D_MODEL and s % TS == 0 and norm1_g.shape[0] == 1
    t = b * s
    n_blocks = t // TS
    bf16, f32 = jnp.bfloat16, jnp.float32
    x2 = x.reshape(t, d)

    rc, rs1, rs2 = _rope_lane_tables(s)
    sk = attn_sinks[0].astype(f32)
    sinks = jnp.stack([
        jnp.concatenate([jnp.full((BLOCK,), 1.0, f32) * sk[4 * g + p], jnp.full((BLOCK,), 1.0, f32) * sk[4 * g + 2 + p]])
        for g in range(N_KV_HEADS) for p in range(2)])
    ws = gmlp_ws[0]
    wcat = jnp.stack([jnp.concatenate([ws[2 * j], ws[2 * j + 1]], axis=1) for j in range(G_GROUPS // 2)])
    bsf = jnp.repeat(gmlp_bs[0].T, G_GROUP_DIM, axis=1)
    wrt = jnp.concatenate([router_group_w[0].T, jnp.zeros((8 - N_GROUPS, d), f32), router_expert_w[0].T], axis=0)
    brt = jnp.concatenate([router_group_b[0], jnp.zeros((8 - N_GROUPS,), f32), router_expert_b[0]])
    brt = jnp.broadcast_to(brt[:, None], (ROUTER_ROWS, TS))
    tri = (jnp.arange(TS)[:, None] < jnp.arange(TS)[None, :]).astype(bf16)
    ltri = (jnp.arange(LANES)[None, :] < jnp.arange(N_EXPERTS)[:, None]).astype(bf16)

    x1, xs, rw, cnt = _mixer(
        x2, norm1_g, w_in[0].astype(bf16), rc, rs1, rs2, sinks, gmlp_ln_g, gmlp_ln_b, wcat, bsf,
        w_attn_branch[0].astype(bf16), w_gmlp_branch[0].astype(bf16), w_out[0].astype(bf16),
        norm2_g, wrt.astype(bf16), brt, tri, ltri, s)

    max_tiles = (TOP_K * t + n_blocks * N_EXPERTS * (ROW_ALIGN - 1)) // TM + N_EXPERTS
    grid_tiles = max_tiles + 2
    tables = _tile_tables(cnt, n_blocks, grid_tiles)
    ys = _experts(tables, xs, expert_w_gate[0], expert_w_up[0], expert_w_down[0], n_blocks, grid_tiles)
    out = _combine(x1, rw, final_norm_g.reshape(1, d), ys)
    return out.reshape(b, s, d)
```

```python
import functools
import math

import jax
import jax.numpy as jnp
from jax import lax
from jax.experimental import pallas as pl
from jax.experimental.pallas import tpu as pltpu

D_MODEL = 1024
HEAD_DIM = 64
N_HEADS = 8
N_KV_HEADS = 2
BLOCK = 128
ROT_DIM = HEAD_DIM // 4
ROPE_THETA = 500000.0
ATTN_WIDTH = N_HEADS * HEAD_DIM
KV_WIDTH = N_KV_HEADS * HEAD_DIM
G_GROUPS = 8
G_GROUP_DIM = 64
G_WIDTH = G_GROUPS * G_GROUP_DIM
Q_END = ATTN_WIDTH
K_END = Q_END + KV_WIDTH
V_END = K_END + KV_WIDTH
Z_END = V_END + 2 * G_WIDTH
GA_END = Z_END + D_MODEL
IN_COLS = GA_END + D_MODEL
N_GROUPS = 4
EXPERTS_PER_GROUP = 8
N_EXPERTS = N_GROUPS * EXPERTS_PER_GROUP
TOP_K = 2
D_EXPERT = 512
EPS = 1e-5
NEG_INF = -1e30

LANES = 128
ROW_ALIGN = 8
ROUTER_ROWS = 8 + N_EXPERTS
PACK_COLS = D_MODEL // 2

TS = 512
TM = 512
FAST_RUNS = 20
COMPACT_ROWS = 256
CHUNK_COLS = 256
LOOKAHEAD = 3
ROUTE_AT = (0, 1, 5, 14, 15)
BLOCK_ROWS = -(-(TOP_K * TS + N_EXPERTS * (ROW_ALIGN - 1)) // COMPACT_ROWS) * COMPACT_ROWS
VMEM_LIMIT = 58 * 1024 * 1024


def _rms(x, g):
    return x * lax.rsqrt(jnp.mean(x * x, axis=-1, keepdims=True) + EPS) * g


def _pack_bf16_pair(a, b):
    ua = lax.bitcast_convert_type(a.astype(jnp.bfloat16).astype(jnp.float32), jnp.uint32)
    ub = lax.bitcast_convert_type(b.astype(jnp.bfloat16).astype(jnp.float32), jnp.uint32)
    return ub | (ua >> 16)


def _unpack_bf16_rows(w):
    lo = lax.bitcast_convert_type(w << 16, jnp.float32)
    hi = lax.bitcast_convert_type(w & jnp.uint32(0xFFFF0000), jnp.float32)
    return jnp.concatenate([lo.astype(jnp.bfloat16), hi.astype(jnp.bfloat16)], axis=1)


def _route_and_compact(xnb, wrt_ref, brt_ref, tri_ref, ltri_ref, xs_ref, rw_ref, cnt_ref):
    ts = xnb.shape[0]
    f32, bf16 = jnp.float32, jnp.bfloat16
    lt = lax.dot_general(wrt_ref[...], xnb, (((1,), (1,)), ((), ())),
                         preferred_element_type=f32) + brt_ref[...]
    yield
    r = [lt[i:i + 1, :] for i in range(N_GROUPS)]
    gm_ = jnp.maximum(jnp.maximum(r[0], r[1]), jnp.maximum(r[2], r[3]))
    gsel = jnp.where(r[0] == gm_, 0.0, jnp.where(r[1] == gm_, 1.0, jnp.where(r[2] == gm_, 2.0, 3.0)))
    g_w = 1.0 / (jnp.exp(r[0] - gm_) + jnp.exp(r[1] - gm_) + jnp.exp(r[2] - gm_) + jnp.exp(r[3] - gm_))
    eg = [lt[8 + 8 * kk:16 + 8 * kk, :] for kk in range(N_GROUPS)]
    ein = jnp.where(gsel == 0.0, eg[0], jnp.where(gsel == 1.0, eg[1], jnp.where(gsel == 2.0, eg[2], eg[3])))
    io8 = lax.broadcasted_iota(jnp.int32, (EXPERTS_PER_GROUP, ts), 0).astype(f32)
    m1 = jnp.max(ein, axis=0, keepdims=True)
    i1 = jnp.min(jnp.where(ein == m1, io8, 8.0), axis=0, keepdims=True)
    e2 = jnp.where(io8 == i1, NEG_INF, ein)
    m2 = jnp.max(e2, axis=0, keepdims=True)
    i2 = jnp.min(jnp.where(e2 == m2, io8, 8.0), axis=0, keepdims=True)
    t2 = jnp.exp(m2 - m1)
    w1 = 1.0 / (1.0 + t2)
    wt0, wt1 = g_w * w1, g_w * (t2 * w1)
    eid0, eid1 = gsel * 8.0 + i1, gsel * 8.0 + i2

    io32 = lax.broadcasted_iota(jnp.int32, (N_EXPERTS, ts), 0).astype(f32)
    oh0, oh1 = io32 == eid0, io32 == eid1
    both = jnp.where(oh0, 1.0, jnp.where(oh1, 1.0, 0.0))
    pref = jnp.dot(both.astype(bf16), tri_ref[...], preferred_element_type=f32)
    yield
    n_e = jnp.sum(both, axis=1, keepdims=True)
    cnt_ref[0] = jnp.broadcast_to(n_e, (N_EXPERTS, LANES))
    units = jnp.broadcast_to(jnp.floor((n_e + (ROW_ALIGN - 1.0)) * (1.0 / ROW_ALIGN)), (N_EXPERTS, ts))
    units = jnp.concatenate([units, jnp.zeros((LANES - N_EXPERTS, ts), f32)], axis=0).astype(bf16)
    tot = ROW_ALIGN * jnp.dot(ltri_ref[...], units, preferred_element_type=f32) + pref
    yield
    slot0 = jnp.sum(jnp.where(oh0, tot, 0.0), axis=0, keepdims=True)
    slot1 = jnp.sum(jnp.where(oh1, tot, 0.0), axis=0, keepdims=True)

    io128 = lax.broadcasted_iota(jnp.int32, (LANES, ts), 0)
    rw_ref[...] = jnp.where(io128 == 0, wt0, jnp.where(io128 == 1, wt1,
                            jnp.where(io128 == 2, slot0, jnp.where(io128 == 3, slot1, 0.0)))).T

    s0i, s1i = slot0.astype(jnp.int32), slot1.astype(jnp.int32)
    for c in range(xs_ref.shape[0] // COMPACT_ROWS):
        io = lax.broadcasted_iota(jnp.int32, (COMPACT_ROWS, ts), 0) + c * COMPACT_ROWS
        onehot = jnp.where(io == s0i, 1.0, jnp.where(io == s1i, 1.0, 0.0)).astype(bf16)
        rows = jnp.dot(onehot, xnb, preferred_element_type=f32)
        xs_ref[c * COMPACT_ROWS:(c + 1) * COMPACT_ROWS, :] = _pack_bf16_pair(rows[:, 0:PACK_COLS], rows[:, PACK_COLS:])
        yield


def _mixer_kernel(x_ref, xp_ref, g1_ref, win_ref, rc_ref, rs1_ref, rs2_ref, sink_ref, lng_ref, lnb_ref,
                  wcat_ref, bsf_ref, wab_ref, wgb_ref, wout_ref, g2_ref, wrt_ref, brt_ref, tri_ref, ltri_ref,
                  x1_ref, xs_ref, rw_ref, cnt_ref,
                  kl_ref, vt_ref, attnt_ref, gated_ref, mix_ref, *, blocks_per_seq, n_blocks):
    step = pl.program_id(0)
    s_blk = jnp.minimum(step, n_blocks - 1) % blocks_per_seq
    cur = step % 2
    ts = x_ref.shape[0]
    nb = ts // BLOCK
    f32, bf16 = jnp.float32, jnp.bfloat16

    @pl.when(step == 0)
    def _():
        mix_ref[1] = jnp.zeros((ts, D_MODEL), bf16)

    @pl.when(s_blk == 0)
    def _():
        kl_ref[:, 0:BLOCK, :] = jnp.zeros((4, BLOCK, LANES), bf16)
        vt_ref[:, 0:BLOCK] = jnp.zeros((LANES, BLOCK), bf16)

    x1 = xp_ref[...] + jnp.dot(mix_ref[1 - cur], wout_ref[...], preferred_element_type=f32)
    x1_ref[...] = x1
    route = _route_and_compact(_rms(x1, g2_ref[...]).astype(bf16), wrt_ref, brt_ref, tri_ref, ltri_ref,
                               xs_ref, rw_ref, cnt_ref)
    step_route = lambda: next(route, None)

    x = x_ref[...]
    hb = _rms(x, g1_ref[...]).astype(bf16)

    qkv = jnp.dot(hb, win_ref[:, 0:V_END], preferred_element_type=f32)
    rc, rs1, rs2 = rc_ref[...], rs1_ref[...], rs2_ref[...]

    def rope(t):
        return t * rc + pltpu.roll(t, 8, 1) * rs1 + pltpu.roll(t, LANES - 8, 1) * rs2

    scale = 1.0 / math.sqrt(HEAD_DIM)
    qb = [(rope(qkv[:, j * LANES:(j + 1) * LANES]) * scale).astype(bf16) for j in range(4)]
    k = rope(qkv[:, Q_END:K_END])
    v = qkv[:, K_END:V_END]
    k_sw = pltpu.roll(k, HEAD_DIM, 1)
    lo = lax.broadcasted_iota(jnp.int32, (ts, LANES), 1) < HEAD_DIM
    zero = jnp.zeros_like(k)
    kl_ref[0, BLOCK:BLOCK + ts, :] = jnp.where(lo, k, zero).astype(bf16)
    kl_ref[1, BLOCK:BLOCK + ts, :] = jnp.where(lo, zero, k_sw).astype(bf16)
    kl_ref[2, BLOCK:BLOCK + ts, :] = jnp.where(lo, k_sw, zero).astype(bf16)
    kl_ref[3, BLOCK:BLOCK + ts, :] = jnp.where(lo, zero, k).astype(bf16)
    vt_ref[:, BLOCK:BLOCK + ts] = v.T.astype(bf16)

    kj = lax.broadcasted_iota(jnp.int32, (2 * BLOCK, 2 * BLOCK), 0)
    qi = lax.broadcasted_iota(jnp.int32, (2 * BLOCK, 2 * BLOCK), 1) % BLOCK
    band = (kj > qi) & (kj <= qi + BLOCK)
    kmin = jnp.where(s_blk == 0, BLOCK, 0)
    bias_first = jnp.where(band & (kj >= kmin), 0.0, NEG_INF).astype(f32)
    bias_rest = jnp.where(band, 0.0, NEG_INF).astype(f32)
    items = [(i, g, p) for i in range(nb) for g in range(N_KV_HEADS) for p in range(2)]

    def scores(n):
        i, g, p = items[n]
        rows = slice(i * BLOCK, (i + 1) * BLOCK)
        qs = jnp.concatenate([qb[2 * g][rows], qb[2 * g + 1][rows]], axis=0)
        st = lax.dot_general(kl_ref[2 * g + p, i * BLOCK:(i + 2) * BLOCK, :], qs, (((1,), (1,)), ((), ())),
                             preferred_element_type=f32)
        return st + (bias_first if i == 0 else bias_rest)

    def finish(n, st):
        i, g, p = items[n]
        rows = slice(i * BLOCK, (i + 1) * BLOCK)
        sink = sink_ref[2 * g + p:2 * g + p + 1, :]
        m = jnp.maximum(jnp.max(st, axis=0, keepdims=True), sink)
        e = jnp.exp(st - m)
        den = jnp.sum(e, axis=0, keepdims=True) + jnp.exp(sink - m)
        ot = jnp.dot(vt_ref[g * HEAD_DIM:(g + 1) * HEAD_DIM, i * BLOCK:(i + 2) * BLOCK], e.astype(bf16),
                     preferred_element_type=f32) * (1.0 / den)
        h0 = 4 * g + p
        attnt_ref[h0 * HEAD_DIM:(h0 + 1) * HEAD_DIM, rows] = ot[:, 0:BLOCK]
        attnt_ref[(h0 + 2) * HEAD_DIM:(h0 + 3) * HEAD_DIM, rows] = ot[:, BLOCK:2 * BLOCK]

    n_chunks = D_MODEL // CHUNK_COLS
    zc, sga, sgg, gmc = [None] * n_chunks, [None] * n_chunks, [None] * n_chunks, [None] * n_chunks
    vn_box = []

    def proj(lo_col, c):
        cols = slice(lo_col + c * CHUNK_COLS, lo_col + (c + 1) * CHUNK_COLS)
        return jnp.dot(hb, win_ref[:, cols], preferred_element_type=f32)

    def z_chunk(c):
        zc[c] = jax.nn.gelu(proj(V_END, c))
        if c == n_chunks - 1:
            v2 = jnp.concatenate(zc[n_chunks // 2:], axis=1)
            mu = jnp.mean(v2, axis=-1, keepdims=True)
            vc = v2 - mu
            var = jnp.mean(vc * vc, axis=-1, keepdims=True)
            vn_box.append((vc * lax.rsqrt(var + EPS) * lng_ref[...] + lnb_ref[...]).astype(bf16))

    def ga_chunk(c):
        sga[c] = jax.nn.sigmoid(proj(Z_END, c))

    def gg_chunk(c):
        sgg[c] = jax.nn.sigmoid(proj(GA_END, c))

    wq = lax.broadcasted_iota(jnp.int32, (BLOCK, 2 * BLOCK), 0)
    wp = lax.broadcasted_iota(jnp.int32, (BLOCK, 2 * BLOCK), 1) % BLOCK
    causal = wp <= wq
    lo_c = lax.broadcasted_iota(jnp.int32, (BLOCK, LANES), 1) < G_GROUP_DIM
    wjs = {}

    def gmlp_dot(j, c):
        if j not in wjs:
            wjs[j] = jnp.where(causal, wcat_ref[j], 0.0).astype(bf16)
        cols = slice(j * LANES, (j + 1) * LANES)
        rhs = []
        for cc in (c, c + 1):
            vp = vn_box[0][cc * BLOCK:(cc + 1) * BLOCK, cols]
            zb = jnp.zeros_like(vp)
            rhs.append(jnp.concatenate([jnp.where(lo_c, vp, zb), jnp.where(lo_c, zb, vp)], axis=0))
        mixed = jnp.dot(wjs[j], jnp.concatenate(rhs, axis=1), preferred_element_type=f32)
        u = zc[j // 2][:, (j % 2) * LANES:(j % 2 + 1) * LANES]
        for q, cc in enumerate((c, c + 1)):
            rows = slice(cc * BLOCK, (cc + 1) * BLOCK)
            gated_ref[rows, cols] = u[rows] * (mixed[:, q * LANES:(q + 1) * LANES] + bsf_ref[:, cols])

    def gm_chunk(c):
        cols = slice(c * CHUNK_COLS, (c + 1) * CHUNK_COLS)
        gmc[c] = jnp.dot(gated_ref[...].astype(bf16), wgb_ref[:, cols], preferred_element_type=f32)

    fillers = [[functools.partial(z_chunk, c)] for c in range(n_chunks)]
    fillers += [[functools.partial(ga_chunk, c)] for c in range(n_chunks)]
    fillers += [[functools.partial(gg_chunk, c)] for c in range(n_chunks)]
    fillers += [[functools.partial(gm_chunk, c)] for c in range(n_chunks)]
    gd = [functools.partial(gmlp_dot, j, c) for j in range(G_GROUPS // 2) for c in range(0, nb, 2)]
    for q in range(len(gd)):
        fillers[n_chunks + q].append(gd[q])
    assert len(fillers) == len(items)

    fillers[0][0]()
    step_route()
    fillers[1][0]()
    step_route()
    step_route()
    fillers = fillers[2:] + [[], []]
    for n in ROUTE_AT:
        fillers[n].append(step_route)
    sts = {n: scores(n) for n in range(LOOKAHEAD)}
    for n in range(len(items)):
        finish(n, sts.pop(n))
        if n + LOOKAHEAD < len(items):
            sts[n + LOOKAHEAD] = scores(n + LOOKAHEAD)
        for f in fillers[n]:
            f()
    kl_ref[:, 0:BLOCK, :] = kl_ref[:, ts:ts + BLOCK, :]
    vt_ref[:, 0:BLOCK] = vt_ref[:, ts:ts + BLOCK]
    attn_b = attnt_ref[...].T.astype(bf16)

    mix = []
    for c in range(n_chunks):
        cols = slice(c * CHUNK_COLS, (c + 1) * CHUNK_COLS)
        a_c = jnp.dot(attn_b, wab_ref[:, cols], preferred_element_type=f32)
        mix.append((sga[c] * a_c + sgg[c] * gmc[c]).astype(bf16))
    mix_ref[cur] = jnp.concatenate(mix, axis=1)
    for _ in route:
        pass


def _const_spec(shape):
    nd = len(shape)
    return pl.BlockSpec(shape, lambda i: (0,) * nd)


def _mixer(x2, g1, win, rc, rs1, rs2, sinks, lng, lnb, wcat, bsf, wab, wgb, wout, g2, wrt, brt, tri, ltri, seq):
    t = x2.shape[0]
    n_blocks = t // TS
    blocks_per_seq = seq // TS
    cur_blk = lambda i: (jnp.minimum(i, n_blocks - 1), 0)
    prev_blk = lambda i: (jnp.maximum(i - 1, 0), 0)
    pos = lambda i: (jnp.minimum(i, n_blocks - 1) % blocks_per_seq, 0)
    consts = (sinks, lng, lnb, wcat, bsf, wab, wgb, wout, g2, wrt, brt, tri, ltri)
    in_specs = [
        pl.BlockSpec((TS, D_MODEL), cur_blk),
        pl.BlockSpec((TS, D_MODEL), prev_blk),
        _const_spec(g1.shape), _const_spec(win.shape),
        pl.BlockSpec((TS, LANES), pos), pl.BlockSpec((TS, LANES), pos), pl.BlockSpec((TS, LANES), pos),
    ] + [_const_spec(c.shape) for c in consts]
    out_shape = (
        jax.ShapeDtypeStruct((t, D_MODEL), jnp.float32),
        jax.ShapeDtypeStruct((n_blocks * BLOCK_ROWS, PACK_COLS), jnp.uint32),
        jax.ShapeDtypeStruct((t, LANES), jnp.float32),
        jax.ShapeDtypeStruct((n_blocks, N_EXPERTS, LANES), jnp.float32),
    )
    out_specs = (
        pl.BlockSpec((TS, D_MODEL), prev_blk),
        pl.BlockSpec((BLOCK_ROWS, PACK_COLS), prev_blk),
        pl.BlockSpec((TS, LANES), prev_blk),
        pl.BlockSpec((1, N_EXPERTS, LANES), lambda i: (jnp.maximum(i - 1, 0), 0, 0)),
    )
    scratch = [
        pltpu.VMEM((4, BLOCK + TS, LANES), jnp.bfloat16),
        pltpu.VMEM((LANES, BLOCK + TS), jnp.bfloat16),
        pltpu.VMEM((ATTN_WIDTH, TS), jnp.float32),
        pltpu.VMEM((TS, G_WIDTH), jnp.float32),
        pltpu.VMEM((2, TS, D_MODEL), jnp.bfloat16),
    ]
    return pl.pallas_call(
        functools.partial(_mixer_kernel, blocks_per_seq=blocks_per_seq, n_blocks=n_blocks),
        grid=(n_blocks + 1,),
        in_specs=in_specs,
        out_specs=out_specs,
        out_shape=out_shape,
        scratch_shapes=scratch,
        compiler_params=pltpu.CompilerParams(dimension_semantics=("arbitrary",),
                                             vmem_limit_bytes=VMEM_LIMIT),
        name="mixer",
    )(x2, x2, g1, win, rc, rs1, rs2, *consts)


def _expert_kernel(te_ref, tr0_ref, tb0_ref, tb1_ref, trows_ref, nt_ref, cum_ref, src_ref, used_ref,
                   xs_ref, wg_ref, wu_ref, wd_ref, ys_ref,
                   xbuf_ref, ybuf_ref, wgb_ref, wub_ref, wdb_ref, zero_ref, isem, osem, zsem,
                   *, n_blocks):
    j = pl.program_id(0)
    n_tiles = nt_ref[0]
    slot = j % 2
    f32, bf16 = jnp.float32, jnp.bfloat16

    def run_piece(tile, b, ok, fn):
        e, r0 = te_ref[tile], tr0_ref[tile]
        bb = jnp.minimum(b, n_blocks - 1)
        start = cum_ref[e * (n_blocks + 1) + bb]
        lo = jnp.maximum(start, r0)
        n = jnp.minimum(cum_ref[e * (n_blocks + 1) + bb + 1], r0 + TM) - lo

        @pl.when(ok & (b < tb1_ref[tile]) & (n > 0))
        def _():
            fn(pl.multiple_of(src_ref[e * n_blocks + bb] + (lo - start), ROW_ALIGN),
               pl.multiple_of(lo - r0, ROW_ALIGN), pl.multiple_of(n, ROW_ALIGN))

    def first_runs(tile, ok, fn):
        for k in range(FAST_RUNS):
            run_piece(tile, tb0_ref[tile] + k, ok, fn)

    def other_runs(tile, first, fn):
        lax.fori_loop(tb0_ref[tile] + first, tb1_ref[tile], lambda b, c: (run_piece(tile, b, True, fn), c)[1], 0)

    def gather_fn(s):
        return lambda src, dst, n: pltpu.make_async_copy(
            xs_ref.at[pl.ds(src, n)], xbuf_ref.at[s, pl.ds(dst, n)], isem.at[s]).start()

    def scatter_fn(s):
        return lambda src, dst, n: pltpu.make_async_copy(
            ybuf_ref.at[s, pl.ds(dst, n)], ys_ref.at[pl.ds(src, n)], osem.at[s]).start()

    def wait_rows(sem, n):
        @pl.when(n > 0)
        def _():
            rows = pl.multiple_of(n, ROW_ALIGN)
            pltpu.make_async_copy(xs_ref.at[pl.ds(0, rows)], xbuf_ref.at[0, pl.ds(0, rows)], sem).wait()

    def zero_tail_copy(b, c):
        first = used_ref[b] + c * TM
        n = pl.multiple_of(jnp.minimum(BLOCK_ROWS - first, TM), ROW_ALIGN)
        return pltpu.make_async_copy(
            zero_ref.at[pl.ds(0, n)], ys_ref.at[pl.ds(pl.multiple_of(b * BLOCK_ROWS + first, ROW_ALIGN), n)], zsem)

    @pl.when(j == 0)
    def _():
        xbuf_ref[...] = jnp.zeros_like(xbuf_ref)
        zero_ref[...] = jnp.zeros_like(zero_ref)

        def per_block(b, carry):
            chunks = (BLOCK_ROWS - used_ref[b] + TM - 1) // TM
            lax.fori_loop(0, chunks, lambda c, x: (zero_tail_copy(b, c).start(), x)[1], 0)
            lax.fori_loop(0, chunks, lambda c, x: (zero_tail_copy(b, c).wait(), x)[1], 0)
            return carry

        lax.fori_loop(0, n_blocks, per_block, 0)

        @pl.when(n_tiles > 0)
        def _():
            other_runs(0, 0, gather_fn(0))

    @pl.when((j >= 2) & (j - 2 < n_tiles))
    def _():
        wait_rows(osem.at[slot], trows_ref[jnp.maximum(j - 2, 0)])

    prev = jnp.maximum(j - 1, 0)

    @pl.when((j == n_tiles) & (j > 0))
    def _():
        other_runs(prev, 0, scatter_fn(1 - slot))

    @pl.when(j < n_tiles)
    def _():
        @pl.when((j == 0) | (te_ref[j] != te_ref[prev]))
        def _():
            wgb_ref[...] = wg_ref[0].astype(bf16)
            wub_ref[...] = wu_ref[0].astype(bf16)
            wdb_ref[...] = wd_ref[0].astype(bf16)

        wait_rows(isem.at[slot], trows_ref[j])
        xb = _unpack_bf16_rows(xbuf_ref[slot])
        nxt = jnp.minimum(j + 1, n_tiles - 1)
        first_runs(prev, j > 0, scatter_fn(1 - slot))
        first_runs(nxt, j + 1 < n_tiles, gather_fn(1 - slot))
        half = D_EXPERT // 2
        g0 = jnp.dot(xb, wgb_ref[:, 0:half], preferred_element_type=f32)
        u0 = jnp.dot(xb, wub_ref[:, 0:half], preferred_element_type=f32)
        g1 = jnp.dot(xb, wgb_ref[:, half:], preferred_element_type=f32)
        u1 = jnp.dot(xb, wub_ref[:, half:], preferred_element_type=f32)
        y = jnp.dot((jax.nn.silu(g0) * u0).astype(bf16), wdb_ref[0:half, :], preferred_element_type=f32)
        y = y + jnp.dot((jax.nn.silu(g1) * u1).astype(bf16), wdb_ref[half:, :], preferred_element_type=f32)
        packed = _pack_bf16_pair(y[:, 0:PACK_COLS], y[:, PACK_COLS:])
        probe = xbuf_ref[slot, 0:ROW_ALIGN, 0:LANES]
        ybuf_ref[slot] = packed
        ybuf_ref[slot, 0:ROW_ALIGN, 0:LANES] = packed[0:ROW_ALIGN, 0:LANES] | ((probe >> 16) >> 16)

        @pl.when(j + 1 < n_tiles)
        def _():
            other_runs(nxt, FAST_RUNS, gather_fn(1 - slot))

        @pl.when(j > 0)
        def _():
            other_runs(prev, FAST_RUNS, scatter_fn(1 - slot))


def _experts(tables, xs, wg, wu, wd, n_blocks, grid_tiles):
    wsel = lambda j, te, *_: (te[j], 0, 0)
    grid_spec = pltpu.PrefetchScalarGridSpec(
        num_scalar_prefetch=len(tables),
        grid=(grid_tiles,),
        in_specs=[
            pl.BlockSpec(memory_space=pl.ANY),
            pl.BlockSpec((1, D_MODEL, D_EXPERT), wsel),
            pl.BlockSpec((1, D_MODEL, D_EXPERT), wsel),
            pl.BlockSpec((1, D_EXPERT, D_MODEL), wsel),
        ],
        out_specs=pl.BlockSpec(memory_space=pl.ANY),
        scratch_shapes=[
            pltpu.VMEM((2, TM, PACK_COLS), jnp.uint32),
            pltpu.VMEM((2, TM, PACK_COLS), jnp.uint32),
            pltpu.VMEM((D_MODEL, D_EXPERT), jnp.bfloat16),
            pltpu.VMEM((D_MODEL, D_EXPERT), jnp.bfloat16),
            pltpu.VMEM((D_EXPERT, D_MODEL), jnp.bfloat16),
            pltpu.VMEM((TM, PACK_COLS), jnp.uint32),
            pltpu.SemaphoreType.DMA((2,)),
            pltpu.SemaphoreType.DMA((2,)),
            pltpu.SemaphoreType.DMA,
        ],
    )
    return pl.pallas_call(
        functools.partial(_expert_kernel, n_blocks=n_blocks),
        grid_spec=grid_spec,
        out_shape=jax.ShapeDtypeStruct(xs.shape, jnp.uint32),
        compiler_params=pltpu.CompilerParams(dimension_semantics=("arbitrary",),
                                             vmem_limit_bytes=VMEM_LIMIT),
        name="experts",
    )(*tables, xs, wg, wu, wd)


def _combine_kernel(x1_ref, rw_ref, gf_ref, ys_ref, o_ref):
    ts = x1_ref.shape[0]
    rw = rw_ref[...]
    io = lax.broadcasted_iota(jnp.int32, (ts, BLOCK_ROWS), 1).astype(jnp.float32)
    wsel = jnp.where(io == rw[:, 2:3], rw[:, 0:1], 0.0) + jnp.where(io == rw[:, 3:4], rw[:, 1:2], 0.0)
    y = jnp.dot(wsel.astype(jnp.bfloat16), _unpack_bf16_rows(ys_ref[...]), preferred_element_type=jnp.float32)
    o_ref[...] = _rms(x1_ref[...] + y, gf_ref[...])


def _combine(x1, rw, gf, ys):
    t = x1.shape[0]
    tok = lambda i: (i, 0)
    return pl.pallas_call(
        _combine_kernel,
        grid=(t // TS,),
        in_specs=[
            pl.BlockSpec((TS, D_MODEL), tok),
            pl.BlockSpec((TS, LANES), tok),
            pl.BlockSpec((1, D_MODEL), lambda i: (0, 0)),
            pl.BlockSpec((BLOCK_ROWS, PACK_COLS), tok),
        ],
        out_specs=pl.BlockSpec((TS, D_MODEL), tok),
        out_shape=jax.ShapeDtypeStruct((t, D_MODEL), jnp.float32),
        compiler_params=pltpu.CompilerParams(dimension_semantics=("arbitrary",),
                                             vmem_limit_bytes=VMEM_LIMIT),
        name="combine",
    )(x1, rw, gf, ys)


def _rope_lane_tables(seq):
    inv_freq = ROPE_THETA ** (-jnp.arange(0, ROT_DIM, 2, dtype=jnp.float32) / ROT_DIM)
    ang = jnp.arange(seq, dtype=jnp.float32)[:, None] * inv_freq[None, :]
    cos, sin = jnp.cos(ang), jnp.sin(ang)
    half = ROT_DIM // 2
    ones = jnp.ones((seq, HEAD_DIM - ROT_DIM), jnp.float32)
    zeros = jnp.zeros((seq, HEAD_DIM - ROT_DIM), jnp.float32)
    zh = jnp.zeros((seq, half), jnp.float32)
    c = jnp.concatenate([cos, cos, ones], axis=1)
    s1 = jnp.concatenate([zh, sin, zeros], axis=1)
    s2 = jnp.concatenate([-sin, zh, zeros], axis=1)
    rep = LANES // HEAD_DIM
    return jnp.tile(c, (1, rep)), jnp.tile(s1, (1, rep)), jnp.tile(s2, (1, rep))


def _pick(table, idx):
    return jnp.sum(jnp.where(idx[:, None] == jnp.arange(table.shape[0])[None, :], table[None, :], 0), axis=1)


def _tile_tables(cnt, n_blocks, grid_tiles):
    i32 = jnp.int32
    n = cnt[:, :, 0].astype(i32)
    n8 = (n + ROW_ALIGN - 1) // ROW_ALIGN * ROW_ALIGN
    loc = jnp.cumsum(n8, axis=1) - n8
    used = jnp.sum(n8, axis=1)
    cum = jnp.concatenate([jnp.zeros((1, N_EXPERTS), i32), jnp.cumsum(n8, axis=0)], axis=0).T
    total = cum[:, -1]
    src = (jnp.arange(n_blocks, dtype=i32)[:, None] * BLOCK_ROWS + loc).T
    tiles_e = (total + TM - 1) // TM
    tile_end = jnp.cumsum(tiles_e)
    n_tiles = tile_end[-1]
    j = jnp.arange(grid_tiles, dtype=i32)
    te = jnp.minimum(jnp.sum(tile_end[None, :] <= j[:, None], axis=1), N_EXPERTS - 1).astype(i32)
    live = j < n_tiles
    te = jnp.where(live, te, _pick(te, jnp.maximum(n_tiles - 1, 0)[None])[0])
    r0 = (j - _pick(tile_end - tiles_e, te)) * TM
    cum_t = cum[te]
    b0 = jnp.sum(cum_t[:, 1:] <= r0[:, None], axis=1)
    b1 = jnp.sum(cum_t[:, :-1] < (r0 + TM)[:, None], axis=1)
    rows = jnp.clip(_pick(total, te) - r0, 0, TM)
    zero = jnp.zeros_like(j)
    return (te, jnp.where(live, r0, zero), jnp.where(live, b0, zero).astype(i32),
            jnp.where(live, b1, zero).astype(i32), jnp.where(live, rows, zero).astype(i32),
            n_tiles.reshape(1).astype(i32), cum.reshape(-1), src.reshape(-1).astype(i32), used.astype(i32))


def kernel(x, norm1_g, w_in, attn_sinks, gmlp_ln_g, gmlp_ln_b, gmlp_ws, gmlp_bs, w_attn_branch,
           w_gmlp_branch, w_out, norm2_g, router_group_w, router_group_b, router_expert_w,
           router_expert_b, expert_w_gate, expert_w_up, expert_w_down, final_norm_g):
    b, s, d = x.shape
    assert d == D_MODEL and s % TS == 0 and norm1_g.shape[0] == 1
    t = b * s
    n_blocks = t // TS
    bf16, f32 = jnp.bfloat16, jnp.float32
    x2 = x.reshape(t, d)

    rc, rs1, rs2 = _rope_lane_tables(s)
    sk = attn_sinks[0].astype(f32)
    sinks = jnp.stack([
        jnp.concatenate([jnp.full((BLOCK,), 1.0, f32) * sk[4 * g + p], jnp.full((BLOCK,), 1.0, f32) * sk[4 * g + 2 + p]])
        for g in range(N_KV_HEADS) for p in range(2)])
    ws = gmlp_ws[0]
    wcat = jnp.stack([jnp.concatenate([ws[2 * j], ws[2 * j + 1]], axis=1) for j in range(G_GROUPS // 2)])
    bsf = jnp.repeat(gmlp_bs[0].T, G_GROUP_DIM, axis=1)
    wrt = jnp.concatenate([router_group_w[0].T, jnp.zeros((8 - N_GROUPS, d), f32), router_expert_w[0].T], axis=0)
    brt = jnp.concatenate([router_group_b[0], jnp.zeros((8 - N_GROUPS,), f32), router_expert_b[0]])
    brt = jnp.broadcast_to(brt[:, None], (ROUTER_ROWS, TS))
    tri = (jnp.arange(TS)[:, None] < jnp.arange(TS)[None, :]).astype(bf16)
    ltri = (jnp.arange(LANES)[None, :] < jnp.arange(N_EXPERTS)[:, None]).astype(bf16)

    x1, xs, rw, cnt = _mixer(
        x2, norm1_g, w_in[0].astype(bf16), rc, rs1, rs2, sinks, gmlp_ln_g, gmlp_ln_b, wcat, bsf,
        w_attn_branch[0].astype(bf16), w_gmlp_branch[0].astype(bf16), w_out[0].astype(bf16),
        norm2_g, wrt.astype(bf16), brt, tri, ltri, s)

    max_tiles = (TOP_K * t + n_blocks * N_EXPERTS * (ROW_ALIGN - 1)) // TM + N_EXPERTS
    grid_tiles = max_tiles + 2
    tables = _tile_tables(cnt, n_blocks, grid_tiles)
    ys = _experts(tables, xs, expert_w_gate[0], expert_w_up[0], expert_w_down[0], n_blocks, grid_tiles)
    out = _combine(x1, rw, final_norm_g.reshape(1, d), ys)
    return out.reshape(b, s, d)
```

```python
import functools
import math

import jax
import jax.numpy as jnp
from jax import lax
from jax.experimental import pallas as pl
from jax.experimental.pallas import tpu as pltpu

D_MODEL = 1024
HEAD_DIM = 64
N_HEADS = 8
N_KV_HEADS = 2
BLOCK = 128
ROT_DIM = HEAD_DIM // 4
ROPE_THETA = 500000.0
ATTN_WIDTH = N_HEADS * HEAD_DIM
KV_WIDTH = N_KV_HEADS * HEAD_DIM
G_GROUPS = 8
G_GROUP_DIM = 64
G_WIDTH = G_GROUPS * G_GROUP_DIM
Q_END = ATTN_WIDTH
K_END = Q_END + KV_WIDTH
V_END = K_END + KV_WIDTH
Z_END = V_END + 2 * G_WIDTH
GA_END = Z_END + D_MODEL
IN_COLS = GA_END + D_MODEL
N_GROUPS = 4
EXPERTS_PER_GROUP = 8
N_EXPERTS = N_GROUPS * EXPERTS_PER_GROUP
TOP_K = 2
D_EXPERT = 512
EPS = 1e-5
NEG_INF = -1e30

LANES = 128
ROW_ALIGN = 8
ROUTER_ROWS = 8 + N_EXPERTS
PACK_COLS = D_MODEL // 2

TS = 512
TM = 512
GATHER_AHEAD = 2
FAST_RUNS = 20
COMPACT_ROWS = 256
CHUNK_COLS = 256
LOOKAHEAD = 3
ROUTE_AT = (0, 1, 5, 14, 15)
BLOCK_ROWS = -(-(TOP_K * TS + N_EXPERTS * (ROW_ALIGN - 1)) // COMPACT_ROWS) * COMPACT_ROWS
VMEM_LIMIT = 58 * 1024 * 1024


def _rms(x, g):
    return x * lax.rsqrt(jnp.mean(x * x, axis=-1, keepdims=True) + EPS) * g


def _pack_bf16_pair(a, b):
    ua = lax.bitcast_convert_type(a.astype(jnp.bfloat16).astype(jnp.float32), jnp.uint32)
    ub = lax.bitcast_convert_type(b.astype(jnp.bfloat16).astype(jnp.float32), jnp.uint32)
    return ub | (ua >> 16)


def _unpack_bf16_rows(w):
    lo = lax.bitcast_convert_type(w << 16, jnp.float32)
    hi = lax.bitcast_convert_type(w & jnp.uint32(0xFFFF0000), jnp.float32)
    return jnp.concatenate([lo.astype(jnp.bfloat16), hi.astype(jnp.bfloat16)], axis=1)


def _route_and_compact(xnb, wrt_ref, brt_ref, tri_ref, ltri_ref, xs_ref, rw_ref, cnt_ref):
    ts = xnb.shape[0]
    f32, bf16 = jnp.float32, jnp.bfloat16
    lt = lax.dot_general(wrt_ref[...], xnb, (((1,), (1,)), ((), ())),
                         preferred_element_type=f32) + brt_ref[...]
    yield
    r = [lt[i:i + 1, :] for i in range(N_GROUPS)]
    gm_ = jnp.maximum(jnp.maximum(r[0], r[1]), jnp.maximum(r[2], r[3]))
    gsel = jnp.where(r[0] == gm_, 0.0, jnp.where(r[1] == gm_, 1.0, jnp.where(r[2] == gm_, 2.0, 3.0)))
    g_w = 1.0 / (jnp.exp(r[0] - gm_) + jnp.exp(r[1] - gm_) + jnp.exp(r[2] - gm_) + jnp.exp(r[3] - gm_))
    eg = [lt[8 + 8 * kk:16 + 8 * kk, :] for kk in range(N_GROUPS)]
    ein = jnp.where(gsel == 0.0, eg[0], jnp.where(gsel == 1.0, eg[1], jnp.where(gsel == 2.0, eg[2], eg[3])))
    io8 = lax.broadcasted_iota(jnp.int32, (EXPERTS_PER_GROUP, ts), 0).astype(f32)
    m1 = jnp.max(ein, axis=0, keepdims=True)
    i1 = jnp.min(jnp.where(ein == m1, io8, 8.0), axis=0, keepdims=True)
    e2 = jnp.where(io8 == i1, NEG_INF, ein)
    m2 = jnp.max(e2, axis=0, keepdims=True)
    i2 = jnp.min(jnp.where(e2 == m2, io8, 8.0), axis=0, keepdims=True)
    t2 = jnp.exp(m2 - m1)
    w1 = 1.0 / (1.0 + t2)
    wt0, wt1 = g_w * w1, g_w * (t2 * w1)
    eid0, eid1 = gsel * 8.0 + i1, gsel * 8.0 + i2

    io32 = lax.broadcasted_iota(jnp.int32, (N_EXPERTS, ts), 0).astype(f32)
    oh0, oh1 = io32 == eid0, io32 == eid1
    both = jnp.where(oh0, 1.0, jnp.where(oh1, 1.0, 0.0))
    pref = jnp.dot(both.astype(bf16), tri_ref[...], preferred_element_type=f32)
    yield
    n_e = jnp.sum(both, axis=1, keepdims=True)
    cnt_ref[0] = jnp.broadcast_to(n_e, (N_EXPERTS, LANES))
    units = jnp.broadcast_to(jnp.floor((n_e + (ROW_ALIGN - 1.0)) * (1.0 / ROW_ALIGN)), (N_EXPERTS, ts))
    units = jnp.concatenate([units, jnp.zeros((LANES - N_EXPERTS, ts), f32)], axis=0).astype(bf16)
    tot = ROW_ALIGN * jnp.dot(ltri_ref[...], units, preferred_element_type=f32) + pref
    yield
    slot0 = jnp.sum(jnp.where(oh0, tot, 0.0), axis=0, keepdims=True)
    slot1 = jnp.sum(jnp.where(oh1, tot, 0.0), axis=0, keepdims=True)

    io128 = lax.broadcasted_iota(jnp.int32, (LANES, ts), 0)
    rw_ref[...] = jnp.where(io128 == 0, wt0, jnp.where(io128 == 1, wt1,
                            jnp.where(io128 == 2, slot0, jnp.where(io128 == 3, slot1, 0.0)))).T

    s0i, s1i = slot0.astype(jnp.int32), slot1.astype(jnp.int32)
    for c in range(xs_ref.shape[0] // COMPACT_ROWS):
        io = lax.broadcasted_iota(jnp.int32, (COMPACT_ROWS, ts), 0) + c * COMPACT_ROWS
        onehot = jnp.where(io == s0i, 1.0, jnp.where(io == s1i, 1.0, 0.0)).astype(bf16)
        rows = jnp.dot(onehot, xnb, preferred_element_type=f32)
        xs_ref[c * COMPACT_ROWS:(c + 1) * COMPACT_ROWS, :] = _pack_bf16_pair(rows[:, 0:PACK_COLS], rows[:, PACK_COLS:])
        yield


def _mixer_kernel(x_ref, xp_ref, g1_ref, win_ref, rc_ref, rs1_ref, rs2_ref, sink_ref, lng_ref, lnb_ref,
                  wcat_ref, bsf_ref, wab_ref, wgb_ref, wout_ref, g2_ref, wrt_ref, brt_ref, tri_ref, ltri_ref,
                  x1_ref, xs_ref, rw_ref, cnt_ref,
                  kl_ref, vt_ref, attnt_ref, gated_ref, mix_ref, *, blocks_per_seq, n_blocks):
    step = pl.program_id(0)
    s_blk = jnp.minimum(step, n_blocks - 1) % blocks_per_seq
    cur = step % 2
    ts = x_ref.shape[0]
    nb = ts // BLOCK
    f32, bf16 = jnp.float32, jnp.bfloat16

    @pl.when(step == 0)
    def _():
        mix_ref[1] = jnp.zeros((ts, D_MODEL), bf16)

    @pl.when(s_blk == 0)
    def _():
        kl_ref[:, 0:BLOCK, :] = jnp.zeros((4, BLOCK, LANES), bf16)
        vt_ref[:, 0:BLOCK] = jnp.zeros((LANES, BLOCK), bf16)

    x1 = xp_ref[...] + jnp.dot(mix_ref[1 - cur], wout_ref[...], preferred_element_type=f32)
    x1_ref[...] = x1
    route = _route_and_compact(_rms(x1, g2_ref[...]).astype(bf16), wrt_ref, brt_ref, tri_ref, ltri_ref,
                               xs_ref, rw_ref, cnt_ref)
    step_route = lambda: next(route, None)

    x = x_ref[...]
    hb = _rms(x, g1_ref[...]).astype(bf16)

    qkv = jnp.dot(hb, win_ref[:, 0:V_END], preferred_element_type=f32)
    rc, rs1, rs2 = rc_ref[...], rs1_ref[...], rs2_ref[...]

    def rope(t):
        return t * rc + pltpu.roll(t, 8, 1) * rs1 + pltpu.roll(t, LANES - 8, 1) * rs2

    scale = 1.0 / math.sqrt(HEAD_DIM)
    qb = [(rope(qkv[:, j * LANES:(j + 1) * LANES]) * scale).astype(bf16) for j in range(4)]
    k = rope(qkv[:, Q_END:K_END])
    v = qkv[:, K_END:V_END]
    k_sw = pltpu.roll(k, HEAD_DIM, 1)
    lo = lax.broadcasted_iota(jnp.int32, (ts, LANES), 1) < HEAD_DIM
    zero = jnp.zeros_like(k)
    kl_ref[0, BLOCK:BLOCK + ts, :] = jnp.where(lo, k, zero).astype(bf16)
    kl_ref[1, BLOCK:BLOCK + ts, :] = jnp.where(lo, zero, k_sw).astype(bf16)
    kl_ref[2, BLOCK:BLOCK + ts, :] = jnp.where(lo, k_sw, zero).astype(bf16)
    kl_ref[3, BLOCK:BLOCK + ts, :] = jnp.where(lo, zero, k).astype(bf16)
    vt_ref[:, BLOCK:BLOCK + ts] = v.T.astype(bf16)

    kj = lax.broadcasted_iota(jnp.int32, (2 * BLOCK, 2 * BLOCK), 0)
    qi = lax.broadcasted_iota(jnp.int32, (2 * BLOCK, 2 * BLOCK), 1) % BLOCK
    band = (kj > qi) & (kj <= qi + BLOCK)
    kmin = jnp.where(s_blk == 0, BLOCK, 0)
    bias_first = jnp.where(band & (kj >= kmin), 0.0, NEG_INF).astype(f32)
    bias_rest = jnp.where(band, 0.0, NEG_INF).astype(f32)
    items = [(i, g, p) for i in range(nb) for g in range(N_KV_HEADS) for p in range(2)]

    def scores(n):
        i, g, p = items[n]
        rows = slice(i * BLOCK, (i + 1) * BLOCK)
        qs = jnp.concatenate([qb[2 * g][rows], qb[2 * g + 1][rows]], axis=0)
        st = lax.dot_general(kl_ref[2 * g + p, i * BLOCK:(i + 2) * BLOCK, :], qs, (((1,), (1,)), ((), ())),
                             preferred_element_type=f32)
        return st + (bias_first if i == 0 else bias_rest)

    def finish(n, st):
        i, g, p = items[n]
        rows = slice(i * BLOCK, (i + 1) * BLOCK)
        sink = sink_ref[2 * g + p:2 * g + p + 1, :]
        m = jnp.maximum(jnp.max(st, axis=0, keepdims=True), sink)
        e = jnp.exp(st - m)
        den = jnp.sum(e, axis=0, keepdims=True) + jnp.exp(sink - m)
        ot = jnp.dot(vt_ref[g * HEAD_DIM:(g + 1) * HEAD_DIM, i * BLOCK:(i + 2) * BLOCK], e.astype(bf16),
                     preferred_element_type=f32) * (1.0 / den)
        h0 = 4 * g + p
        attnt_ref[h0 * HEAD_DIM:(h0 + 1) * HEAD_DIM, rows] = ot[:, 0:BLOCK]
        attnt_ref[(h0 + 2) * HEAD_DIM:(h0 + 3) * HEAD_DIM, rows] = ot[:, BLOCK:2 * BLOCK]

    n_chunks = D_MODEL // CHUNK_COLS
    zc, sga, sgg, gmc = [None] * n_chunks, [None] * n_chunks, [None] * n_chunks, [None] * n_chunks
    vn_box = []

    def proj(lo_col, c):
        cols = slice(lo_col + c * CHUNK_COLS, lo_col + (c + 1) * CHUNK_COLS)
        return jnp.dot(hb, win_ref[:, cols], preferred_element_type=f32)

    def z_chunk(c):
        zc[c] = jax.nn.gelu(proj(V_END, c))
        if c == n_chunks - 1:
            v2 = jnp.concatenate(zc[n_chunks // 2:], axis=1)
            mu = jnp.mean(v2, axis=-1, keepdims=True)
            vc = v2 - mu
            var = jnp.mean(vc * vc, axis=-1, keepdims=True)
            vn_box.append((vc * lax.rsqrt(var + EPS) * lng_ref[...] + lnb_ref[...]).astype(bf16))

    def ga_chunk(c):
        sga[c] = jax.nn.sigmoid(proj(Z_END, c))

    def gg_chunk(c):
        sgg[c] = jax.nn.sigmoid(proj(GA_END, c))

    wq = lax.broadcasted_iota(jnp.int32, (BLOCK, 2 * BLOCK), 0)
    wp = lax.broadcasted_iota(jnp.int32, (BLOCK, 2 * BLOCK), 1) % BLOCK
    causal = wp <= wq
    lo_c = lax.broadcasted_iota(jnp.int32, (BLOCK, LANES), 1) < G_GROUP_DIM
    wjs = {}

    def gmlp_dot(j, c):
        if j not in wjs:
            wjs[j] = jnp.where(causal, wcat_ref[j], 0.0).astype(bf16)
        cols = slice(j * LANES, (j + 1) * LANES)
        rhs = []
        for cc in (c, c + 1):
            vp = vn_box[0][cc * BLOCK:(cc + 1) * BLOCK, cols]
            zb = jnp.zeros_like(vp)
            rhs.append(jnp.concatenate([jnp.where(lo_c, vp, zb), jnp.where(lo_c, zb, vp)], axis=0))
        mixed = jnp.dot(wjs[j], jnp.concatenate(rhs, axis=1), preferred_element_type=f32)
        u = zc[j // 2][:, (j % 2) * LANES:(j % 2 + 1) * LANES]
        for q, cc in enumerate((c, c + 1)):
            rows = slice(cc * BLOCK, (cc + 1) * BLOCK)
            gated_ref[rows, cols] = u[rows] * (mixed[:, q * LANES:(q + 1) * LANES] + bsf_ref[:, cols])

    def gm_chunk(c):
        cols = slice(c * CHUNK_COLS, (c + 1) * CHUNK_COLS)
        gmc[c] = jnp.dot(gated_ref[...].astype(bf16), wgb_ref[:, cols], preferred_element_type=f32)

    fillers = [[functools.partial(z_chunk, c)] for c in range(n_chunks)]
    fillers += [[functools.partial(ga_chunk, c)] for c in range(n_chunks)]
    fillers += [[functools.partial(gg_chunk, c)] for c in range(n_chunks)]
    fillers += [[functools.partial(gm_chunk, c)] for c in range(n_chunks)]
    gd = [functools.partial(gmlp_dot, j, c) for j in range(G_GROUPS // 2) for c in range(0, nb, 2)]
    for q in range(len(gd)):
        fillers[n_chunks + q].append(gd[q])
    assert len(fillers) == len(items)

    fillers[0][0]()
    step_route()
    fillers[1][0]()
    step_route()
    step_route()
    fillers = fillers[2:] + [[], []]
    for n in ROUTE_AT:
        fillers[n].append(step_route)
    sts = {n: scores(n) for n in range(LOOKAHEAD)}
    for n in range(len(items)):
        finish(n, sts.pop(n))
        if n + LOOKAHEAD < len(items):
            sts[n + LOOKAHEAD] = scores(n + LOOKAHEAD)
        for f in fillers[n]:
            f()
    kl_ref[:, 0:BLOCK, :] = kl_ref[:, ts:ts + BLOCK, :]
    vt_ref[:, 0:BLOCK] = vt_ref[:, ts:ts + BLOCK]
    attn_b = attnt_ref[...].T.astype(bf16)

    mix = []
    for c in range(n_chunks):
        cols = slice(c * CHUNK_COLS, (c + 1) * CHUNK_COLS)
        a_c = jnp.dot(attn_b, wab_ref[:, cols], preferred_element_type=f32)
        mix.append((sga[c] * a_c + sgg[c] * gmc[c]).astype(bf16))
    mix_ref[cur] = jnp.concatenate(mix, axis=1)
    for _ in route:
        pass


def _const_spec(shape):
    nd = len(shape)
    return pl.BlockSpec(shape, lambda i: (0,) * nd)


def _mixer(x2, g1, win, rc, rs1, rs2, sinks, lng, lnb, wcat, bsf, wab, wgb, wout, g2, wrt, brt, tri, ltri, seq):
    t = x2.shape[0]
    n_blocks = t // TS
    blocks_per_seq = seq // TS
    cur_blk = lambda i: (jnp.minimum(i, n_blocks - 1), 0)
    prev_blk = lambda i: (jnp.maximum(i - 1, 0), 0)
    pos = lambda i: (jnp.minimum(i, n_blocks - 1) % blocks_per_seq, 0)
    consts = (sinks, lng, lnb, wcat, bsf, wab, wgb, wout, g2, wrt, brt, tri, ltri)
    in_specs = [
        pl.BlockSpec((TS, D_MODEL), cur_blk),
        pl.BlockSpec((TS, D_MODEL), prev_blk),
        _const_spec(g1.shape), _const_spec(win.shape),
        pl.BlockSpec((TS, LANES), pos), pl.BlockSpec((TS, LANES), pos), pl.BlockSpec((TS, LANES), pos),
    ] + [_const_spec(c.shape) for c in consts]
    out_shape = (
        jax.ShapeDtypeStruct((t, D_MODEL), jnp.float32),
        jax.ShapeDtypeStruct((n_blocks * BLOCK_ROWS, PACK_COLS), jnp.uint32),
        jax.ShapeDtypeStruct((t, LANES), jnp.float32),
        jax.ShapeDtypeStruct((n_blocks, N_EXPERTS, LANES), jnp.float32),
    )
    out_specs = (
        pl.BlockSpec((TS, D_MODEL), prev_blk),
        pl.BlockSpec((BLOCK_ROWS, PACK_COLS), prev_blk),
        pl.BlockSpec((TS, LANES), prev_blk),
        pl.BlockSpec((1, N_EXPERTS, LANES), lambda i: (jnp.maximum(i - 1, 0), 0, 0)),
    )
    scratch = [
        pltpu.VMEM((4, BLOCK + TS, LANES), jnp.bfloat16),
        pltpu.VMEM((LANES, BLOCK + TS), jnp.bfloat16),
        pltpu.VMEM((ATTN_WIDTH, TS), jnp.float32),
        pltpu.VMEM((TS, G_WIDTH), jnp.float32),
        pltpu.VMEM((2, TS, D_MODEL), jnp.bfloat16),
    ]
    return pl.pallas_call(
        functools.partial(_mixer_kernel, blocks_per_seq=blocks_per_seq, n_blocks=n_blocks),
        grid=(n_blocks + 1,),
        in_specs=in_specs,
        out_specs=out_specs,
        out_shape=out_shape,
        scratch_shapes=scratch,
        compiler_params=pltpu.CompilerParams(dimension_semantics=("arbitrary",),
                                             vmem_limit_bytes=VMEM_LIMIT),
        name="mixer",
    )(x2, x2, g1, win, rc, rs1, rs2, *consts)


def _expert_kernel(te_ref, tr0_ref, tb0_ref, tb1_ref, trows_ref, nt_ref, cum_ref, src_ref, used_ref,
                   xs_ref, wg_ref, wu_ref, wd_ref, ys_ref,
                   xbuf_ref, ybuf_ref, wgb_ref, wub_ref, wdb_ref, zero_ref, isem, osem, zsem,
                   *, n_blocks):
    j = pl.program_id(0)
    n_tiles = nt_ref[0]
    slot = j % 2
    f32, bf16 = jnp.float32, jnp.bfloat16

    def run_piece(tile, b, ok, fn):
        e, r0 = te_ref[tile], tr0_ref[tile]
        bb = jnp.minimum(b, n_blocks - 1)
        start = cum_ref[e * (n_blocks + 1) + bb]
        lo = jnp.maximum(start, r0)
        n = jnp.minimum(cum_ref[e * (n_blocks + 1) + bb + 1], r0 + TM) - lo

        @pl.when(ok & (b < tb1_ref[tile]) & (n > 0))
        def _():
            fn(pl.multiple_of(src_ref[e * n_blocks + bb] + (lo - start), ROW_ALIGN),
               pl.multiple_of(lo - r0, ROW_ALIGN), pl.multiple_of(n, ROW_ALIGN))

    def first_runs(tile, ok, fn):
        for k in range(FAST_RUNS):
            run_piece(tile, tb0_ref[tile] + k, ok, fn)

    def other_runs(tile, first, fn):
        lax.fori_loop(tb0_ref[tile] + first, tb1_ref[tile], lambda b, c: (run_piece(tile, b, True, fn), c)[1], 0)

    def gather_fn(s):
        return lambda src, dst, n: pltpu.make_async_copy(
            xs_ref.at[pl.ds(src, n)], xbuf_ref.at[s, pl.ds(dst, n)], isem.at[s]).start()

    def scatter_fn(s):
        return lambda src, dst, n: pltpu.make_async_copy(
            ybuf_ref.at[s, pl.ds(dst, n)], ys_ref.at[pl.ds(src, n)], osem.at[s]).start()

    def wait_rows(sem, n):
        @pl.when(n > 0)
        def _():
            rows = pl.multiple_of(n, ROW_ALIGN)
            pltpu.make_async_copy(xs_ref.at[pl.ds(0, rows)], xbuf_ref.at[0, pl.ds(0, rows)], sem).wait()

    def zero_tail_copy(b, c):
        first = used_ref[b] + c * TM
        n = pl.multiple_of(jnp.minimum(BLOCK_ROWS - first, TM), ROW_ALIGN)
        return pltpu.make_async_copy(
            zero_ref.at[pl.ds(0, n)], ys_ref.at[pl.ds(pl.multiple_of(b * BLOCK_ROWS + first, ROW_ALIGN), n)], zsem)

    @pl.when(j == 0)
    def _():
        xbuf_ref[...] = jnp.zeros_like(xbuf_ref)
        zero_ref[...] = jnp.zeros_like(zero_ref)

        def per_block(b, carry):
            chunks = (BLOCK_ROWS - used_ref[b] + TM - 1) // TM
            lax.fori_loop(0, chunks, lambda c, x: (zero_tail_copy(b, c).start(), x)[1], 0)
            lax.fori_loop(0, chunks, lambda c, x: (zero_tail_copy(b, c).wait(), x)[1], 0)
            return carry

        lax.fori_loop(0, n_blocks, per_block, 0)

        for t0 in range(GATHER_AHEAD):
            @pl.when(t0 < n_tiles)
            def _():
                other_runs(t0, 0, gather_fn(t0))

    prev = jnp.maximum(j - 1, 0)
    xslot = j % (GATHER_AHEAD + 1)
    nslot = (j + GATHER_AHEAD) % (GATHER_AHEAD + 1)

    @pl.when((j == n_tiles) & (j > 0))
    def _():
        other_runs(prev, 0, scatter_fn(1 - slot))

    @pl.when((j >= n_tiles) & (j >= 2) & (j - 2 < n_tiles))
    def _():
        wait_rows(osem.at[slot], trows_ref[jnp.maximum(j - 2, 0)])

    @pl.when(j < n_tiles)
    def _():
        @pl.when((j == 0) | (te_ref[j] != te_ref[prev]))
        def _():
            wgb_ref[...] = wg_ref[0].astype(bf16)
            wub_ref[...] = wu_ref[0].astype(bf16)
            wdb_ref[...] = wd_ref[0].astype(bf16)

        wait_rows(isem.at[xslot], trows_ref[j])
        xb = _unpack_bf16_rows(xbuf_ref[xslot])
        nxt = jnp.minimum(j + GATHER_AHEAD, n_tiles - 1)
        more = j + GATHER_AHEAD < n_tiles
        first_runs(prev, j > 0, scatter_fn(1 - slot))
        first_runs(nxt, more, gather_fn(nslot))
        half = D_EXPERT // 2
        g0 = jnp.dot(xb, wgb_ref[:, 0:half], preferred_element_type=f32)
        u0 = jnp.dot(xb, wub_ref[:, 0:half], preferred_element_type=f32)
        g1 = jnp.dot(xb, wgb_ref[:, half:], preferred_element_type=f32)
        u1 = jnp.dot(xb, wub_ref[:, half:], preferred_element_type=f32)
        y = jnp.dot((jax.nn.silu(g0) * u0).astype(bf16), wdb_ref[0:half, :], preferred_element_type=f32)
        y = y + jnp.dot((jax.nn.silu(g1) * u1).astype(bf16), wdb_ref[half:, :], preferred_element_type=f32)
        packed = _pack_bf16_pair(y[:, 0:PACK_COLS], y[:, PACK_COLS:])
        probe = xbuf_ref[xslot, 0:ROW_ALIGN, 0:LANES]

        @pl.when(j >= 2)
        def _():
            wait_rows(osem.at[slot], trows_ref[jnp.maximum(j - 2, 0)])

        ybuf_ref[slot] = packed
        ybuf_ref[slot, 0:ROW_ALIGN, 0:LANES] = packed[0:ROW_ALIGN, 0:LANES] | ((probe >> 16) >> 16)

        @pl.when(more)
        def _():
            other_runs(nxt, FAST_RUNS, gather_fn(nslot))

        @pl.when(j > 0)
        def _():
            other_runs(prev, FAST_RUNS, scatter_fn(1 - slot))


def _experts(tables, xs, wg, wu, wd, n_blocks, grid_tiles):
    wsel = lambda j, te, *_: (te[j], 0, 0)
    grid_spec = pltpu.PrefetchScalarGridSpec(
        num_scalar_prefetch=len(tables),
        grid=(grid_tiles,),
        in_specs=[
            pl.BlockSpec(memory_space=pl.ANY),
            pl.BlockSpec((1, D_MODEL, D_EXPERT), wsel),
            pl.BlockSpec((1, D_MODEL, D_EXPERT), wsel),
            pl.BlockSpec((1, D_EXPERT, D_MODEL), wsel),
        ],
        out_specs=pl.BlockSpec(memory_space=pl.ANY),
        scratch_shapes=[
            pltpu.VMEM((GATHER_AHEAD + 1, TM, PACK_COLS), jnp.uint32),
            pltpu.VMEM((2, TM, PACK_COLS), jnp.uint32),
            pltpu.VMEM((D_MODEL, D_EXPERT), jnp.bfloat16),
            pltpu.VMEM((D_MODEL, D_EXPERT), jnp.bfloat16),
            pltpu.VMEM((D_EXPERT, D_MODEL), jnp.bfloat16),
            pltpu.VMEM((TM, PACK_COLS), jnp.uint32),
            pltpu.SemaphoreType.DMA((GATHER_AHEAD + 1,)),
            pltpu.SemaphoreType.DMA((2,)),
            pltpu.SemaphoreType.DMA,
        ],
    )
    return pl.pallas_call(
        functools.partial(_expert_kernel, n_blocks=n_blocks),
        grid_spec=grid_spec,
        out_shape=jax.ShapeDtypeStruct(xs.shape, jnp.uint32),
        compiler_params=pltpu.CompilerParams(dimension_semantics=("arbitrary",),
                                             vmem_limit_bytes=VMEM_LIMIT),
        name="experts",
    )(*tables, xs, wg, wu, wd)


def _combine_kernel(x1_ref, rw_ref, gf_ref, ys_ref, o_ref):
    ts = x1_ref.shape[0]
    rw = rw_ref[...]
    io = lax.broadcasted_iota(jnp.int32, (ts, BLOCK_ROWS), 1).astype(jnp.float32)
    wsel = jnp.where(io == rw[:, 2:3], rw[:, 0:1], 0.0) + jnp.where(io == rw[:, 3:4], rw[:, 1:2], 0.0)
    y = jnp.dot(wsel.astype(jnp.bfloat16), _unpack_bf16_rows(ys_ref[...]), preferred_element_type=jnp.float32)
    o_ref[...] = _rms(x1_ref[...] + y, gf_ref[...])


def _combine(x1, rw, gf, ys):
    t = x1.shape[0]
    tok = lambda i: (i, 0)
    return pl.pallas_call(
        _combine_kernel,
        grid=(t // TS,),
        in_specs=[
            pl.BlockSpec((TS, D_MODEL), tok),
            pl.BlockSpec((TS, LANES), tok),
            pl.BlockSpec((1, D_MODEL), lambda i: (0, 0)),
            pl.BlockSpec((BLOCK_ROWS, PACK_COLS), tok),
        ],
        out_specs=pl.BlockSpec((TS, D_MODEL), tok),
        out_shape=jax.ShapeDtypeStruct((t, D_MODEL), jnp.float32),
        compiler_params=pltpu.CompilerParams(dimension_semantics=("arbitrary",),
                                             vmem_limit_bytes=VMEM_LIMIT),
        name="combine",
    )(x1, rw, gf, ys)


def _rope_lane_tables(seq):
    inv_freq = ROPE_THETA ** (-jnp.arange(0, ROT_DIM, 2, dtype=jnp.float32) / ROT_DIM)
    ang = jnp.arange(seq, dtype=jnp.float32)[:, None] * inv_freq[None, :]
    cos, sin = jnp.cos(ang), jnp.sin(ang)
    half = ROT_DIM // 2
    ones = jnp.ones((seq, HEAD_DIM - ROT_DIM), jnp.float32)
    zeros = jnp.zeros((seq, HEAD_DIM - ROT_DIM), jnp.float32)
    zh = jnp.zeros((seq, half), jnp.float32)
    c = jnp.concatenate([cos, cos, ones], axis=1)
    s1 = jnp.concatenate([zh, sin, zeros], axis=1)
    s2 = jnp.concatenate([-sin, zh, zeros], axis=1)
    rep = LANES // HEAD_DIM
    return jnp.tile(c, (1, rep)), jnp.tile(s1, (1, rep)), jnp.tile(s2, (1, rep))


def _pick(table, idx):
    return jnp.sum(jnp.where(idx[:, None] == jnp.arange(table.shape[0])[None, :], table[None, :], 0), axis=1)


def _tile_tables(cnt, n_blocks, grid_tiles):
    i32 = jnp.int32
    n = cnt[:, :, 0].astype(i32)
    n8 = (n + ROW_ALIGN - 1) // ROW_ALIGN * ROW_ALIGN
    loc = jnp.cumsum(n8, axis=1) - n8
    used = jnp.sum(n8, axis=1)
    cum = jnp.concatenate([jnp.zeros((1, N_EXPERTS), i32), jnp.cumsum(n8, axis=0)], axis=0).T
    total = cum[:, -1]
    src = (jnp.arange(n_blocks, dtype=i32)[:, None] * BLOCK_ROWS + loc).T
    tiles_e = (total + TM - 1) // TM
    tile_end = jnp.cumsum(tiles_e)
    n_tiles = tile_end[-1]
    j = jnp.arange(grid_tiles, dtype=i32)
    te = jnp.minimum(jnp.sum(tile_end[None, :] <= j[:, None], axis=1), N_EXPERTS - 1).astype(i32)
    live = j < n_tiles
    te = jnp.where(live, te, _pick(te, jnp.maximum(n_tiles - 1, 0)[None])[0])
    r0 = (j - _pick(tile_end - tiles_e, te)) * TM
    cum_t = cum[te]
    b0 = jnp.sum(cum_t[:, 1:] <= r0[:, None], axis=1)
    b1 = jnp.sum(cum_t[:, :-1] < (r0 + TM)[:, None], axis=1)
    rows = jnp.clip(_pick(total, te) - r0, 0, TM)
    zero = jnp.zeros_like(j)
    return (te, jnp.where(live, r0, zero), jnp.where(live, b0, zero).astype(i32),
            jnp.where(live, b1, zero).astype(i32), jnp.where(live, rows, zero).astype(i32),
            n_tiles.reshape(1).astype(i32), cum.reshape(-1), src.reshape(-1).astype(i32), used.astype(i32))


def kernel(x, norm1_g, w_in, attn_sinks, gmlp_ln_g, gmlp_ln_b, gmlp_ws, gmlp_bs, w_attn_branch,
           w_gmlp_branch, w_out, norm2_g, router_group_w, router_group_b, router_expert_w,
           router_expert_b, expert_w_gate, expert_w_up, expert_w_down, final_norm_g):
    b, s, d = x.shape
    assert d == D_MODEL and s % TS == 0 and norm1_g.shape[0] == 1
    t = b * s
    n_blocks = t // TS
    bf16, f32 = jnp.bfloat16, jnp.float32
    x2 = x.reshape(t, d)

    rc, rs1, rs2 = _rope_lane_tables(s)
    sk = attn_sinks[0].astype(f32)
    sinks = jnp.stack([
        jnp.concatenate([jnp.full((BLOCK,), 1.0, f32) * sk[4 * g + p], jnp.full((BLOCK,), 1.0, f32) * sk[4 * g + 2 + p]])
        for g in range(N_KV_HEADS) for p in range(2)])
    ws = gmlp_ws[0]
    wcat = jnp.stack([jnp.concatenate([ws[2 * j], ws[2 * j + 1]], axis=1) for j in range(G_GROUPS // 2)])
    bsf = jnp.repeat(gmlp_bs[0].T, G_GROUP_DIM, axis=1)
    wrt = jnp.concatenate([router_group_w[0].T, jnp.zeros((8 - N_GROUPS, d), f32), router_expert_w[0].T], axis=0)
    brt = jnp.concatenate([router_group_b[0], jnp.zeros((8 - N_GROUPS,), f32), router_expert_b[0]])
    brt = jnp.broadcast_to(brt[:, None], (ROUTER_ROWS, TS))
    tri = (jnp.arange(TS)[:, None] < jnp.arange(TS)[None, :]).astype(bf16)
    ltri = (jnp.arange(LANES)[None, :] < jnp.arange(N_EXPERTS)[:, None]).astype(bf16)

    x1, xs, rw, cnt = _mixer(
        x2, norm1_g, w_in[0].astype(bf16), rc, rs1, rs2, sinks, gmlp_ln_g, gmlp_ln_b, wcat, bsf,
        w_attn_branch[0].astype(bf16), w_gmlp_branch[0].astype(bf16), w_out[0].astype(bf16),
        norm2_g, wrt.astype(bf16), brt, tri, ltri, s)

    max_tiles = (TOP_K * t + n_blocks * N_EXPERTS * (ROW_ALIGN - 1)) // TM + N_EXPERTS
    grid_tiles = max_tiles + 2
    tables = _tile_tables(cnt, n_blocks, grid_tiles)
    ys = _experts(tables, xs, expert_w_gate[0], expert_w_up[0], expert_w_down[0], n_blocks, grid_tiles)
    out = _combine(x1, rw, final_norm_g.reshape(1, d), ys)
    return out.reshape(b, s, d)
```

```python
import functools
import math

import jax
import jax.numpy as jnp
from jax import lax
from jax.experimental import pallas as pl
from jax.experimental.pallas import tpu as pltpu

D_MODEL = 1024
HEAD_DIM = 64
N_HEADS = 8
N_KV_HEADS = 2
BLOCK = 128
ROT_DIM = HEAD_DIM // 4
ROPE_THETA = 500000.0
ATTN_WIDTH = N_HEADS * HEAD_DIM
KV_WIDTH = N_KV_HEADS * HEAD_DIM
G_GROUPS = 8
G_GROUP_DIM = 64
G_WIDTH = G_GROUPS * G_GROUP_DIM
Q_END = ATTN_WIDTH
K_END = Q_END + KV_WIDTH
V_END = K_END + KV_WIDTH
Z_END = V_END + 2 * G_WIDTH
GA_END = Z_END + D_MODEL
IN_COLS = GA_END + D_MODEL
N_GROUPS = 4
EXPERTS_PER_GROUP = 8
N_EXPERTS = N_GROUPS * EXPERTS_PER_GROUP
TOP_K = 2
D_EXPERT = 512
EPS = 1e-5
NEG_INF = -1e30

LANES = 128
SUBLANES = 8
ROUTER_ROWS = 8 + N_EXPERTS
PACK_COLS = D_MODEL // 2
ROW_LINES = PACK_COLS // LANES
ROW_ALIGN = SUBLANES // ROW_LINES

TS = 512
TM = 512
GATHER_AHEAD = 2
FAST_RUNS = 20
COMPACT_ROWS = 384
CHUNK_COLS = 256
LOOKAHEAD = 3
ROUTE_AT = (0, 1, 5, 14, 15)
BLOCK_ROWS = -(-(TOP_K * TS + N_EXPERTS * (ROW_ALIGN - 1)) // COMPACT_ROWS) * COMPACT_ROWS
VMEM_LIMIT = 58 * 1024 * 1024


def _rms(x, g):
    return x * lax.rsqrt(jnp.mean(x * x, axis=-1, keepdims=True) + EPS) * g


def _pack_bf16_pair(a, b):
    ua = lax.bitcast_convert_type(a.astype(jnp.bfloat16).astype(jnp.float32), jnp.uint32)
    ub = lax.bitcast_convert_type(b.astype(jnp.bfloat16).astype(jnp.float32), jnp.uint32)
    return ub | (ua >> 16)


def _unpack_bf16_rows(w):
    lo = lax.bitcast_convert_type(w << 16, jnp.float32)
    hi = lax.bitcast_convert_type(w & jnp.uint32(0xFFFF0000), jnp.float32)
    return jnp.concatenate([lo.astype(jnp.bfloat16), hi.astype(jnp.bfloat16)], axis=1)


def _load_rows(lines_ref, first_row, rows):
    return jnp.concatenate([lines_ref[pl.ds(first_row * ROW_LINES + c, rows, stride=ROW_LINES), :]
                            for c in range(ROW_LINES)], axis=1)


def _store_rows(lines_ref, first_row, packed):
    for c in range(ROW_LINES):
        lines_ref[pl.ds(first_row * ROW_LINES + c, packed.shape[0], stride=ROW_LINES), :] = \
            packed[:, c * LANES:(c + 1) * LANES]


def _route_and_compact(xnb, wrt_ref, brt_ref, tri_ref, ltri_ref, xs_ref, rw_ref, cnt_ref):
    ts = xnb.shape[0]
    f32, bf16 = jnp.float32, jnp.bfloat16
    lt = lax.dot_general(wrt_ref[...], xnb, (((1,), (1,)), ((), ())),
                         preferred_element_type=f32) + brt_ref[...]
    yield
    r = [lt[i:i + 1, :] for i in range(N_GROUPS)]
    gm_ = jnp.maximum(jnp.maximum(r[0], r[1]), jnp.maximum(r[2], r[3]))
    gsel = jnp.where(r[0] == gm_, 0.0, jnp.where(r[1] == gm_, 1.0, jnp.where(r[2] == gm_, 2.0, 3.0)))
    g_w = 1.0 / (jnp.exp(r[0] - gm_) + jnp.exp(r[1] - gm_) + jnp.exp(r[2] - gm_) + jnp.exp(r[3] - gm_))
    eg = [lt[8 + 8 * kk:16 + 8 * kk, :] for kk in range(N_GROUPS)]
    ein = jnp.where(gsel == 0.0, eg[0], jnp.where(gsel == 1.0, eg[1], jnp.where(gsel == 2.0, eg[2], eg[3])))
    io8 = lax.broadcasted_iota(jnp.int32, (EXPERTS_PER_GROUP, ts), 0).astype(f32)
    m1 = jnp.max(ein, axis=0, keepdims=True)
    i1 = jnp.min(jnp.where(ein == m1, io8, 8.0), axis=0, keepdims=True)
    e2 = jnp.where(io8 == i1, NEG_INF, ein)
    m2 = jnp.max(e2, axis=0, keepdims=True)
    i2 = jnp.min(jnp.where(e2 == m2, io8, 8.0), axis=0, keepdims=True)
    t2 = jnp.exp(m2 - m1)
    w1 = 1.0 / (1.0 + t2)
    wt0, wt1 = g_w * w1, g_w * (t2 * w1)
    eid0, eid1 = gsel * 8.0 + i1, gsel * 8.0 + i2

    io32 = lax.broadcasted_iota(jnp.int32, (N_EXPERTS, ts), 0).astype(f32)
    oh0, oh1 = io32 == eid0, io32 == eid1
    both = jnp.where(oh0, 1.0, jnp.where(oh1, 1.0, 0.0))
    pref = jnp.dot(both.astype(bf16), tri_ref[...], preferred_element_type=f32)
    yield
    n_e = jnp.sum(both, axis=1, keepdims=True)
    cnt_ref[0] = jnp.broadcast_to(n_e, (N_EXPERTS, LANES))
    units = jnp.broadcast_to(jnp.floor((n_e + (ROW_ALIGN - 1.0)) * (1.0 / ROW_ALIGN)), (N_EXPERTS, ts))
    units = jnp.concatenate([units, jnp.zeros((LANES - N_EXPERTS, ts), f32)], axis=0).astype(bf16)
    tot = ROW_ALIGN * jnp.dot(ltri_ref[...], units, preferred_element_type=f32) + pref
    yield
    slot0 = jnp.sum(jnp.where(oh0, tot, 0.0), axis=0, keepdims=True)
    slot1 = jnp.sum(jnp.where(oh1, tot, 0.0), axis=0, keepdims=True)

    io128 = lax.broadcasted_iota(jnp.int32, (LANES, ts), 0)
    rw_ref[...] = jnp.where(io128 == 0, wt0, jnp.where(io128 == 1, wt1,
                            jnp.where(io128 == 2, slot0, jnp.where(io128 == 3, slot1, 0.0)))).T

    s0i, s1i = slot0.astype(jnp.int32), slot1.astype(jnp.int32)
    for c in range(BLOCK_ROWS // COMPACT_ROWS):
        io = lax.broadcasted_iota(jnp.int32, (COMPACT_ROWS, ts), 0) + c * COMPACT_ROWS
        onehot = jnp.where(io == s0i, 1.0, jnp.where(io == s1i, 1.0, 0.0)).astype(bf16)
        rows = jnp.dot(onehot, xnb, preferred_element_type=f32)
        _store_rows(xs_ref, c * COMPACT_ROWS, _pack_bf16_pair(rows[:, 0:PACK_COLS], rows[:, PACK_COLS:]))
        yield


def _mixer_kernel(x_ref, xp_ref, g1_ref, win_ref, rc_ref, rs1_ref, rs2_ref, sink_ref, lng_ref, lnb_ref,
                  wcat_ref, bsf_ref, wab_ref, wgb_ref, wout_ref, g2_ref, wrt_ref, brt_ref, tri_ref, ltri_ref,
                  x1_ref, xs_ref, rw_ref, cnt_ref,
                  kl_ref, vt_ref, attnt_ref, gated_ref, mix_ref, *, blocks_per_seq, n_blocks):
    step = pl.program_id(0)
    s_blk = jnp.minimum(step, n_blocks - 1) % blocks_per_seq
    cur = step % 2
    ts = x_ref.shape[0]
    nb = ts // BLOCK
    f32, bf16 = jnp.float32, jnp.bfloat16

    @pl.when(step == 0)
    def _():
        mix_ref[1] = jnp.zeros((ts, D_MODEL), bf16)

    @pl.when(s_blk == 0)
    def _():
        kl_ref[:, 0:BLOCK, :] = jnp.zeros((4, BLOCK, LANES), bf16)
        vt_ref[:, 0:BLOCK] = jnp.zeros((LANES, BLOCK), bf16)

    x1 = xp_ref[...] + jnp.dot(mix_ref[1 - cur], wout_ref[...], preferred_element_type=f32)
    x1_ref[...] = x1
    route = _route_and_compact(_rms(x1, g2_ref[...]).astype(bf16), wrt_ref, brt_ref, tri_ref, ltri_ref,
                               xs_ref, rw_ref, cnt_ref)
    step_route = lambda: next(route, None)

    x = x_ref[...]
    hb = _rms(x, g1_ref[...]).astype(bf16)

    qkv = jnp.dot(hb, win_ref[:, 0:V_END], preferred_element_type=f32)
    rc, rs1, rs2 = rc_ref[...], rs1_ref[...], rs2_ref[...]

    def rope(t):
        return t * rc + pltpu.roll(t, 8, 1) * rs1 + pltpu.roll(t, LANES - 8, 1) * rs2

    scale = 1.0 / math.sqrt(HEAD_DIM)
    qb = [(rope(qkv[:, j * LANES:(j + 1) * LANES]) * scale).astype(bf16) for j in range(4)]
    k = rope(qkv[:, Q_END:K_END])
    v = qkv[:, K_END:V_END]
    k_sw = pltpu.roll(k, HEAD_DIM, 1)
    lo = lax.broadcasted_iota(jnp.int32, (ts, LANES), 1) < HEAD_DIM
    zero = jnp.zeros_like(k)
    kl_ref[0, BLOCK:BLOCK + ts, :] = jnp.where(lo, k, zero).astype(bf16)
    kl_ref[1, BLOCK:BLOCK + ts, :] = jnp.where(lo, zero, k_sw).astype(bf16)
    kl_ref[2, BLOCK:BLOCK + ts, :] = jnp.where(lo, k_sw, zero).astype(bf16)
    kl_ref[3, BLOCK:BLOCK + ts, :] = jnp.where(lo, zero, k).astype(bf16)
    vt_ref[:, BLOCK:BLOCK + ts] = v.T.astype(bf16)

    kj = lax.broadcasted_iota(jnp.int32, (2 * BLOCK, 2 * BLOCK), 0)
    qi = lax.broadcasted_iota(jnp.int32, (2 * BLOCK, 2 * BLOCK), 1) % BLOCK
    band = (kj > qi) & (kj <= qi + BLOCK)
    kmin = jnp.where(s_blk == 0, BLOCK, 0)
    bias_first = jnp.where(band & (kj >= kmin), 0.0, NEG_INF).astype(f32)
    bias_rest = jnp.where(band, 0.0, NEG_INF).astype(f32)
    items = [(i, g, p) for i in range(nb) for g in range(N_KV_HEADS) for p in range(2)]

    def scores(n):
        i, g, p = items[n]
        rows = slice(i * BLOCK, (i + 1) * BLOCK)
        qs = jnp.concatenate([qb[2 * g][rows], qb[2 * g + 1][rows]], axis=0)
        st = lax.dot_general(kl_ref[2 * g + p, i * BLOCK:(i + 2) * BLOCK, :], qs, (((1,), (1,)), ((), ())),
                             preferred_element_type=f32)
        return st + (bias_first if i == 0 else bias_rest)

    def finish(n, st):
        i, g, p = items[n]
        rows = slice(i * BLOCK, (i + 1) * BLOCK)
        sink = sink_ref[2 * g + p:2 * g + p + 1, :]
        m = jnp.maximum(jnp.max(st, axis=0, keepdims=True), sink)
        e = jnp.exp(st - m)
        den = jnp.sum(e, axis=0, keepdims=True) + jnp.exp(sink - m)
        ot = jnp.dot(vt_ref[g * HEAD_DIM:(g + 1) * HEAD_DIM, i * BLOCK:(i + 2) * BLOCK], e.astype(bf16),
                     preferred_element_type=f32) * (1.0 / den)
        h0 = 4 * g + p
        attnt_ref[h0 * HEAD_DIM:(h0 + 1) * HEAD_DIM, rows] = ot[:, 0:BLOCK]
        attnt_ref[(h0 + 2) * HEAD_DIM:(h0 + 3) * HEAD_DIM, rows] = ot[:, BLOCK:2 * BLOCK]

    n_chunks = D_MODEL // CHUNK_COLS
    zc, sga, sgg, gmc = [None] * n_chunks, [None] * n_chunks, [None] * n_chunks, [None] * n_chunks
    vn_box = []

    def proj(lo_col, c):
        cols = slice(lo_col + c * CHUNK_COLS, lo_col + (c + 1) * CHUNK_COLS)
        return jnp.dot(hb, win_ref[:, cols], preferred_element_type=f32)

    def z_chunk(c):
        zc[c] = jax.nn.gelu(proj(V_END, c))
        if c == n_chunks - 1:
            v2 = jnp.concatenate(zc[n_chunks // 2:], axis=1)
            mu = jnp.mean(v2, axis=-1, keepdims=True)
            vc = v2 - mu
            var = jnp.mean(vc * vc, axis=-1, keepdims=True)
            vn_box.append((vc * lax.rsqrt(var + EPS) * lng_ref[...] + lnb_ref[...]).astype(bf16))

    def ga_chunk(c):
        sga[c] = jax.nn.sigmoid(proj(Z_END, c))

    def gg_chunk(c):
        sgg[c] = jax.nn.sigmoid(proj(GA_END, c))

    wq = lax.broadcasted_iota(jnp.int32, (BLOCK, 2 * BLOCK), 0)
    wp = lax.broadcasted_iota(jnp.int32, (BLOCK, 2 * BLOCK), 1) % BLOCK
    causal = wp <= wq
    lo_c = lax.broadcasted_iota(jnp.int32, (BLOCK, LANES), 1) < G_GROUP_DIM
    wjs = {}

    def gmlp_dot(j, c):
        if j not in wjs:
            wjs[j] = jnp.where(causal, wcat_ref[j], 0.0).astype(bf16)
        cols = slice(j * LANES, (j + 1) * LANES)
        rhs = []
        for cc in (c, c + 1):
            vp = vn_box[0][cc * BLOCK:(cc + 1) * BLOCK, cols]
            zb = jnp.zeros_like(vp)
            rhs.append(jnp.concatenate([jnp.where(lo_c, vp, zb), jnp.where(lo_c, zb, vp)], axis=0))
        mixed = jnp.dot(wjs[j], jnp.concatenate(rhs, axis=1), preferred_element_type=f32)
        u = zc[j // 2][:, (j % 2) * LANES:(j % 2 + 1) * LANES]
        for q, cc in enumerate((c, c + 1)):
            rows = slice(cc * BLOCK, (cc + 1) * BLOCK)
            gated_ref[rows, cols] = u[rows] * (mixed[:, q * LANES:(q + 1) * LANES] + bsf_ref[:, cols])

    def gm_chunk(c):
        cols = slice(c * CHUNK_COLS, (c + 1) * CHUNK_COLS)
        gmc[c] = jnp.dot(gated_ref[...].astype(bf16), wgb_ref[:, cols], preferred_element_type=f32)

    fillers = [[functools.partial(z_chunk, c)] for c in range(n_chunks)]
    fillers += [[functools.partial(ga_chunk, c)] for c in range(n_chunks)]
    fillers += [[functools.partial(gg_chunk, c)] for c in range(n_chunks)]
    fillers += [[functools.partial(gm_chunk, c)] for c in range(n_chunks)]
    gd = [functools.partial(gmlp_dot, j, c) for j in range(G_GROUPS // 2) for c in range(0, nb, 2)]
    for q in range(len(gd)):
        fillers[n_chunks + q].append(gd[q])
    assert len(fillers) == len(items)

    fillers[0][0]()
    step_route()
    fillers[1][0]()
    step_route()
    step_route()
    fillers = fillers[2:] + [[], []]
    for n in ROUTE_AT:
        fillers[n].append(step_route)
    sts = {n: scores(n) for n in range(LOOKAHEAD)}
    for n in range(len(items)):
        finish(n, sts.pop(n))
        if n + LOOKAHEAD < len(items):
            sts[n + LOOKAHEAD] = scores(n + LOOKAHEAD)
        for f in fillers[n]:
            f()
    kl_ref[:, 0:BLOCK, :] = kl_ref[:, ts:ts + BLOCK, :]
    vt_ref[:, 0:BLOCK] = vt_ref[:, ts:ts + BLOCK]
    attn_b = attnt_ref[...].T.astype(bf16)

    mix = []
    for c in range(n_chunks):
        cols = slice(c * CHUNK_COLS, (c + 1) * CHUNK_COLS)
        a_c = jnp.dot(attn_b, wab_ref[:, cols], preferred_element_type=f32)
        mix.append((sga[c] * a_c + sgg[c] * gmc[c]).astype(bf16))
    mix_ref[cur] = jnp.concatenate(mix, axis=1)
    for _ in route:
        pass


def _const_spec(shape):
    nd = len(shape)
    return pl.BlockSpec(shape, lambda i: (0,) * nd)


def _mixer(x2, g1, win, rc, rs1, rs2, sinks, lng, lnb, wcat, bsf, wab, wgb, wout, g2, wrt, brt, tri, ltri, seq):
    t = x2.shape[0]
    n_blocks = t // TS
    blocks_per_seq = seq // TS
    cur_blk = lambda i: (jnp.minimum(i, n_blocks - 1), 0)
    prev_blk = lambda i: (jnp.maximum(i - 1, 0), 0)
    pos = lambda i: (jnp.minimum(i, n_blocks - 1) % blocks_per_seq, 0)
    consts = (sinks, lng, lnb, wcat, bsf, wab, wgb, wout, g2, wrt, brt, tri, ltri)
    in_specs = [
        pl.BlockSpec((TS, D_MODEL), cur_blk),
        pl.BlockSpec((TS, D_MODEL), prev_blk),
        _const_spec(g1.shape), _const_spec(win.shape),
        pl.BlockSpec((TS, LANES), pos), pl.BlockSpec((TS, LANES), pos), pl.BlockSpec((TS, LANES), pos),
    ] + [_const_spec(c.shape) for c in consts]
    out_shape = (
        jax.ShapeDtypeStruct((t, D_MODEL), jnp.float32),
        jax.ShapeDtypeStruct((n_blocks * BLOCK_ROWS * ROW_LINES, LANES), jnp.uint32),
        jax.ShapeDtypeStruct((t, LANES), jnp.float32),
        jax.ShapeDtypeStruct((n_blocks, N_EXPERTS, LANES), jnp.float32),
    )
    out_specs = (
        pl.BlockSpec((TS, D_MODEL), prev_blk),
        pl.BlockSpec((BLOCK_ROWS * ROW_LINES, LANES), prev_blk),
        pl.BlockSpec((TS, LANES), prev_blk),
        pl.BlockSpec((1, N_EXPERTS, LANES), lambda i: (jnp.maximum(i - 1, 0), 0, 0)),
    )
    scratch = [
        pltpu.VMEM((4, BLOCK + TS, LANES), jnp.bfloat16),
        pltpu.VMEM((LANES, BLOCK + TS), jnp.bfloat16),
        pltpu.VMEM((ATTN_WIDTH, TS), jnp.float32),
        pltpu.VMEM((TS, G_WIDTH), jnp.float32),
        pltpu.VMEM((2, TS, D_MODEL), jnp.bfloat16),
    ]
    return pl.pallas_call(
        functools.partial(_mixer_kernel, blocks_per_seq=blocks_per_seq, n_blocks=n_blocks),
        grid=(n_blocks + 1,),
        in_specs=in_specs,
        out_specs=out_specs,
        out_shape=out_shape,
        scratch_shapes=scratch,
        compiler_params=pltpu.CompilerParams(dimension_semantics=("arbitrary",),
                                             vmem_limit_bytes=VMEM_LIMIT),
        name="mixer",
    )(x2, x2, g1, win, rc, rs1, rs2, *consts)


def _expert_kernel(te_ref, tr0_ref, tb0_ref, tb1_ref, trows_ref, nt_ref, cum_ref, src_ref, used_ref,
                   xs_ref, wg_ref, wu_ref, wd_ref, ys_ref,
                   xbuf_ref, ybuf_ref, wgb_ref, wub_ref, wdb_ref, zero_ref, isem, osem, zsem,
                   *, n_blocks):
    j = pl.program_id(0)
    n_tiles = nt_ref[0]
    slot = j % 2
    f32, bf16 = jnp.float32, jnp.bfloat16

    def lines(rows):
        return pl.multiple_of(rows * ROW_LINES, SUBLANES)

    def run_piece(tile, b, ok, fn):
        e, r0 = te_ref[tile], tr0_ref[tile]
        bb = jnp.minimum(b, n_blocks - 1)
        start = cum_ref[e * (n_blocks + 1) + bb]
        lo = jnp.maximum(start, r0)
        n = jnp.minimum(cum_ref[e * (n_blocks + 1) + bb + 1], r0 + TM) - lo

        @pl.when(ok & (b < tb1_ref[tile]) & (n > 0))
        def _():
            fn(lines(src_ref[e * n_blocks + bb] + (lo - start)), lines(lo - r0), lines(n))

    def first_runs(tile, ok, fn):
        for k in range(FAST_RUNS):
            run_piece(tile, tb0_ref[tile] + k, ok, fn)

    def other_runs(tile, first, fn):
        lax.fori_loop(tb0_ref[tile] + first, tb1_ref[tile], lambda b, c: (run_piece(tile, b, True, fn), c)[1], 0)

    def gather_fn(s):
        return lambda src, dst, n: pltpu.make_async_copy(
            xs_ref.at[pl.ds(src, n)], xbuf_ref.at[s, pl.ds(dst, n)], isem.at[s]).start()

    def scatter_fn(s):
        return lambda src, dst, n: pltpu.make_async_copy(
            ybuf_ref.at[s, pl.ds(dst, n)], ys_ref.at[pl.ds(src, n)], osem.at[s]).start()

    def wait_rows(sem, n):
        @pl.when(n > 0)
        def _():
            pltpu.make_async_copy(xs_ref.at[pl.ds(0, lines(n))], xbuf_ref.at[0, pl.ds(0, lines(n))], sem).wait()

    def zero_tail_copy(b, c):
        first = used_ref[b] + c * TM
        n = lines(jnp.minimum(BLOCK_ROWS - first, TM))
        return pltpu.make_async_copy(zero_ref.at[pl.ds(0, n)], ys_ref.at[pl.ds(lines(b * BLOCK_ROWS + first), n)], zsem)

    @pl.when(j == 0)
    def _():
        xbuf_ref[...] = jnp.zeros_like(xbuf_ref)
        zero_ref[...] = jnp.zeros_like(zero_ref)

        def per_block(b, carry):
            chunks = (BLOCK_ROWS - used_ref[b] + TM - 1) // TM
            lax.fori_loop(0, chunks, lambda c, x: (zero_tail_copy(b, c).start(), x)[1], 0)
            lax.fori_loop(0, chunks, lambda c, x: (zero_tail_copy(b, c).wait(), x)[1], 0)
            return carry

        lax.fori_loop(0, n_blocks, per_block, 0)

        for t0 in range(GATHER_AHEAD):
            @pl.when(t0 < n_tiles)
            def _():
                other_runs(t0, 0, gather_fn(t0))

    prev = jnp.maximum(j - 1, 0)
    xslot = j % (GATHER_AHEAD + 1)
    nslot = (j + GATHER_AHEAD) % (GATHER_AHEAD + 1)

    @pl.when((j == n_tiles) & (j > 0))
    def _():
        other_runs(prev, 0, scatter_fn(1 - slot))

    @pl.when((j >= n_tiles) & (j >= 2) & (j - 2 < n_tiles))
    def _():
        wait_rows(osem.at[slot], trows_ref[jnp.maximum(j - 2, 0)])

    @pl.when(j < n_tiles)
    def _():
        @pl.when((j == 0) | (te_ref[j] != te_ref[prev]))
        def _():
            wgb_ref[...] = wg_ref[0].astype(bf16)
            wub_ref[...] = wu_ref[0].astype(bf16)
            wdb_ref[...] = wd_ref[0].astype(bf16)

        wait_rows(isem.at[xslot], trows_ref[j])
        xb = _unpack_bf16_rows(_load_rows(xbuf_ref.at[xslot], 0, TM))
        nxt = jnp.minimum(j + GATHER_AHEAD, n_tiles - 1)
        more = j + GATHER_AHEAD < n_tiles
        first_runs(prev, j > 0, scatter_fn(1 - slot))
        first_runs(nxt, more, gather_fn(nslot))
        half = D_EXPERT // 2
        g0 = jnp.dot(xb, wgb_ref[:, 0:half], preferred_element_type=f32)
        u0 = jnp.dot(xb, wub_ref[:, 0:half], preferred_element_type=f32)
        g1 = jnp.dot(xb, wgb_ref[:, half:], preferred_element_type=f32)
        u1 = jnp.dot(xb, wub_ref[:, half:], preferred_element_type=f32)
        y = jnp.dot((jax.nn.silu(g0) * u0).astype(bf16), wdb_ref[0:half, :], preferred_element_type=f32)
        y = y + jnp.dot((jax.nn.silu(g1) * u1).astype(bf16), wdb_ref[half:, :], preferred_element_type=f32)
        packed = _pack_bf16_pair(y[:, 0:PACK_COLS], y[:, PACK_COLS:])
        probe = xbuf_ref[xslot, pl.ds(0, SUBLANES, stride=ROW_LINES), :]

        @pl.when(j >= 2)
        def _():
            wait_rows(osem.at[slot], trows_ref[jnp.maximum(j - 2, 0)])

        _store_rows(ybuf_ref.at[slot], 0, packed)
        ybuf_ref[slot, pl.ds(0, SUBLANES, stride=ROW_LINES), :] = packed[0:SUBLANES, 0:LANES] | ((probe >> 16) >> 16)

        @pl.when(more)
        def _():
            other_runs(nxt, FAST_RUNS, gather_fn(nslot))

        @pl.when(j > 0)
        def _():
            other_runs(prev, FAST_RUNS, scatter_fn(1 - slot))


def _experts(tables, xs, wg, wu, wd, n_blocks, grid_tiles):
    wsel = lambda j, te, *_: (te[j], 0, 0)
    grid_spec = pltpu.PrefetchScalarGridSpec(
        num_scalar_prefetch=len(tables),
        grid=(grid_tiles,),
        in_specs=[
            pl.BlockSpec(memory_space=pl.ANY),
            pl.BlockSpec((1, D_MODEL, D_EXPERT), wsel),
            pl.BlockSpec((1, D_MODEL, D_EXPERT), wsel),
            pl.BlockSpec((1, D_EXPERT, D_MODEL), wsel),
        ],
        out_specs=pl.BlockSpec(memory_space=pl.ANY),
        scratch_shapes=[
            pltpu.VMEM((GATHER_AHEAD + 1, TM * ROW_LINES, LANES), jnp.uint32),
            pltpu.VMEM((2, TM * ROW_LINES, LANES), jnp.uint32),
            pltpu.VMEM((D_MODEL, D_EXPERT), jnp.bfloat16),
            pltpu.VMEM((D_MODEL, D_EXPERT), jnp.bfloat16),
            pltpu.VMEM((D_EXPERT, D_MODEL), jnp.bfloat16),
            pltpu.VMEM((TM * ROW_LINES, LANES), jnp.uint32),
            pltpu.SemaphoreType.DMA((GATHER_AHEAD + 1,)),
            pltpu.SemaphoreType.DMA((2,)),
            pltpu.SemaphoreType.DMA,
        ],
    )
    return pl.pallas_call(
        functools.partial(_expert_kernel, n_blocks=n_blocks),
        grid_spec=grid_spec,
        out_shape=jax.ShapeDtypeStruct(xs.shape, jnp.uint32),
        compiler_params=pltpu.CompilerParams(dimension_semantics=("arbitrary",),
                                             vmem_limit_bytes=VMEM_LIMIT),
        name="experts",
    )(*tables, xs, wg, wu, wd)


def _combine_kernel(x1_ref, rw_ref, gf_ref, ys_ref, o_ref):
    ts = x1_ref.shape[0]
    rw = rw_ref[...]
    io = lax.broadcasted_iota(jnp.int32, (ts, BLOCK_ROWS), 1).astype(jnp.float32)
    wsel = jnp.where(io == rw[:, 2:3], rw[:, 0:1], 0.0) + jnp.where(io == rw[:, 3:4], rw[:, 1:2], 0.0)
    ys = _unpack_bf16_rows(_load_rows(ys_ref, 0, BLOCK_ROWS))
    y = jnp.dot(wsel.astype(jnp.bfloat16), ys, preferred_element_type=jnp.float32)
    o_ref[...] = _rms(x1_ref[...] + y, gf_ref[...])


def _combine(x1, rw, gf, ys):
    t = x1.shape[0]
    tok = lambda i: (i, 0)
    return pl.pallas_call(
        _combine_kernel,
        grid=(t // TS,),
        in_specs=[
            pl.BlockSpec((TS, D_MODEL), tok),
            pl.BlockSpec((TS, LANES), tok),
            pl.BlockSpec((1, D_MODEL), lambda i: (0, 0)),
            pl.BlockSpec((BLOCK_ROWS * ROW_LINES, LANES), tok),
        ],
        out_specs=pl.BlockSpec((TS, D_MODEL), tok),
        out_shape=jax.ShapeDtypeStruct((t, D_MODEL), jnp.float32),
        compiler_params=pltpu.CompilerParams(dimension_semantics=("arbitrary",),
                                             vmem_limit_bytes=VMEM_LIMIT),
        name="combine",
    )(x1, rw, gf, ys)


def _rope_lane_tables(seq):
    inv_freq = ROPE_THETA ** (-jnp.arange(0, ROT_DIM, 2, dtype=jnp.float32) / ROT_DIM)
    ang = jnp.arange(seq, dtype=jnp.float32)[:, None] * inv_freq[None, :]
    cos, sin = jnp.cos(ang), jnp.sin(ang)
    half = ROT_DIM // 2
    ones = jnp.ones((seq, HEAD_DIM - ROT_DIM), jnp.float32)
    zeros = jnp.zeros((seq, HEAD_DIM - ROT_DIM), jnp.float32)
    zh = jnp.zeros((seq, half), jnp.float32)
    c = jnp.concatenate([cos, cos, ones], axis=1)
    s1 = jnp.concatenate([zh, sin, zeros], axis=1)
    s2 = jnp.concatenate([-sin, zh, zeros], axis=1)
    rep = LANES // HEAD_DIM
    return jnp.tile(c, (1, rep)), jnp.tile(s1, (1, rep)), jnp.tile(s2, (1, rep))


def _pick(table, idx):
    return jnp.sum(jnp.where(idx[:, None] == jnp.arange(table.shape[0])[None, :], table[None, :], 0), axis=1)


def _tile_tables(cnt, n_blocks, grid_tiles):
    i32 = jnp.int32
    n = cnt[:, :, 0].astype(i32)
    n8 = (n + ROW_ALIGN - 1) // ROW_ALIGN * ROW_ALIGN
    loc = jnp.cumsum(n8, axis=1) - n8
    used = jnp.sum(n8, axis=1)
    cum = jnp.concatenate([jnp.zeros((1, N_EXPERTS), i32), jnp.cumsum(n8, axis=0)], axis=0).T
    total = cum[:, -1]
    src = (jnp.arange(n_blocks, dtype=i32)[:, None] * BLOCK_ROWS + loc).T
    tiles_e = (total + TM - 1) // TM
    tile_end = jnp.cumsum(tiles_e)
    n_tiles = tile_end[-1]
    j = jnp.arange(grid_tiles, dtype=i32)
    te = jnp.minimum(jnp.sum(tile_end[None, :] <= j[:, None], axis=1), N_EXPERTS - 1).astype(i32)
    live = j < n_tiles
    te = jnp.where(live, te, _pick(te, jnp.maximum(n_tiles - 1, 0)[None])[0])
    r0 = (j - _pick(tile_end - tiles_e, te)) * TM
    cum_t = cum[te]
    b0 = jnp.sum(cum_t[:, 1:] <= r0[:, None], axis=1)
    b1 = jnp.sum(cum_t[:, :-1] < (r0 + TM)[:, None], axis=1)
    rows = jnp.clip(_pick(total, te) - r0, 0, TM)
    zero = jnp.zeros_like(j)
    return (te, jnp.where(live, r0, zero), jnp.where(live, b0, zero).astype(i32),
            jnp.where(live, b1, zero).astype(i32), jnp.where(live, rows, zero).astype(i32),
            n_tiles.reshape(1).astype(i32), cum.reshape(-1), src.reshape(-1).astype(i32), used.astype(i32))


def kernel(x, norm1_g, w_in, attn_sinks, gmlp_ln_g, gmlp_ln_b, gmlp_ws, gmlp_bs, w_attn_branch,
           w_gmlp_branch, w_out, norm2_g, router_group_w, router_group_b, router_expert_w,
           router_expert_b, expert_w_gate, expert_w_up, expert_w_down, final_norm_g):
    b, s, d = x.shape
    assert d == D_MODEL and s % TS == 0 and norm1_g.shape[0] == 1
    t = b * s
    n_blocks = t // TS
    bf16, f32 = jnp.bfloat16, jnp.float32
    x2 = x.reshape(t, d)

    rc, rs1, rs2 = _rope_lane_tables(s)
    sk = attn_sinks[0].astype(f32)
    sinks = jnp.stack([
        jnp.concatenate([jnp.full((BLOCK,), 1.0, f32) * sk[4 * g + p], jnp.full((BLOCK,), 1.0, f32) * sk[4 * g + 2 + p]])
        for g in range(N_KV_HEADS) for p in range(2)])
    ws = gmlp_ws[0]
    wcat = jnp.stack([jnp.concatenate([ws[2 * j], ws[2 * j + 1]], axis=1) for j in range(G_GROUPS // 2)])
    bsf = jnp.repeat(gmlp_bs[0].T, G_GROUP_DIM, axis=1)
    wrt = jnp.concatenate([router_group_w[0].T, jnp.zeros((8 - N_GROUPS, d), f32), router_expert_w[0].T], axis=0)
    brt = jnp.concatenate([router_group_b[0], jnp.zeros((8 - N_GROUPS,), f32), router_expert_b[0]])
    brt = jnp.broadcast_to(brt[:, None], (ROUTER_ROWS, TS))
    tri = (jnp.arange(TS)[:, None] < jnp.arange(TS)[None, :]).astype(bf16)
    ltri = (jnp.arange(LANES)[None, :] < jnp.arange(N_EXPERTS)[:, None]).astype(bf16)

    x1, xs, rw, cnt = _mixer(
        x2, norm1_g, w_in[0].astype(bf16), rc, rs1, rs2, sinks, gmlp_ln_g, gmlp_ln_b, wcat, bsf,
        w_attn_branch[0].astype(bf16), w_gmlp_branch[0].astype(bf16), w_out[0].astype(bf16),
        norm2_g, wrt.astype(bf16), brt, tri, ltri, s)

    max_tiles = (TOP_K * t + n_blocks * N_EXPERTS * (ROW_ALIGN - 1)) // TM + N_EXPERTS
    grid_tiles = max_tiles + 2
    tables = _tile_tables(cnt, n_blocks, grid_tiles)
    ys = _experts(tables, xs, expert_w_gate[0], expert_w_up[0], expert_w_down[0], n_blocks, grid_tiles)
    out = _combine(x1, rw, final_norm_g.reshape(1, d), ys)
    return out.reshape(b, s, d)
```

```python
import functools
import math

import jax
import jax.numpy as jnp
from jax import lax
from jax.experimental import pallas as pl
from jax.experimental.pallas import tpu as pltpu

D_MODEL = 1024
HEAD_DIM = 64
N_HEADS = 8
N_KV_HEADS = 2
BLOCK = 128
ROT_DIM = HEAD_DIM // 4
ROPE_THETA = 500000.0
ATTN_WIDTH = N_HEADS * HEAD_DIM
KV_WIDTH = N_KV_HEADS * HEAD_DIM
G_GROUPS = 8
G_GROUP_DIM = 64
G_WIDTH = G_GROUPS * G_GROUP_DIM
Q_END = ATTN_WIDTH
K_END = Q_END + KV_WIDTH
V_END = K_END + KV_WIDTH
Z_END = V_END + 2 * G_WIDTH
GA_END = Z_END + D_MODEL
IN_COLS = GA_END + D_MODEL
N_GROUPS = 4
EXPERTS_PER_GROUP = 8
N_EXPERTS = N_GROUPS * EXPERTS_PER_GROUP
TOP_K = 2
D_EXPERT = 512
EPS = 1e-5
NEG_INF = -1e30

LANES = 128
SUBLANES = 8
ROUTER_ROWS = 8 + N_EXPERTS
PACK_COLS = D_MODEL // 2
ROW_LINES = PACK_COLS // LANES
ROW_ALIGN = SUBLANES // ROW_LINES

TS = 512
TM = 512
GATHER_AHEAD = 2
FAST_RUNS = 36
GROUP = 256
CHUNK_COLS = 256
LOOKAHEAD = 3
ROUTE_AT = (0, 1)
GROUPS_PER_STEP = TS // GROUP
BLOCK_ROWS = TOP_K * GROUP + N_EXPERTS * (ROW_ALIGN - 1)
VMEM_LIMIT = 58 * 1024 * 1024


def _rms(x, g):
    return x * lax.rsqrt(jnp.mean(x * x, axis=-1, keepdims=True) + EPS) * g


def _pack_bf16_pair(a, b):
    ua = lax.bitcast_convert_type(a.astype(jnp.bfloat16).astype(jnp.float32), jnp.uint32)
    ub = lax.bitcast_convert_type(b.astype(jnp.bfloat16).astype(jnp.float32), jnp.uint32)
    return ub | (ua >> 16)


def _unpack_bf16_rows(w):
    lo = lax.bitcast_convert_type(w << 16, jnp.float32)
    hi = lax.bitcast_convert_type(w & jnp.uint32(0xFFFF0000), jnp.float32)
    return jnp.concatenate([lo.astype(jnp.bfloat16), hi.astype(jnp.bfloat16)], axis=1)


def _load_rows(lines_ref, first_row, rows):
    return jnp.concatenate([lines_ref[pl.ds(first_row * ROW_LINES + c, rows, stride=ROW_LINES), :]
                            for c in range(ROW_LINES)], axis=1)


def _store_rows(lines_ref, first_row, packed):
    for c in range(ROW_LINES):
        lines_ref[pl.ds(first_row * ROW_LINES + c, packed.shape[0], stride=ROW_LINES), :] = \
            packed[:, c * LANES:(c + 1) * LANES]


def _route_and_compact(xnb, wrt_ref, brt_ref, tri_ref, ltri_ref, xs_ref, rw_ref, cnt_ref):
    ts = xnb.shape[0]
    f32, bf16 = jnp.float32, jnp.bfloat16
    lt = lax.dot_general(wrt_ref[...], xnb, (((1,), (1,)), ((), ())),
                         preferred_element_type=f32) + brt_ref[...]
    yield
    r = [lt[i:i + 1, :] for i in range(N_GROUPS)]
    gm_ = jnp.maximum(jnp.maximum(r[0], r[1]), jnp.maximum(r[2], r[3]))
    gsel = jnp.where(r[0] == gm_, 0.0, jnp.where(r[1] == gm_, 1.0, jnp.where(r[2] == gm_, 2.0, 3.0)))
    g_w = 1.0 / (jnp.exp(r[0] - gm_) + jnp.exp(r[1] - gm_) + jnp.exp(r[2] - gm_) + jnp.exp(r[3] - gm_))
    eg = [lt[8 + 8 * kk:16 + 8 * kk, :] for kk in range(N_GROUPS)]
    ein = jnp.where(gsel == 0.0, eg[0], jnp.where(gsel == 1.0, eg[1], jnp.where(gsel == 2.0, eg[2], eg[3])))
    io8 = lax.broadcasted_iota(jnp.int32, (EXPERTS_PER_GROUP, ts), 0).astype(f32)
    m1 = jnp.max(ein, axis=0, keepdims=True)
    i1 = jnp.min(jnp.where(ein == m1, io8, 8.0), axis=0, keepdims=True)
    e2 = jnp.where(io8 == i1, NEG_INF, ein)
    m2 = jnp.max(e2, axis=0, keepdims=True)
    i2 = jnp.min(jnp.where(e2 == m2, io8, 8.0), axis=0, keepdims=True)
    t2 = jnp.exp(m2 - m1)
    w1 = 1.0 / (1.0 + t2)
    wt0, wt1 = g_w * w1, g_w * (t2 * w1)
    eid0, eid1 = gsel * 8.0 + i1, gsel * 8.0 + i2

    io32 = lax.broadcasted_iota(jnp.int32, (N_EXPERTS, ts), 0).astype(f32)
    oh0, oh1 = io32 == eid0, io32 == eid1
    both = jnp.where(oh0, 1.0, jnp.where(oh1, 1.0, 0.0))
    pref = jnp.dot(both.astype(bf16), tri_ref[...], preferred_element_type=f32)
    yield
    gid = lax.broadcasted_iota(jnp.int32, (N_EXPERTS, ts), 1) // GROUP
    units = jnp.zeros((N_EXPERTS, ts), f32)
    for gi in range(GROUPS_PER_STEP):
        n_e = jnp.sum(jnp.where(gid == gi, both, 0.0), axis=1, keepdims=True)
        cnt_ref[gi] = jnp.broadcast_to(n_e, (N_EXPERTS, LANES))
        units = jnp.where(gid == gi, jnp.floor((n_e + (ROW_ALIGN - 1.0)) * (1.0 / ROW_ALIGN)), units)
    units = jnp.concatenate([units, jnp.zeros((LANES - N_EXPERTS, ts), f32)], axis=0).astype(bf16)
    tot = ROW_ALIGN * jnp.dot(ltri_ref[...], units, preferred_element_type=f32) + pref
    yield
    slot0 = jnp.sum(jnp.where(oh0, tot, 0.0), axis=0, keepdims=True)
    slot1 = jnp.sum(jnp.where(oh1, tot, 0.0), axis=0, keepdims=True)

    io128 = lax.broadcasted_iota(jnp.int32, (LANES, ts), 0)
    rw_ref[...] = jnp.where(io128 == 0, wt0, jnp.where(io128 == 1, wt1,
                            jnp.where(io128 == 2, slot0, jnp.where(io128 == 3, slot1, 0.0)))).T

    s0i, s1i = slot0.astype(jnp.int32), slot1.astype(jnp.int32)
    io = lax.broadcasted_iota(jnp.int32, (BLOCK_ROWS, GROUP), 0)
    for gi in range(GROUPS_PER_STEP):
        tok = slice(gi * GROUP, (gi + 1) * GROUP)
        onehot = jnp.where(io == s0i[:, tok], 1.0, jnp.where(io == s1i[:, tok], 1.0, 0.0)).astype(bf16)
        rows = jnp.dot(onehot, xnb[tok, :], preferred_element_type=f32)
        _store_rows(xs_ref, gi * BLOCK_ROWS, _pack_bf16_pair(rows[:, 0:PACK_COLS], rows[:, PACK_COLS:]))
        yield


def _mixer_kernel(x_ref, xp_ref, g1_ref, win_ref, rc_ref, rs1_ref, rs2_ref, sink_ref, lng_ref, lnb_ref,
                  wcat_ref, bsf_ref, wab_ref, wgb_ref, wout_ref, g2_ref, wrt_ref, brt_ref, tri_ref, ltri_ref,
                  x1_ref, xs_ref, rw_ref, cnt_ref,
                  kl_ref, vt_ref, attnt_ref, gated_ref, mix_ref, *, blocks_per_seq, n_blocks):
    step = pl.program_id(0)
    s_blk = jnp.minimum(step, n_blocks - 1) % blocks_per_seq
    cur = step % 2
    ts = x_ref.shape[0]
    nb = ts // BLOCK
    f32, bf16 = jnp.float32, jnp.bfloat16

    @pl.when(step == 0)
    def _():
        mix_ref[1] = jnp.zeros((ts, D_MODEL), bf16)

    @pl.when(s_blk == 0)
    def _():
        kl_ref[:, 0:BLOCK, :] = jnp.zeros((4, BLOCK, LANES), bf16)
        vt_ref[:, 0:BLOCK] = jnp.zeros((LANES, BLOCK), bf16)

    x1 = xp_ref[...] + jnp.dot(mix_ref[1 - cur], wout_ref[...], preferred_element_type=f32)
    x1_ref[...] = x1
    route = _route_and_compact(_rms(x1, g2_ref[...]).astype(bf16), wrt_ref, brt_ref, tri_ref, ltri_ref,
                               xs_ref, rw_ref, cnt_ref)
    step_route = lambda: next(route, None)

    x = x_ref[...]
    hb = _rms(x, g1_ref[...]).astype(bf16)

    qkv = jnp.dot(hb, win_ref[:, 0:V_END], preferred_element_type=f32)
    rc, rs1, rs2 = rc_ref[...], rs1_ref[...], rs2_ref[...]

    def rope(t):
        return t * rc + pltpu.roll(t, 8, 1) * rs1 + pltpu.roll(t, LANES - 8, 1) * rs2

    scale = 1.0 / math.sqrt(HEAD_DIM)
    qb = [(rope(qkv[:, j * LANES:(j + 1) * LANES]) * scale).astype(bf16) for j in range(4)]
    k = rope(qkv[:, Q_END:K_END])
    v = qkv[:, K_END:V_END]
    k_sw = pltpu.roll(k, HEAD_DIM, 1)
    lo = lax.broadcasted_iota(jnp.int32, (ts, LANES), 1) < HEAD_DIM
    zero = jnp.zeros_like(k)
    kl_ref[0, BLOCK:BLOCK + ts, :] = jnp.where(lo, k, zero).astype(bf16)
    kl_ref[1, BLOCK:BLOCK + ts, :] = jnp.where(lo, zero, k_sw).astype(bf16)
    kl_ref[2, BLOCK:BLOCK + ts, :] = jnp.where(lo, k_sw, zero).astype(bf16)
    kl_ref[3, BLOCK:BLOCK + ts, :] = jnp.where(lo, zero, k).astype(bf16)
    vt_ref[:, BLOCK:BLOCK + ts] = v.T.astype(bf16)

    kj = lax.broadcasted_iota(jnp.int32, (2 * BLOCK, 2 * BLOCK), 0)
    qi = lax.broadcasted_iota(jnp.int32, (2 * BLOCK, 2 * BLOCK), 1) % BLOCK
    band = (kj > qi) & (kj <= qi + BLOCK)
    kmin = jnp.where(s_blk == 0, BLOCK, 0)
    bias_first = jnp.where(band & (kj >= kmin), 0.0, NEG_INF).astype(f32)
    bias_rest = jnp.where(band, 0.0, NEG_INF).astype(f32)
    items = [(i, g, p) for i in range(nb) for g in range(N_KV_HEADS) for p in range(2)]

    def scores(n):
        i, g, p = items[n]
        rows = slice(i * BLOCK, (i + 1) * BLOCK)
        qs = jnp.concatenate([qb[2 * g][rows], qb[2 * g + 1][rows]], axis=0)
        st = lax.dot_general(kl_ref[2 * g + p, i * BLOCK:(i + 2) * BLOCK, :], qs, (((1,), (1,)), ((), ())),
                             preferred_element_type=f32)
        return st + (bias_first if i == 0 else bias_rest)

    def finish(n, st):
        i, g, p = items[n]
        rows = slice(i * BLOCK, (i + 1) * BLOCK)
        sink = sink_ref[2 * g + p:2 * g + p + 1, :]
        m = jnp.maximum(jnp.max(st, axis=0, keepdims=True), sink)
        e = jnp.exp(st - m)
        den = jnp.sum(e, axis=0, keepdims=True) + jnp.exp(sink - m)
        ot = jnp.dot(vt_ref[g * HEAD_DIM:(g + 1) * HEAD_DIM, i * BLOCK:(i + 2) * BLOCK], e.astype(bf16),
                     preferred_element_type=f32) * (1.0 / den)
        h0 = 4 * g + p
        attnt_ref[h0 * HEAD_DIM:(h0 + 1) * HEAD_DIM, rows] = ot[:, 0:BLOCK]
        attnt_ref[(h0 + 2) * HEAD_DIM:(h0 + 3) * HEAD_DIM, rows] = ot[:, BLOCK:2 * BLOCK]

    n_chunks = D_MODEL // CHUNK_COLS
    zc, sga, sgg, gmc = [None] * n_chunks, [None] * n_chunks, [None] * n_chunks, [None] * n_chunks
    vn_box = []

    def proj(lo_col, c):
        cols = slice(lo_col + c * CHUNK_COLS, lo_col + (c + 1) * CHUNK_COLS)
        return jnp.dot(hb, win_ref[:, cols], preferred_element_type=f32)

    def z_chunk(c):
        zc[c] = jax.nn.gelu(proj(V_END, c))
        if c == n_chunks - 1:
            v2 = jnp.concatenate(zc[n_chunks // 2:], axis=1)
            mu = jnp.mean(v2, axis=-1, keepdims=True)
            vc = v2 - mu
            var = jnp.mean(vc * vc, axis=-1, keepdims=True)
            vn_box.append((vc * lax.rsqrt(var + EPS) * lng_ref[...] + lnb_ref[...]).astype(bf16))

    def ga_chunk(c):
        sga[c] = jax.nn.sigmoid(proj(Z_END, c))

    def gg_chunk(c):
        sgg[c] = jax.nn.sigmoid(proj(GA_END, c))

    wq = lax.broadcasted_iota(jnp.int32, (BLOCK, 2 * BLOCK), 0)
    wp = lax.broadcasted_iota(jnp.int32, (BLOCK, 2 * BLOCK), 1) % BLOCK
    causal = wp <= wq
    lo_c = lax.broadcasted_iota(jnp.int32, (BLOCK, LANES), 1) < G_GROUP_DIM
    wjs = {}

    def gmlp_dot(j, c):
        if j not in wjs:
            wjs[j] = jnp.where(causal, wcat_ref[j], 0.0).astype(bf16)
        cols = slice(j * LANES, (j + 1) * LANES)
        rhs = []
        for cc in (c, c + 1):
            vp = vn_box[0][cc * BLOCK:(cc + 1) * BLOCK, cols]
            zb = jnp.zeros_like(vp)
            rhs.append(jnp.concatenate([jnp.where(lo_c, vp, zb), jnp.where(lo_c, zb, vp)], axis=0))
        mixed = jnp.dot(wjs[j], jnp.concatenate(rhs, axis=1), preferred_element_type=f32)
        u = zc[j // 2][:, (j % 2) * LANES:(j % 2 + 1) * LANES]
        for q, cc in enumerate((c, c + 1)):
            rows = slice(cc * BLOCK, (cc + 1) * BLOCK)
            gated_ref[rows, cols] = u[rows] * (mixed[:, q * LANES:(q + 1) * LANES] + bsf_ref[:, cols])

    def gm_chunk(c):
        cols = slice(c * CHUNK_COLS, (c + 1) * CHUNK_COLS)
        gmc[c] = jnp.dot(gated_ref[...].astype(bf16), wgb_ref[:, cols], preferred_element_type=f32)

    fillers = [[functools.partial(z_chunk, c)] for c in range(n_chunks)]
    fillers += [[functools.partial(ga_chunk, c)] for c in range(n_chunks)]
    fillers += [[functools.partial(gg_chunk, c)] for c in range(n_chunks)]
    fillers += [[functools.partial(gm_chunk, c)] for c in range(n_chunks)]
    gd = [functools.partial(gmlp_dot, j, c) for j in range(G_GROUPS // 2) for c in range(0, nb, 2)]
    for q in range(len(gd)):
        fillers[n_chunks + q].append(gd[q])
    assert len(fillers) == len(items)

    fillers[0][0]()
    step_route()
    fillers[1][0]()
    step_route()
    step_route()
    fillers = fillers[2:] + [[], []]
    for n in ROUTE_AT:
        fillers[n].append(step_route)
    sts = {n: scores(n) for n in range(LOOKAHEAD)}
    for n in range(len(items)):
        finish(n, sts.pop(n))
        if n + LOOKAHEAD < len(items):
            sts[n + LOOKAHEAD] = scores(n + LOOKAHEAD)
        for f in fillers[n]:
            f()
    kl_ref[:, 0:BLOCK, :] = kl_ref[:, ts:ts + BLOCK, :]
    vt_ref[:, 0:BLOCK] = vt_ref[:, ts:ts + BLOCK]
    attn_b = attnt_ref[...].T.astype(bf16)

    mix = []
    for c in range(n_chunks):
        cols = slice(c * CHUNK_COLS, (c + 1) * CHUNK_COLS)
        a_c = jnp.dot(attn_b, wab_ref[:, cols], preferred_element_type=f32)
        mix.append((sga[c] * a_c + sgg[c] * gmc[c]).astype(bf16))
    mix_ref[cur] = jnp.concatenate(mix, axis=1)
    for _ in route:
        pass


def _const_spec(shape):
    nd = len(shape)
    return pl.BlockSpec(shape, lambda i: (0,) * nd)


def _mixer(x2, g1, win, rc, rs1, rs2, sinks, lng, lnb, wcat, bsf, wab, wgb, wout, g2, wrt, brt, tri, ltri, seq):
    t = x2.shape[0]
    n_blocks = t // TS
    blocks_per_seq = seq // TS
    cur_blk = lambda i: (jnp.minimum(i, n_blocks - 1), 0)
    prev_blk = lambda i: (jnp.maximum(i - 1, 0), 0)
    pos = lambda i: (jnp.minimum(i, n_blocks - 1) % blocks_per_seq, 0)
    consts = (sinks, lng, lnb, wcat, bsf, wab, wgb, wout, g2, wrt, brt, tri, ltri)
    in_specs = [
        pl.BlockSpec((TS, D_MODEL), cur_blk),
        pl.BlockSpec((TS, D_MODEL), prev_blk),
        _const_spec(g1.shape), _const_spec(win.shape),
        pl.BlockSpec((TS, LANES), pos), pl.BlockSpec((TS, LANES), pos), pl.BlockSpec((TS, LANES), pos),
    ] + [_const_spec(c.shape) for c in consts]
    out_shape = (
        jax.ShapeDtypeStruct((t, D_MODEL), jnp.float32),
        jax.ShapeDtypeStruct((n_blocks * GROUPS_PER_STEP * BLOCK_ROWS * ROW_LINES, LANES), jnp.uint32),
        jax.ShapeDtypeStruct((t, LANES), jnp.float32),
        jax.ShapeDtypeStruct((n_blocks * GROUPS_PER_STEP, N_EXPERTS, LANES), jnp.float32),
    )
    out_specs = (
        pl.BlockSpec((TS, D_MODEL), prev_blk),
        pl.BlockSpec((GROUPS_PER_STEP * BLOCK_ROWS * ROW_LINES, LANES), prev_blk),
        pl.BlockSpec((TS, LANES), prev_blk),
        pl.BlockSpec((GROUPS_PER_STEP, N_EXPERTS, LANES), lambda i: (jnp.maximum(i - 1, 0), 0, 0)),
    )
    scratch = [
        pltpu.VMEM((4, BLOCK + TS, LANES), jnp.bfloat16),
        pltpu.VMEM((LANES, BLOCK + TS), jnp.bfloat16),
        pltpu.VMEM((ATTN_WIDTH, TS), jnp.float32),
        pltpu.VMEM((TS, G_WIDTH), jnp.float32),
        pltpu.VMEM((2, TS, D_MODEL), jnp.bfloat16),
    ]
    return pl.pallas_call(
        functools.partial(_mixer_kernel, blocks_per_seq=blocks_per_seq, n_blocks=n_blocks),
        grid=(n_blocks + 1,),
        in_specs=in_specs,
        out_specs=out_specs,
        out_shape=out_shape,
        scratch_shapes=scratch,
        compiler_params=pltpu.CompilerParams(dimension_semantics=("arbitrary",),
                                             vmem_limit_bytes=VMEM_LIMIT),
        name="mixer",
    )(x2, x2, g1, win, rc, rs1, rs2, *consts)


def _expert_kernel(te_ref, tr0_ref, tb0_ref, tb1_ref, trows_ref, nt_ref, cum_ref, src_ref, used_ref,
                   xs_ref, wg_ref, wu_ref, wd_ref, ys_ref,
                   xbuf_ref, ybuf_ref, wgb_ref, wub_ref, wdb_ref, zero_ref, isem, osem, zsem,
                   *, n_blocks):
    j = pl.program_id(0)
    n_tiles = nt_ref[0]
    slot = j % 2
    f32, bf16 = jnp.float32, jnp.bfloat16

    def lines(rows):
        return pl.multiple_of(rows * ROW_LINES, SUBLANES)

    def run_piece(tile, b, ok, fn):
        e, r0 = te_ref[tile], tr0_ref[tile]
        bb = jnp.minimum(b, n_blocks - 1)
        start = cum_ref[e * (n_blocks + 1) + bb]
        lo = jnp.maximum(start, r0)
        n = jnp.minimum(cum_ref[e * (n_blocks + 1) + bb + 1], r0 + TM) - lo

        @pl.when(ok & (b < tb1_ref[tile]) & (n > 0))
        def _():
            fn(lines(src_ref[e * n_blocks + bb] + (lo - start)), lines(lo - r0), lines(n))

    def first_runs(tile, ok, fn):
        for k in range(FAST_RUNS):
            run_piece(tile, tb0_ref[tile] + k, ok, fn)

    def other_runs(tile, first, fn):
        lax.fori_loop(tb0_ref[tile] + first, tb1_ref[tile], lambda b, c: (run_piece(tile, b, True, fn), c)[1], 0)

    def gather_fn(s):
        return lambda src, dst, n: pltpu.make_async_copy(
            xs_ref.at[pl.ds(src, n)], xbuf_ref.at[s, pl.ds(dst, n)], isem.at[s]).start()

    def scatter_fn(s):
        return lambda src, dst, n: pltpu.make_async_copy(
            ybuf_ref.at[s, pl.ds(dst, n)], ys_ref.at[pl.ds(src, n)], osem.at[s]).start()

    def wait_rows(sem, n):
        @pl.when(n > 0)
        def _():
            pltpu.make_async_copy(xs_ref.at[pl.ds(0, lines(n))], xbuf_ref.at[0, pl.ds(0, lines(n))], sem).wait()

    def zero_tail_copy(b, c):
        first = used_ref[b] + c * TM
        n = lines(jnp.minimum(BLOCK_ROWS - first, TM))
        return pltpu.make_async_copy(zero_ref.at[pl.ds(0, n)], ys_ref.at[pl.ds(lines(b * BLOCK_ROWS + first), n)], zsem)

    @pl.when(j == 0)
    def _():
        xbuf_ref[...] = jnp.zeros_like(xbuf_ref)
        zero_ref[...] = jnp.zeros_like(zero_ref)

        def per_block(b, carry):
            chunks = (BLOCK_ROWS - used_ref[b] + TM - 1) // TM
            lax.fori_loop(0, chunks, lambda c, x: (zero_tail_copy(b, c).start(), x)[1], 0)
            lax.fori_loop(0, chunks, lambda c, x: (zero_tail_copy(b, c).wait(), x)[1], 0)
            return carry

        lax.fori_loop(0, n_blocks, per_block, 0)

        for t0 in range(GATHER_AHEAD):
            @pl.when(t0 < n_tiles)
            def _():
                other_runs(t0, 0, gather_fn(t0))

    prev = jnp.maximum(j - 1, 0)
    xslot = j % (GATHER_AHEAD + 1)
    nslot = (j + GATHER_AHEAD) % (GATHER_AHEAD + 1)

    @pl.when((j == n_tiles) & (j > 0))
    def _():
        other_runs(prev, 0, scatter_fn(1 - slot))

    @pl.when((j >= n_tiles) & (j >= 2) & (j - 2 < n_tiles))
    def _():
        wait_rows(osem.at[slot], trows_ref[jnp.maximum(j - 2, 0)])

    @pl.when(j < n_tiles)
    def _():
        @pl.when((j == 0) | (te_ref[j] != te_ref[prev]))
        def _():
            wgb_ref[...] = wg_ref[0].astype(bf16)
            wub_ref[...] = wu_ref[0].astype(bf16)
            wdb_ref[...] = wd_ref[0].astype(bf16)

        wait_rows(isem.at[xslot], trows_ref[j])
        xb = _unpack_bf16_rows(_load_rows(xbuf_ref.at[xslot], 0, TM))
        nxt = jnp.minimum(j + GATHER_AHEAD, n_tiles - 1)
        more = j + GATHER_AHEAD < n_tiles
        first_runs(prev, j > 0, scatter_fn(1 - slot))
        first_runs(nxt, more, gather_fn(nslot))
        half = D_EXPERT // 2
        g0 = jnp.dot(xb, wgb_ref[:, 0:half], preferred_element_type=f32)
        u0 = jnp.dot(xb, wub_ref[:, 0:half], preferred_element_type=f32)
        g1 = jnp.dot(xb, wgb_ref[:, half:], preferred_element_type=f32)
        u1 = jnp.dot(xb, wub_ref[:, half:], preferred_element_type=f32)
        y = jnp.dot((jax.nn.silu(g0) * u0).astype(bf16), wdb_ref[0:half, :], preferred_element_type=f32)
        y = y + jnp.dot((jax.nn.silu(g1) * u1).astype(bf16), wdb_ref[half:, :], preferred_element_type=f32)
        packed = _pack_bf16_pair(y[:, 0:PACK_COLS], y[:, PACK_COLS:])
        probe = xbuf_ref[xslot, pl.ds(0, SUBLANES, stride=ROW_LINES), :]

        @pl.when(j >= 2)
        def _():
            wait_rows(osem.at[slot], trows_ref[jnp.maximum(j - 2, 0)])

        _store_rows(ybuf_ref.at[slot], 0, packed)
        ybuf_ref[slot, pl.ds(0, SUBLANES, stride=ROW_LINES), :] = packed[0:SUBLANES, 0:LANES] | ((probe >> 16) >> 16)

        @pl.when(more)
        def _():
            other_runs(nxt, FAST_RUNS, gather_fn(nslot))

        @pl.when(j > 0)
        def _():
            other_runs(prev, FAST_RUNS, scatter_fn(1 - slot))


def _experts(tables, xs, wg, wu, wd, n_blocks, grid_tiles):
    wsel = lambda j, te, *_: (te[j], 0, 0)
    grid_spec = pltpu.PrefetchScalarGridSpec(
        num_scalar_prefetch=len(tables),
        grid=(grid_tiles,),
        in_specs=[
            pl.BlockSpec(memory_space=pl.ANY),
            pl.BlockSpec((1, D_MODEL, D_EXPERT), wsel),
            pl.BlockSpec((1, D_MODEL, D_EXPERT), wsel),
            pl.BlockSpec((1, D_EXPERT, D_MODEL), wsel),
        ],
        out_specs=pl.BlockSpec(memory_space=pl.ANY),
        scratch_shapes=[
            pltpu.VMEM((GATHER_AHEAD + 1, TM * ROW_LINES, LANES), jnp.uint32),
            pltpu.VMEM((2, TM * ROW_LINES, LANES), jnp.uint32),
            pltpu.VMEM((D_MODEL, D_EXPERT), jnp.bfloat16),
            pltpu.VMEM((D_MODEL, D_EXPERT), jnp.bfloat16),
            pltpu.VMEM((D_EXPERT, D_MODEL), jnp.bfloat16),
            pltpu.VMEM((TM * ROW_LINES, LANES), jnp.uint32),
            pltpu.SemaphoreType.DMA((GATHER_AHEAD + 1,)),
            pltpu.SemaphoreType.DMA((2,)),
            pltpu.SemaphoreType.DMA,
        ],
    )
    return pl.pallas_call(
        functools.partial(_expert_kernel, n_blocks=n_blocks),
        grid_spec=grid_spec,
        out_shape=jax.ShapeDtypeStruct(xs.shape, jnp.uint32),
        compiler_params=pltpu.CompilerParams(dimension_semantics=("arbitrary",),
                                             vmem_limit_bytes=VMEM_LIMIT),
        name="experts",
    )(*tables, xs, wg, wu, wd)


def _combine_kernel(x1_ref, rw_ref, gf_ref, ys_ref, o_ref):
    rw = rw_ref[...]
    k_pad = -(-BLOCK_ROWS // LANES) * LANES
    io = lax.broadcasted_iota(jnp.int32, (GROUP, k_pad), 1).astype(jnp.float32)
    parts = []
    for gi in range(GROUPS_PER_STEP):
        r = rw[gi * GROUP:(gi + 1) * GROUP]
        wsel = jnp.where(io == r[:, 2:3], r[:, 0:1], 0.0) + jnp.where(io == r[:, 3:4], r[:, 1:2], 0.0)
        ys = _unpack_bf16_rows(_load_rows(ys_ref, gi * BLOCK_ROWS, BLOCK_ROWS))
        ys = jnp.concatenate([ys, jnp.zeros((k_pad - BLOCK_ROWS, D_MODEL), jnp.bfloat16)], axis=0)
        parts.append(jnp.dot(wsel.astype(jnp.bfloat16), ys, preferred_element_type=jnp.float32))
    o_ref[...] = _rms(x1_ref[...] + jnp.concatenate(parts, axis=0), gf_ref[...])


def _combine(x1, rw, gf, ys):
    t = x1.shape[0]
    tok = lambda i: (i, 0)
    return pl.pallas_call(
        _combine_kernel,
        grid=(t // TS,),
        in_specs=[
            pl.BlockSpec((TS, D_MODEL), tok),
            pl.BlockSpec((TS, LANES), tok),
            pl.BlockSpec((1, D_MODEL), lambda i: (0, 0)),
            pl.BlockSpec((GROUPS_PER_STEP * BLOCK_ROWS * ROW_LINES, LANES), tok),
        ],
        out_specs=pl.BlockSpec((TS, D_MODEL), tok),
        out_shape=jax.ShapeDtypeStruct((t, D_MODEL), jnp.float32),
        compiler_params=pltpu.CompilerParams(dimension_semantics=("arbitrary",),
                                             vmem_limit_bytes=VMEM_LIMIT),
        name="combine",
    )(x1, rw, gf, ys)


def _rope_lane_tables(seq):
    inv_freq = ROPE_THETA ** (-jnp.arange(0, ROT_DIM, 2, dtype=jnp.float32) / ROT_DIM)
    ang = jnp.arange(seq, dtype=jnp.float32)[:, None] * inv_freq[None, :]
    cos, sin = jnp.cos(ang), jnp.sin(ang)
    half = ROT_DIM // 2
    ones = jnp.ones((seq, HEAD_DIM - ROT_DIM), jnp.float32)
    zeros = jnp.zeros((seq, HEAD_DIM - ROT_DIM), jnp.float32)
    zh = jnp.zeros((seq, half), jnp.float32)
    c = jnp.concatenate([cos, cos, ones], axis=1)
    s1 = jnp.concatenate([zh, sin, zeros], axis=1)
    s2 = jnp.concatenate([-sin, zh, zeros], axis=1)
    rep = LANES // HEAD_DIM
    return jnp.tile(c, (1, rep)), jnp.tile(s1, (1, rep)), jnp.tile(s2, (1, rep))


def _pick(table, idx):
    return jnp.sum(jnp.where(idx[:, None] == jnp.arange(table.shape[0])[None, :], table[None, :], 0), axis=1)


def _tile_tables(cnt, n_blocks, grid_tiles):
    i32 = jnp.int32
    n = cnt[:, :, 0].astype(i32)
    n8 = (n + ROW_ALIGN - 1) // ROW_ALIGN * ROW_ALIGN
    loc = jnp.cumsum(n8, axis=1) - n8
    used = jnp.sum(n8, axis=1)
    cum = jnp.concatenate([jnp.zeros((1, N_EXPERTS), i32), jnp.cumsum(n8, axis=0)], axis=0).T
    total = cum[:, -1]
    src = (jnp.arange(n_blocks, dtype=i32)[:, None] * BLOCK_ROWS + loc).T
    tiles_e = (total + TM - 1) // TM
    tile_end = jnp.cumsum(tiles_e)
    n_tiles = tile_end[-1]
    j = jnp.arange(grid_tiles, dtype=i32)
    te = jnp.minimum(jnp.sum(tile_end[None, :] <= j[:, None], axis=1), N_EXPERTS - 1).astype(i32)
    live = j < n_tiles
    te = jnp.where(live, te, _pick(te, jnp.maximum(n_tiles - 1, 0)[None])[0])
    r0 = (j - _pick(tile_end - tiles_e, te)) * TM
    cum_t = cum[te]
    b0 = jnp.sum(cum_t[:, 1:] <= r0[:, None], axis=1)
    b1 = jnp.sum(cum_t[:, :-1] < (r0 + TM)[:, None], axis=1)
    rows = jnp.clip(_pick(total, te) - r0, 0, TM)
    zero = jnp.zeros_like(j)
    return (te, jnp.where(live, r0, zero), jnp.where(live, b0, zero).astype(i32),
            jnp.where(live, b1, zero).astype(i32), jnp.where(live, rows, zero).astype(i32),
            n_tiles.reshape(1).astype(i32), cum.reshape(-1), src.reshape(-1).astype(i32), used.astype(i32))


def kernel(x, norm1_g, w_in, attn_sinks, gmlp_ln_g, gmlp_ln_b, gmlp_ws, gmlp_bs, w_attn_branch,
           w_gmlp_branch, w_out, norm2_g, router_group_w, router_group_b, router_expert_w,
           router_expert_b, expert_w_gate, expert_w_up, expert_w_down, final_norm_g):
    b, s, d = x.shape
    assert d == D_MODEL and s % TS == 0 and norm1_g.shape[0] == 1
    t = b * s
    n_blocks = t // TS
    bf16, f32 = jnp.bfloat16, jnp.float32
    x2 = x.reshape(t, d)

    rc, rs1, rs2 = _rope_lane_tables(s)
    sk = attn_sinks[0].astype(f32)
    sinks = jnp.stack([
        jnp.concatenate([jnp.full((BLOCK,), 1.0, f32) * sk[4 * g + p], jnp.full((BLOCK,), 1.0, f32) * sk[4 * g + 2 + p]])
        for g in range(N_KV_HEADS) for p in range(2)])
    ws = gmlp_ws[0]
    wcat = jnp.stack([jnp.concatenate([ws[2 * j], ws[2 * j + 1]], axis=1) for j in range(G_GROUPS // 2)])
    bsf = jnp.repeat(gmlp_bs[0].T, G_GROUP_DIM, axis=1)
    wrt = jnp.concatenate([router_group_w[0].T, jnp.zeros((8 - N_GROUPS, d), f32), router_expert_w[0].T], axis=0)
    brt = jnp.concatenate([router_group_b[0], jnp.zeros((8 - N_GROUPS,), f32), router_expert_b[0]])
    brt = jnp.broadcast_to(brt[:, None], (ROUTER_ROWS, TS))
    tok_ids = jnp.arange(TS)
    tri = ((tok_ids[:, None] < tok_ids[None, :]) & (tok_ids[:, None] // GROUP == tok_ids[None, :] // GROUP)
           ).astype(bf16)
    ltri = (jnp.arange(LANES)[None, :] < jnp.arange(N_EXPERTS)[:, None]).astype(bf16)

    x1, xs, rw, cnt = _mixer(
        x2, norm1_g, w_in[0].astype(bf16), rc, rs1, rs2, sinks, gmlp_ln_g, gmlp_ln_b, wcat, bsf,
        w_attn_branch[0].astype(bf16), w_gmlp_branch[0].astype(bf16), w_out[0].astype(bf16),
        norm2_g, wrt.astype(bf16), brt, tri, ltri, s)

    n_groups = n_blocks * GROUPS_PER_STEP
    max_tiles = (TOP_K * t + n_groups * N_EXPERTS * (ROW_ALIGN - 1)) // TM + N_EXPERTS
    grid_tiles = max_tiles + 2
    tables = _tile_tables(cnt, n_groups, grid_tiles)
    ys = _experts(tables, xs, expert_w_gate[0], expert_w_up[0], expert_w_down[0], n_groups, grid_tiles)
    out = _combine(x1, rw, final_norm_g.reshape(1, d), ys)
    return out.reshape(b, s, d)
```

```python
import functools
import math

import jax
import jax.numpy as jnp
from jax import lax
from jax.experimental import pallas as pl
from jax.experimental.pallas import tpu as pltpu

D_MODEL = 1024
HEAD_DIM = 64
N_HEADS = 8
N_KV_HEADS = 2
BLOCK = 128
ROT_DIM = HEAD_DIM // 4
ROPE_THETA = 500000.0
ATTN_WIDTH = N_HEADS * HEAD_DIM
KV_WIDTH = N_KV_HEADS * HEAD_DIM
G_GROUPS = 8
G_GROUP_DIM = 64
G_WIDTH = G_GROUPS * G_GROUP_DIM
Q_END = ATTN_WIDTH
K_END = Q_END + KV_WIDTH
V_END = K_END + KV_WIDTH
Z_END = V_END + 2 * G_WIDTH
GA_END = Z_END + D_MODEL
IN_COLS = GA_END + D_MODEL
N_GROUPS = 4
EXPERTS_PER_GROUP = 8
N_EXPERTS = N_GROUPS * EXPERTS_PER_GROUP
TOP_K = 2
D_EXPERT = 512
EPS = 1e-5
NEG_INF = -1e30

LANES = 128
SUBLANES = 8
ROUTER_ROWS = 8 + N_EXPERTS
PACK_COLS = D_MODEL // 2
ROW_LINES = PACK_COLS // LANES
ROW_ALIGN = SUBLANES // ROW_LINES

TS = 512
TM = 512
GATHER_AHEAD = 2
FAST_RUNS = 36
GROUP = 256
CHUNK_COLS = 256
LOOKAHEAD = 3
ROUTE_AT = (0, 1)
GROUPS_PER_STEP = TS // GROUP
BLOCK_ROWS = TOP_K * GROUP + N_EXPERTS * (ROW_ALIGN - 1)
VMEM_LIMIT = 58 * 1024 * 1024


def _rms(x, g):
    return x * lax.rsqrt(jnp.mean(x * x, axis=-1, keepdims=True) + EPS) * g


def _pack_bf16_pair(a, b):
    ua = lax.bitcast_convert_type(a.astype(jnp.bfloat16).astype(jnp.float32), jnp.uint32)
    ub = lax.bitcast_convert_type(b.astype(jnp.bfloat16).astype(jnp.float32), jnp.uint32)
    return ub | (ua >> 16)


def _unpack_bf16_rows(w):
    lo = lax.bitcast_convert_type(w << 16, jnp.float32)
    hi = lax.bitcast_convert_type(w & jnp.uint32(0xFFFF0000), jnp.float32)
    return jnp.concatenate([lo.astype(jnp.bfloat16), hi.astype(jnp.bfloat16)], axis=1)


def _load_rows(lines_ref, first_row, rows):
    return jnp.concatenate([lines_ref[pl.ds(first_row * ROW_LINES + c, rows, stride=ROW_LINES), :]
                            for c in range(ROW_LINES)], axis=1)


def _store_rows(lines_ref, first_row, packed):
    for c in range(ROW_LINES):
        lines_ref[pl.ds(first_row * ROW_LINES + c, packed.shape[0], stride=ROW_LINES), :] = \
            packed[:, c * LANES:(c + 1) * LANES]


def _route_and_compact(xnb, wrt_ref, brt_ref, tri_ref, ltri_ref, xs_ref, rw_ref, cnt_ref):
    ts = xnb.shape[0]
    f32, bf16 = jnp.float32, jnp.bfloat16
    lt = lax.dot_general(wrt_ref[...], xnb, (((1,), (1,)), ((), ())),
                         preferred_element_type=f32) + brt_ref[...]
    yield
    r = [lt[i:i + 1, :] for i in range(N_GROUPS)]
    gm_ = jnp.maximum(jnp.maximum(r[0], r[1]), jnp.maximum(r[2], r[3]))
    gsel = jnp.where(r[0] == gm_, 0.0, jnp.where(r[1] == gm_, 1.0, jnp.where(r[2] == gm_, 2.0, 3.0)))
    g_w = 1.0 / (jnp.exp(r[0] - gm_) + jnp.exp(r[1] - gm_) + jnp.exp(r[2] - gm_) + jnp.exp(r[3] - gm_))
    eg = [lt[8 + 8 * kk:16 + 8 * kk, :] for kk in range(N_GROUPS)]
    ein = jnp.where(gsel == 0.0, eg[0], jnp.where(gsel == 1.0, eg[1], jnp.where(gsel == 2.0, eg[2], eg[3])))
    io8 = lax.broadcasted_iota(jnp.int32, (EXPERTS_PER_GROUP, ts), 0).astype(f32)
    m1 = jnp.max(ein, axis=0, keepdims=True)
    i1 = jnp.min(jnp.where(ein == m1, io8, 8.0), axis=0, keepdims=True)
    e2 = jnp.where(io8 == i1, NEG_INF, ein)
    m2 = jnp.max(e2, axis=0, keepdims=True)
    i2 = jnp.min(jnp.where(e2 == m2, io8, 8.0), axis=0, keepdims=True)
    t2 = jnp.exp(m2 - m1)
    w1 = 1.0 / (1.0 + t2)
    wt0, wt1 = g_w * w1, g_w * (t2 * w1)
    eid0, eid1 = gsel * 8.0 + i1, gsel * 8.0 + i2

    io32 = lax.broadcasted_iota(jnp.int32, (N_EXPERTS, ts), 0).astype(f32)
    oh0, oh1 = io32 == eid0, io32 == eid1
    both = jnp.where(oh0, 1.0, jnp.where(oh1, 1.0, 0.0))
    pref = jnp.dot(both.astype(bf16), tri_ref[...], preferred_element_type=f32)
    yield
    gid = lax.broadcasted_iota(jnp.int32, (N_EXPERTS, ts), 1) // GROUP
    units = jnp.zeros((N_EXPERTS, ts), f32)
    for gi in range(GROUPS_PER_STEP):
        n_e = jnp.sum(jnp.where(gid == gi, both, 0.0), axis=1, keepdims=True)
        cnt_ref[gi] = jnp.broadcast_to(n_e, (N_EXPERTS, LANES))
        units = jnp.where(gid == gi, jnp.floor((n_e + (ROW_ALIGN - 1.0)) * (1.0 / ROW_ALIGN)), units)
    units = jnp.concatenate([units, jnp.zeros((LANES - N_EXPERTS, ts), f32)], axis=0).astype(bf16)
    tot = ROW_ALIGN * jnp.dot(ltri_ref[...], units, preferred_element_type=f32) + pref
    yield
    slot0 = jnp.sum(jnp.where(oh0, tot, 0.0), axis=0, keepdims=True)
    slot1 = jnp.sum(jnp.where(oh1, tot, 0.0), axis=0, keepdims=True)

    io128 = lax.broadcasted_iota(jnp.int32, (LANES, ts), 0)
    rw_ref[...] = jnp.where(io128 == 0, wt0, jnp.where(io128 == 1, wt1,
                            jnp.where(io128 == 2, slot0, jnp.where(io128 == 3, slot1, 0.0)))).T

    s0i, s1i = slot0.astype(jnp.int32), slot1.astype(jnp.int32)
    io = lax.broadcasted_iota(jnp.int32, (BLOCK_ROWS, GROUP), 0)
    for gi in range(GROUPS_PER_STEP):
        tok = slice(gi * GROUP, (gi + 1) * GROUP)
        onehot = jnp.where(io == s0i[:, tok], 1.0, jnp.where(io == s1i[:, tok], 1.0, 0.0)).astype(bf16)
        rows = jnp.dot(onehot, xnb[tok, :], preferred_element_type=f32)
        _store_rows(xs_ref, gi * BLOCK_ROWS, _pack_bf16_pair(rows[:, 0:PACK_COLS], rows[:, PACK_COLS:]))
        yield


def _mixer_kernel(x_ref, xp_ref, g1_ref, win_ref, rc_ref, rs1_ref, rs2_ref, sink_ref, lng_ref, lnb_ref,
                  wcat_ref, bsf_ref, wab_ref, wgb_ref, wout_ref, g2_ref, wrt_ref, brt_ref, tri_ref, ltri_ref,
                  x1_ref, xs_ref, rw_ref, cnt_ref,
                  kl_ref, vt_ref, attnt_ref, gated_ref, mix_ref, *, blocks_per_seq, n_blocks):
    step = pl.program_id(0)
    s_blk = jnp.minimum(step, n_blocks - 1) % blocks_per_seq
    cur = step % 2
    ts = x_ref.shape[0]
    nb = ts // BLOCK
    f32, bf16 = jnp.float32, jnp.bfloat16

    @pl.when(step == 0)
    def _():
        mix_ref[1] = jnp.zeros((ts, D_MODEL), bf16)

    @pl.when(s_blk == 0)
    def _():
        kl_ref[:, 0:BLOCK, :] = jnp.zeros((4, BLOCK, LANES), bf16)
        vt_ref[:, 0:BLOCK] = jnp.zeros((LANES, BLOCK), bf16)

    x1 = xp_ref[...] + jnp.dot(mix_ref[1 - cur], wout_ref[...], preferred_element_type=f32)
    x1_ref[...] = x1
    route = _route_and_compact(_rms(x1, g2_ref[...]).astype(bf16), wrt_ref, brt_ref, tri_ref, ltri_ref,
                               xs_ref, rw_ref, cnt_ref)
    step_route = lambda: next(route, None)

    x = x_ref[...]
    hb = _rms(x, g1_ref[...]).astype(bf16)

    qkv = jnp.dot(hb, win_ref[:, 0:V_END], preferred_element_type=f32)
    rc, rs1, rs2 = rc_ref[...], rs1_ref[...], rs2_ref[...]

    def rope(t):
        return t * rc + pltpu.roll(t, 8, 1) * rs1 + pltpu.roll(t, LANES - 8, 1) * rs2

    scale = 1.0 / math.sqrt(HEAD_DIM)
    qb = [(rope(qkv[:, j * LANES:(j + 1) * LANES]) * scale).astype(bf16) for j in range(4)]
    k = rope(qkv[:, Q_END:K_END])
    v = qkv[:, K_END:V_END]
    k_sw = pltpu.roll(k, HEAD_DIM, 1)
    lo = lax.broadcasted_iota(jnp.int32, (ts, LANES), 1) < HEAD_DIM
    zero = jnp.zeros_like(k)
    kl_ref[0, BLOCK:BLOCK + ts, :] = jnp.where(lo, k, zero).astype(bf16)
    kl_ref[1, BLOCK:BLOCK + ts, :] = jnp.where(lo, zero, k_sw).astype(bf16)
    kl_ref[2, BLOCK:BLOCK + ts, :] = jnp.where(lo, k_sw, zero).astype(bf16)
    kl_ref[3, BLOCK:BLOCK + ts, :] = jnp.where(lo, zero, k).astype(bf16)
    vt_ref[:, BLOCK:BLOCK + ts] = v.T.astype(bf16)

    kj = lax.broadcasted_iota(jnp.int32, (2 * BLOCK, 2 * BLOCK), 0)
    qi = lax.broadcasted_iota(jnp.int32, (2 * BLOCK, 2 * BLOCK), 1) % BLOCK
    band = (kj > qi) & (kj <= qi + BLOCK)
    kmin = jnp.where(s_blk == 0, BLOCK, 0)
    bias_first = jnp.where(band & (kj >= kmin), 0.0, NEG_INF).astype(f32)
    bias_rest = jnp.where(band, 0.0, NEG_INF).astype(f32)
    items = [(i, g, p) for i in range(nb) for g in range(N_KV_HEADS) for p in range(2)]

    def scores(n):
        i, g, p = items[n]
        rows = slice(i * BLOCK, (i + 1) * BLOCK)
        qs = jnp.concatenate([qb[2 * g][rows], qb[2 * g + 1][rows]], axis=0)
        st = lax.dot_general(kl_ref[2 * g + p, i * BLOCK:(i + 2) * BLOCK, :], qs, (((1,), (1,)), ((), ())),
                             preferred_element_type=f32)
        return st + (bias_first if i == 0 else bias_rest)

    def finish(n, st):
        i, g, p = items[n]
        rows = slice(i * BLOCK, (i + 1) * BLOCK)
        sink = sink_ref[2 * g + p:2 * g + p + 1, :]
        m = jnp.maximum(jnp.max(st, axis=0, keepdims=True), sink)
        e = jnp.exp(st - m)
        den = jnp.sum(e, axis=0, keepdims=True) + jnp.exp(sink - m)
        ot = jnp.dot(vt_ref[g * HEAD_DIM:(g + 1) * HEAD_DIM, i * BLOCK:(i + 2) * BLOCK], e.astype(bf16),
                     preferred_element_type=f32) * (1.0 / den)
        h0 = 4 * g + p
        attnt_ref[h0 * HEAD_DIM:(h0 + 1) * HEAD_DIM, rows] = ot[:, 0:BLOCK]
        attnt_ref[(h0 + 2) * HEAD_DIM:(h0 + 3) * HEAD_DIM, rows] = ot[:, BLOCK:2 * BLOCK]

    n_chunks = D_MODEL // CHUNK_COLS
    zc, sga, sgg, gmc = [None] * n_chunks, [None] * n_chunks, [None] * n_chunks, [None] * n_chunks
    vn_box = []

    def proj(lo_col, c):
        cols = slice(lo_col + c * CHUNK_COLS, lo_col + (c + 1) * CHUNK_COLS)
        return jnp.dot(hb, win_ref[:, cols], preferred_element_type=f32)

    def z_chunk(c):
        zc[c] = jax.nn.gelu(proj(V_END, c))
        if c == n_chunks - 1:
            v2 = jnp.concatenate(zc[n_chunks // 2:], axis=1)
            mu = jnp.mean(v2, axis=-1, keepdims=True)
            vc = v2 - mu
            var = jnp.mean(vc * vc, axis=-1, keepdims=True)
            vn_box.append((vc * lax.rsqrt(var + EPS) * lng_ref[...] + lnb_ref[...]).astype(bf16))

    def ga_chunk(c):
        sga[c] = jax.nn.sigmoid(proj(Z_END, c))

    def gg_chunk(c):
        sgg[c] = jax.nn.sigmoid(proj(GA_END, c))

    wq = lax.broadcasted_iota(jnp.int32, (BLOCK, 2 * BLOCK), 0)
    wp = lax.broadcasted_iota(jnp.int32, (BLOCK, 2 * BLOCK), 1) % BLOCK
    causal = wp <= wq
    lo_c = lax.broadcasted_iota(jnp.int32, (BLOCK, LANES), 1) < G_GROUP_DIM
    wjs = {}

    def gmlp_dot(j, c):
        if j not in wjs:
            wjs[j] = jnp.where(causal, wcat_ref[j], 0.0).astype(bf16)
        cols = slice(j * LANES, (j + 1) * LANES)
        rhs = []
        for cc in (c, c + 1):
            vp = vn_box[0][cc * BLOCK:(cc + 1) * BLOCK, cols]
            zb = jnp.zeros_like(vp)
            rhs.append(jnp.concatenate([jnp.where(lo_c, vp, zb), jnp.where(lo_c, zb, vp)], axis=0))
        mixed = jnp.dot(wjs[j], jnp.concatenate(rhs, axis=1), preferred_element_type=f32)
        u = zc[j // 2][:, (j % 2) * LANES:(j % 2 + 1) * LANES]
        for q, cc in enumerate((c, c + 1)):
            rows = slice(cc * BLOCK, (cc + 1) * BLOCK)
            gated_ref[rows, cols] = u[rows] * (mixed[:, q * LANES:(q + 1) * LANES] + bsf_ref[:, cols])

    def gm_chunk(c):
        cols = slice(c * CHUNK_COLS, (c + 1) * CHUNK_COLS)
        gmc[c] = jnp.dot(gated_ref[...].astype(bf16), wgb_ref[:, cols], preferred_element_type=f32)

    fillers = [[functools.partial(z_chunk, c)] for c in range(n_chunks)]
    fillers += [[functools.partial(ga_chunk, c)] for c in range(n_chunks)]
    fillers += [[functools.partial(gg_chunk, c)] for c in range(n_chunks)]
    fillers += [[functools.partial(gm_chunk, c)] for c in range(n_chunks)]
    gd = [functools.partial(gmlp_dot, j, c) for j in range(G_GROUPS // 2) for c in range(0, nb, 2)]
    for q in range(len(gd)):
        fillers[n_chunks + q].append(gd[q])
    assert len(fillers) == len(items)

    fillers[0][0]()
    step_route()
    fillers[1][0]()
    step_route()
    step_route()
    fillers = fillers[2:] + [[], []]
    for n in ROUTE_AT:
        fillers[n].append(step_route)
    sts = {n: scores(n) for n in range(LOOKAHEAD)}
    for n in range(len(items)):
        finish(n, sts.pop(n))
        if n + LOOKAHEAD < len(items):
            sts[n + LOOKAHEAD] = scores(n + LOOKAHEAD)
        for f in fillers[n]:
            f()
    kl_ref[:, 0:BLOCK, :] = kl_ref[:, ts:ts + BLOCK, :]
    vt_ref[:, 0:BLOCK] = vt_ref[:, ts:ts + BLOCK]
    attn_b = attnt_ref[...].T.astype(bf16)

    mix = []
    for c in range(n_chunks):
        cols = slice(c * CHUNK_COLS, (c + 1) * CHUNK_COLS)
        a_c = jnp.dot(attn_b, wab_ref[:, cols], preferred_element_type=f32)
        mix.append((sga[c] * a_c + sgg[c] * gmc[c]).astype(bf16))
    mix_ref[cur] = jnp.concatenate(mix, axis=1)
    for _ in route:
        pass


def _const_spec(shape):
    nd = len(shape)
    return pl.BlockSpec(shape, lambda i: (0,) * nd)


def _mixer(x2, g1, win, rc, rs1, rs2, sinks, lng, lnb, wcat, bsf, wab, wgb, wout, g2, wrt, brt, tri, ltri, seq):
    t = x2.shape[0]
    n_blocks = t // TS
    blocks_per_seq = seq // TS
    cur_blk = lambda i: (jnp.minimum(i, n_blocks - 1), 0)
    prev_blk = lambda i: (jnp.maximum(i - 1, 0), 0)
    pos = lambda i: (jnp.minimum(i, n_blocks - 1) % blocks_per_seq, 0)
    consts = (sinks, lng, lnb, wcat, bsf, wab, wgb, wout, g2, wrt, brt, tri, ltri)
    in_specs = [
        pl.BlockSpec((TS, D_MODEL), cur_blk),
        pl.BlockSpec((TS, D_MODEL), prev_blk),
        _const_spec(g1.shape), _const_spec(win.shape),
        pl.BlockSpec((TS, LANES), pos), pl.BlockSpec((TS, LANES), pos), pl.BlockSpec((TS, LANES), pos),
    ] + [_const_spec(c.shape) for c in consts]
    out_shape = (
        jax.ShapeDtypeStruct((t, D_MODEL), jnp.float32),
        jax.ShapeDtypeStruct((n_blocks * GROUPS_PER_STEP * BLOCK_ROWS * ROW_LINES, LANES), jnp.uint32),
        jax.ShapeDtypeStruct((t, LANES), jnp.float32),
        jax.ShapeDtypeStruct((n_blocks * GROUPS_PER_STEP, N_EXPERTS, LANES), jnp.float32),
    )
    out_specs = (
        pl.BlockSpec((TS, D_MODEL), prev_blk),
        pl.BlockSpec((GROUPS_PER_STEP * BLOCK_ROWS * ROW_LINES, LANES), prev_blk),
        pl.BlockSpec((TS, LANES), prev_blk),
        pl.BlockSpec((GROUPS_PER_STEP, N_EXPERTS, LANES), lambda i: (jnp.maximum(i - 1, 0), 0, 0)),
    )
    scratch = [
        pltpu.VMEM((4, BLOCK + TS, LANES), jnp.bfloat16),
        pltpu.VMEM((LANES, BLOCK + TS), jnp.bfloat16),
        pltpu.VMEM((ATTN_WIDTH, TS), jnp.float32),
        pltpu.VMEM((TS, G_WIDTH), jnp.float32),
        pltpu.VMEM((2, TS, D_MODEL), jnp.bfloat16),
    ]
    return pl.pallas_call(
        functools.partial(_mixer_kernel, blocks_per_seq=blocks_per_seq, n_blocks=n_blocks),
        grid=(n_blocks + 1,),
        in_specs=in_specs,
        out_specs=out_specs,
        out_shape=out_shape,
        scratch_shapes=scratch,
        compiler_params=pltpu.CompilerParams(dimension_semantics=("arbitrary",),
                                             vmem_limit_bytes=VMEM_LIMIT),
        name="mixer",
    )(x2, x2, g1, win, rc, rs1, rs2, *consts)


def _expert_kernel(te_ref, tr0_ref, tb0_ref, tb1_ref, trows_ref, nt_ref, cum_ref, src_ref, used_ref,
                   xs_ref, wg_ref, wu_ref, wd_ref, ys_ref,
                   xbuf_ref, ybuf_ref, wgb_ref, wub_ref, wdb_ref, zero_ref, isem, osem, zsem,
                   *, n_blocks):
    j = pl.program_id(0)
    n_tiles = nt_ref[0]
    slot = j % 2
    f32, bf16 = jnp.float32, jnp.bfloat16

    def lines(rows):
        return pl.multiple_of(rows * ROW_LINES, SUBLANES)

    def run_piece(tile, b, ok, fn):
        e, r0 = te_ref[tile], tr0_ref[tile]
        bb = jnp.minimum(b, n_blocks - 1)
        start = cum_ref[e * (n_blocks + 1) + bb]
        lo = jnp.maximum(start, r0)
        n = jnp.minimum(cum_ref[e * (n_blocks + 1) + bb + 1], r0 + TM) - lo

        @pl.when(ok & (b < tb1_ref[tile]) & (n > 0))
        def _():
            fn(lines(src_ref[e * n_blocks + bb] + (lo - start)), lines(lo - r0), lines(n))

    def first_runs(tile, ok, fn):
        for k in range(FAST_RUNS):
            run_piece(tile, tb0_ref[tile] + k, ok, fn)

    def other_runs(tile, first, fn):
        lax.fori_loop(tb0_ref[tile] + first, tb1_ref[tile], lambda b, c: (run_piece(tile, b, True, fn), c)[1], 0)

    def gather_fn(s):
        return lambda src, dst, n: pltpu.make_async_copy(
            xs_ref.at[pl.ds(src, n)], xbuf_ref.at[s, pl.ds(dst, n)], isem.at[s]).start()

    def scatter_fn(s):
        return lambda src, dst, n: pltpu.make_async_copy(
            ybuf_ref.at[s, pl.ds(dst, n)], ys_ref.at[pl.ds(src, n)], osem.at[s]).start(priority=1)

    def wait_rows(sem, n):
        @pl.when(n > 0)
        def _():
            pltpu.make_async_copy(xs_ref.at[pl.ds(0, lines(n))], xbuf_ref.at[0, pl.ds(0, lines(n))], sem).wait()

    def zero_tail_copy(b, c):
        first = used_ref[b] + c * TM
        n = lines(jnp.minimum(BLOCK_ROWS - first, TM))
        return pltpu.make_async_copy(zero_ref.at[pl.ds(0, n)], ys_ref.at[pl.ds(lines(b * BLOCK_ROWS + first), n)], zsem)

    @pl.when(j == 0)
    def _():
        xbuf_ref[...] = jnp.zeros_like(xbuf_ref)
        zero_ref[...] = jnp.zeros_like(zero_ref)

        def per_block(b, carry):
            chunks = (BLOCK_ROWS - used_ref[b] + TM - 1) // TM
            lax.fori_loop(0, chunks, lambda c, x: (zero_tail_copy(b, c).start(), x)[1], 0)
            lax.fori_loop(0, chunks, lambda c, x: (zero_tail_copy(b, c).wait(), x)[1], 0)
            return carry

        lax.fori_loop(0, n_blocks, per_block, 0)

        for t0 in range(GATHER_AHEAD):
            @pl.when(t0 < n_tiles)
            def _():
                other_runs(t0, 0, gather_fn(t0))

    prev = jnp.maximum(j - 1, 0)
    xslot = j % (GATHER_AHEAD + 1)
    nslot = (j + GATHER_AHEAD) % (GATHER_AHEAD + 1)

    @pl.when((j == n_tiles) & (j > 0))
    def _():
        other_runs(prev, 0, scatter_fn(1 - slot))

    @pl.when((j >= n_tiles) & (j >= 2) & (j - 2 < n_tiles))
    def _():
        wait_rows(osem.at[slot], trows_ref[jnp.maximum(j - 2, 0)])

    @pl.when(j < n_tiles)
    def _():
        @pl.when((j == 0) | (te_ref[j] != te_ref[prev]))
        def _():
            wgb_ref[...] = wg_ref[0].astype(bf16)
            wub_ref[...] = wu_ref[0].astype(bf16)
            wdb_ref[...] = wd_ref[0].astype(bf16)

        wait_rows(isem.at[xslot], trows_ref[j])
        xb = _unpack_bf16_rows(_load_rows(xbuf_ref.at[xslot], 0, TM))
        nxt = jnp.minimum(j + GATHER_AHEAD, n_tiles - 1)
        more = j + GATHER_AHEAD < n_tiles
        first_runs(prev, j > 0, scatter_fn(1 - slot))
        first_runs(nxt, more, gather_fn(nslot))
        half = D_EXPERT // 2
        g0 = jnp.dot(xb, wgb_ref[:, 0:half], preferred_element_type=f32)
        u0 = jnp.dot(xb, wub_ref[:, 0:half], preferred_element_type=f32)
        g1 = jnp.dot(xb, wgb_ref[:, half:], preferred_element_type=f32)
        u1 = jnp.dot(xb, wub_ref[:, half:], preferred_element_type=f32)
        y = jnp.dot((jax.nn.silu(g0) * u0).astype(bf16), wdb_ref[0:half, :], preferred_element_type=f32)
        y = y + jnp.dot((jax.nn.silu(g1) * u1).astype(bf16), wdb_ref[half:, :], preferred_element_type=f32)
        packed = _pack_bf16_pair(y[:, 0:PACK_COLS], y[:, PACK_COLS:])
        probe = xbuf_ref[xslot, pl.ds(0, SUBLANES, stride=ROW_LINES), :]

        @pl.when(j >= 2)
        def _():
            wait_rows(osem.at[slot], trows_ref[jnp.maximum(j - 2, 0)])

        _store_rows(ybuf_ref.at[slot], 0, packed)
        ybuf_ref[slot, pl.ds(0, SUBLANES, stride=ROW_LINES), :] = packed[0:SUBLANES, 0:LANES] | ((probe >> 16) >> 16)

        @pl.when(more)
        def _():
            other_runs(nxt, FAST_RUNS, gather_fn(nslot))

        @pl.when(j > 0)
        def _():
            other_runs(prev, FAST_RUNS, scatter_fn(1 - slot))


def _experts(tables, xs, wg, wu, wd, n_blocks, grid_tiles):
    wsel = lambda j, te, *_: (te[j], 0, 0)
    grid_spec = pltpu.PrefetchScalarGridSpec(
        num_scalar_prefetch=len(tables),
        grid=(grid_tiles,),
        in_specs=[
            pl.BlockSpec(memory_space=pl.ANY),
            pl.BlockSpec((1, D_MODEL, D_EXPERT), wsel),
            pl.BlockSpec((1, D_MODEL, D_EXPERT), wsel),
            pl.BlockSpec((1, D_EXPERT, D_MODEL), wsel),
        ],
        out_specs=pl.BlockSpec(memory_space=pl.ANY),
        scratch_shapes=[
            pltpu.VMEM((GATHER_AHEAD + 1, TM * ROW_LINES, LANES), jnp.uint32),
            pltpu.VMEM((2, TM * ROW_LINES, LANES), jnp.uint32),
            pltpu.VMEM((D_MODEL, D_EXPERT), jnp.bfloat16),
            pltpu.VMEM((D_MODEL, D_EXPERT), jnp.bfloat16),
            pltpu.VMEM((D_EXPERT, D_MODEL), jnp.bfloat16),
            pltpu.VMEM((TM * ROW_LINES, LANES), jnp.uint32),
            pltpu.SemaphoreType.DMA((GATHER_AHEAD + 1,)),
            pltpu.SemaphoreType.DMA((2,)),
            pltpu.SemaphoreType.DMA,
        ],
    )
    return pl.pallas_call(
        functools.partial(_expert_kernel, n_blocks=n_blocks),
        grid_spec=grid_spec,
        out_shape=jax.ShapeDtypeStruct(xs.shape, jnp.uint32),
        compiler_params=pltpu.CompilerParams(dimension_semantics=("arbitrary",),
                                             vmem_limit_bytes=VMEM_LIMIT),
        name="experts",
    )(*tables, xs, wg, wu, wd)


def _combine_kernel(x1_ref, rw_ref, gf_ref, ys_ref, o_ref):
    rw = rw_ref[...]
    k_pad = -(-BLOCK_ROWS // LANES) * LANES
    io = lax.broadcasted_iota(jnp.int32, (GROUP, k_pad), 1).astype(jnp.float32)
    parts = []
    for gi in range(GROUPS_PER_STEP):
        r = rw[gi * GROUP:(gi + 1) * GROUP]
        wsel = jnp.where(io == r[:, 2:3], r[:, 0:1], 0.0) + jnp.where(io == r[:, 3:4], r[:, 1:2], 0.0)
        ys = _unpack_bf16_rows(_load_rows(ys_ref, gi * BLOCK_ROWS, BLOCK_ROWS))
        ys = jnp.concatenate([ys, jnp.zeros((k_pad - BLOCK_ROWS, D_MODEL), jnp.bfloat16)], axis=0)
        parts.append(jnp.dot(wsel.astype(jnp.bfloat16), ys, preferred_element_type=jnp.float32))
    o_ref[...] = _rms(x1_ref[...] + jnp.concatenate(parts, axis=0), gf_ref[...])


def _combine(x1, rw, gf, ys):
    t = x1.shape[0]
    tok = lambda i: (i, 0)
    return pl.pallas_call(
        _combine_kernel,
        grid=(t // TS,),
        in_specs=[
            pl.BlockSpec((TS, D_MODEL), tok),
            pl.BlockSpec((TS, LANES), tok),
            pl.BlockSpec((1, D_MODEL), lambda i: (0, 0)),
            pl.BlockSpec((GROUPS_PER_STEP * BLOCK_ROWS * ROW_LINES, LANES), tok),
        ],
        out_specs=pl.BlockSpec((TS, D_MODEL), tok),
        out_shape=jax.ShapeDtypeStruct((t, D_MODEL), jnp.float32),
        compiler_params=pltpu.CompilerParams(dimension_semantics=("arbitrary",),
                                             vmem_limit_bytes=VMEM_LIMIT),
        name="combine",
    )(x1, rw, gf, ys)


def _rope_lane_tables(seq):
    inv_freq = ROPE_THETA ** (-jnp.arange(0, ROT_DIM, 2, dtype=jnp.float32) / ROT_DIM)
    ang = jnp.arange(seq, dtype=jnp.float32)[:, None] * inv_freq[None, :]
    cos, sin = jnp.cos(ang), jnp.sin(ang)
    half = ROT_DIM // 2
    ones = jnp.ones((seq, HEAD_DIM - ROT_DIM), jnp.float32)
    zeros = jnp.zeros((seq, HEAD_DIM - ROT_DIM), jnp.float32)
    zh = jnp.zeros((seq, half), jnp.float32)
    c = jnp.concatenate([cos, cos, ones], axis=1)
    s1 = jnp.concatenate([zh, sin, zeros], axis=1)
    s2 = jnp.concatenate([-sin, zh, zeros], axis=1)
    rep = LANES // HEAD_DIM
    return jnp.tile(c, (1, rep)), jnp.tile(s1, (1, rep)), jnp.tile(s2, (1, rep))


def _pick(table, idx):
    return jnp.sum(jnp.where(idx[:, None] == jnp.arange(table.shape[0])[None, :], table[None, :], 0), axis=1)


def _tile_tables(cnt, n_blocks, grid_tiles):
    i32 = jnp.int32
    n = cnt[:, :, 0].astype(i32)
    n8 = (n + ROW_ALIGN - 1) // ROW_ALIGN * ROW_ALIGN
    loc = jnp.cumsum(n8, axis=1) - n8
    used = jnp.sum(n8, axis=1)
    cum = jnp.concatenate([jnp.zeros((1, N_EXPERTS), i32), jnp.cumsum(n8, axis=0)], axis=0).T
    total = cum[:, -1]
    src = (jnp.arange(n_blocks, dtype=i32)[:, None] * BLOCK_ROWS + loc).T
    tiles_e = (total + TM - 1) // TM
    tile_end = jnp.cumsum(tiles_e)
    n_tiles = tile_end[-1]
    j = jnp.arange(grid_tiles, dtype=i32)
    te = jnp.minimum(jnp.sum(tile_end[None, :] <= j[:, None], axis=1), N_EXPERTS - 1).astype(i32)
    live = j < n_tiles
    te = jnp.where(live, te, _pick(te, jnp.maximum(n_tiles - 1, 0)[None])[0])
    r0 = (j - _pick(tile_end - tiles_e, te)) * TM
    cum_t = cum[te]
    b0 = jnp.sum(cum_t[:, 1:] <= r0[:, None], axis=1)
    b1 = jnp.sum(cum_t[:, :-1] < (r0 + TM)[:, None], axis=1)
    rows = jnp.clip(_pick(total, te) - r0, 0, TM)
    zero = jnp.zeros_like(j)
    return (te, jnp.where(live, r0, zero), jnp.where(live, b0, zero).astype(i32),
            jnp.where(live, b1, zero).astype(i32), jnp.where(live, rows, zero).astype(i32),
            n_tiles.reshape(1).astype(i32), cum.reshape(-1), src.reshape(-1).astype(i32), used.astype(i32))


def kernel(x, norm1_g, w_in, attn_sinks, gmlp_ln_g, gmlp_ln_b, gmlp_ws, gmlp_bs, w_attn_branch,
           w_gmlp_branch, w_out, norm2_g, router_group_w, router_group_b, router_expert_w,
           router_expert_b, expert_w_gate, expert_w_up, expert_w_down, final_norm_g):
    b, s, d = x.shape
    assert d == D_MODEL and s % TS == 0 and norm1_g.shape[0] == 1
    t = b * s
    n_blocks = t // TS
    bf16, f32 = jnp.bfloat16, jnp.float32
    x2 = x.reshape(t, d)

    rc, rs1, rs2 = _rope_lane_tables(s)
    sk = attn_sinks[0].astype(f32)
    sinks = jnp.stack([
        jnp.concatenate([jnp.full((BLOCK,), 1.0, f32) * sk[4 * g + p], jnp.full((BLOCK,), 1.0, f32) * sk[4 * g + 2 + p]])
        for g in range(N_KV_HEADS) for p in range(2)])
    ws = gmlp_ws[0]
    wcat = jnp.stack([jnp.concatenate([ws[2 * j], ws[2 * j + 1]], axis=1) for j in range(G_GROUPS // 2)])
    bsf = jnp.repeat(gmlp_bs[0].T, G_GROUP_DIM, axis=1)
    wrt = jnp.concatenate([router_group_w[0].T, jnp.zeros((8 - N_GROUPS, d), f32), router_expert_w[0].T], axis=0)
    brt = jnp.concatenate([router_group_b[0], jnp.zeros((8 - N_GROUPS,), f32), router_expert_b[0]])
    brt = jnp.broadcast_to(brt[:, None], (ROUTER_ROWS, TS))
    tok_ids = jnp.arange(TS)
    tri = ((tok_ids[:, None] < tok_ids[None, :]) & (tok_ids[:, None] // GROUP == tok_ids[None, :] // GROUP)
           ).astype(bf16)
    ltri = (jnp.arange(LANES)[None, :] < jnp.arange(N_EXPERTS)[:, None]).astype(bf16)

    x1, xs, rw, cnt = _mixer(
        x2, norm1_g, w_in[0].astype(bf16), rc, rs1, rs2, sinks, gmlp_ln_g, gmlp_ln_b, wcat, bsf,
        w_attn_branch[0].astype(bf16), w_gmlp_branch[0].astype(bf16), w_out[0].astype(bf16),
        norm2_g, wrt.astype(bf16), brt, tri, ltri, s)

    n_groups = n_blocks * GROUPS_PER_STEP
    max_tiles = (TOP_K * t + n_groups * N_EXPERTS * (ROW_ALIGN - 1)) // TM + N_EXPERTS
    grid_tiles = max_tiles + 2
    tables = _tile_tables(cnt, n_groups, grid_tiles)
    ys = _experts(tables, xs, expert_w_gate[0], expert_w_up[0], expert_w_down[0], n_groups, grid_tiles)
    out = _combine(x1, rw, final_norm_g.reshape(1, d), ys)
    return out.reshape(b, s, d)
```

```python
import functools
import math

import jax
import jax.numpy as jnp
from jax import lax
from jax.experimental import pallas as pl
from jax.experimental.pallas import tpu as pltpu

D_MODEL = 1024
HEAD_DIM = 64
N_HEADS = 8
N_KV_HEADS = 2
BLOCK = 128
ROT_DIM = HEAD_DIM // 4
ROPE_THETA = 500000.0
ATTN_WIDTH = N_HEADS * HEAD_DIM
KV_WIDTH = N_KV_HEADS * HEAD_DIM
G_GROUPS = 8
G_GROUP_DIM = 64
G_WIDTH = G_GROUPS * G_GROUP_DIM
Q_END = ATTN_WIDTH
K_END = Q_END + KV_WIDTH
V_END = K_END + KV_WIDTH
Z_END = V_END + 2 * G_WIDTH
GA_END = Z_END + D_MODEL
IN_COLS = GA_END + D_MODEL
N_GROUPS = 4
EXPERTS_PER_GROUP = 8
N_EXPERTS = N_GROUPS * EXPERTS_PER_GROUP
TOP_K = 2
D_EXPERT = 512
EPS = 1e-5
NEG_INF = -1e30

LANES = 128
SUBLANES = 8
ROUTER_ROWS = 8 + N_EXPERTS
PACK_COLS = D_MODEL // 2
ROW_LINES = PACK_COLS // LANES
ROW_ALIGN = SUBLANES // ROW_LINES

TS = 512
TM = 512
GATHER_AHEAD = 2
FAST_RUNS = 36
GROUP = 256
CHUNK_COLS = 256
LOOKAHEAD = 3
ROUTE_AT = (14, 15)
GROUPS_PER_STEP = TS // GROUP
BLOCK_ROWS = TOP_K * GROUP + N_EXPERTS * (ROW_ALIGN - 1)
VMEM_LIMIT = 58 * 1024 * 1024


def _rms(x, g):
    return x * lax.rsqrt(jnp.mean(x * x, axis=-1, keepdims=True) + EPS) * g


def _pack_bf16_pair(a, b):
    ua = lax.bitcast_convert_type(a.astype(jnp.bfloat16).astype(jnp.float32), jnp.uint32)
    ub = lax.bitcast_convert_type(b.astype(jnp.bfloat16).astype(jnp.float32), jnp.uint32)
    return ub | (ua >> 16)


def _unpack_bf16_rows(w):
    lo = lax.bitcast_convert_type(w << 16, jnp.float32)
    hi = lax.bitcast_convert_type(w & jnp.uint32(0xFFFF0000), jnp.float32)
    return jnp.concatenate([lo.astype(jnp.bfloat16), hi.astype(jnp.bfloat16)], axis=1)


def _load_rows(lines_ref, first_row, rows):
    return jnp.concatenate([lines_ref[pl.ds(first_row * ROW_LINES + c, rows, stride=ROW_LINES), :]
                            for c in range(ROW_LINES)], axis=1)


def _store_rows(lines_ref, first_row, packed):
    for c in range(ROW_LINES):
        lines_ref[pl.ds(first_row * ROW_LINES + c, packed.shape[0], stride=ROW_LINES), :] = \
            packed[:, c * LANES:(c + 1) * LANES]


def _route_and_compact(xnb, wrt_ref, brt_ref, tri_ref, ltri_ref, xs_ref, rw_ref, cnt_ref):
    ts = xnb.shape[0]
    f32, bf16 = jnp.float32, jnp.bfloat16
    lt = lax.dot_general(wrt_ref[...], xnb, (((1,), (1,)), ((), ())),
                         preferred_element_type=f32) + brt_ref[...]
    yield
    r = [lt[i:i + 1, :] for i in range(N_GROUPS)]
    gm_ = jnp.maximum(jnp.maximum(r[0], r[1]), jnp.maximum(r[2], r[3]))
    gsel = jnp.where(r[0] == gm_, 0.0, jnp.where(r[1] == gm_, 1.0, jnp.where(r[2] == gm_, 2.0, 3.0)))
    g_w = 1.0 / (jnp.exp(r[0] - gm_) + jnp.exp(r[1] - gm_) + jnp.exp(r[2] - gm_) + jnp.exp(r[3] - gm_))
    eg = [lt[8 + 8 * kk:16 + 8 * kk, :] for kk in range(N_GROUPS)]
    ein = jnp.where(gsel == 0.0, eg[0], jnp.where(gsel == 1.0, eg[1], jnp.where(gsel == 2.0, eg[2], eg[3])))
    io8 = lax.broadcasted_iota(jnp.int32, (EXPERTS_PER_GROUP, ts), 0).astype(f32)
    m1 = jnp.max(ein, axis=0, keepdims=True)
    i1 = jnp.min(jnp.where(ein == m1, io8, 8.0), axis=0, keepdims=True)
    e2 = jnp.where(io8 == i1, NEG_INF, ein)
    m2 = jnp.max(e2, axis=0, keepdims=True)
    i2 = jnp.min(jnp.where(e2 == m2, io8, 8.0), axis=0, keepdims=True)
    t2 = jnp.exp(m2 - m1)
    w1 = 1.0 / (1.0 + t2)
    wt0, wt1 = g_w * w1, g_w * (t2 * w1)
    eid0, eid1 = gsel * 8.0 + i1, gsel * 8.0 + i2

    io32 = lax.broadcasted_iota(jnp.int32, (N_EXPERTS, ts), 0).astype(f32)
    oh0, oh1 = io32 == eid0, io32 == eid1
    both = jnp.where(oh0, 1.0, jnp.where(oh1, 1.0, 0.0))
    pref = jnp.dot(both.astype(bf16), tri_ref[...], preferred_element_type=f32)
    yield
    gid = lax.broadcasted_iota(jnp.int32, (N_EXPERTS, ts), 1) // GROUP
    units = jnp.zeros((N_EXPERTS, ts), f32)
    for gi in range(GROUPS_PER_STEP):
        n_e = jnp.sum(jnp.where(gid == gi, both, 0.0), axis=1, keepdims=True)
        cnt_ref[gi] = jnp.broadcast_to(n_e, (N_EXPERTS, LANES))
        units = jnp.where(gid == gi, jnp.floor((n_e + (ROW_ALIGN - 1.0)) * (1.0 / ROW_ALIGN)), units)
    units = jnp.concatenate([units, jnp.zeros((LANES - N_EXPERTS, ts), f32)], axis=0).astype(bf16)
    tot = ROW_ALIGN * jnp.dot(ltri_ref[...], units, preferred_element_type=f32) + pref
    yield
    slot0 = jnp.sum(jnp.where(oh0, tot, 0.0), axis=0, keepdims=True)
    slot1 = jnp.sum(jnp.where(oh1, tot, 0.0), axis=0, keepdims=True)

    io128 = lax.broadcasted_iota(jnp.int32, (LANES, ts), 0)
    rw_ref[...] = jnp.where(io128 == 0, wt0, jnp.where(io128 == 1, wt1,
                            jnp.where(io128 == 2, slot0, jnp.where(io128 == 3, slot1, 0.0)))).T

    s0i, s1i = slot0.astype(jnp.int32), slot1.astype(jnp.int32)
    io = lax.broadcasted_iota(jnp.int32, (BLOCK_ROWS, GROUP), 0)
    for gi in range(GROUPS_PER_STEP):
        tok = slice(gi * GROUP, (gi + 1) * GROUP)
        onehot = jnp.where(io == s0i[:, tok], 1.0, jnp.where(io == s1i[:, tok], 1.0, 0.0)).astype(bf16)
        rows = jnp.dot(onehot, xnb[tok, :], preferred_element_type=f32)
        _store_rows(xs_ref, gi * BLOCK_ROWS, _pack_bf16_pair(rows[:, 0:PACK_COLS], rows[:, PACK_COLS:]))
        yield


def _mixer_kernel(x_ref, xp_ref, g1_ref, win_ref, rc_ref, rs1_ref, rs2_ref, sink_ref, lng_ref, lnb_ref,
                  wcat_ref, bsf_ref, wab_ref, wgb_ref, wout_ref, g2_ref, wrt_ref, brt_ref, tri_ref, ltri_ref,
                  x1_ref, xs_ref, rw_ref, cnt_ref,
                  kl_ref, vt_ref, attnt_ref, gated_ref, mix_ref, *, blocks_per_seq, n_blocks):
    step = pl.program_id(0)
    s_blk = jnp.minimum(step, n_blocks - 1) % blocks_per_seq
    cur = step % 2
    ts = x_ref.shape[0]
    nb = ts // BLOCK
    f32, bf16 = jnp.float32, jnp.bfloat16

    @pl.when(step == 0)
    def _():
        mix_ref[1] = jnp.zeros((ts, D_MODEL), bf16)

    @pl.when(s_blk == 0)
    def _():
        kl_ref[:, 0:BLOCK, :] = jnp.zeros((4, BLOCK, LANES), bf16)
        vt_ref[:, 0:BLOCK] = jnp.zeros((LANES, BLOCK), bf16)

    x1 = xp_ref[...] + jnp.dot(mix_ref[1 - cur], wout_ref[...], preferred_element_type=f32)
    x1_ref[...] = x1
    route = _route_and_compact(_rms(x1, g2_ref[...]).astype(bf16), wrt_ref, brt_ref, tri_ref, ltri_ref,
                               xs_ref, rw_ref, cnt_ref)
    step_route = lambda: next(route, None)

    x = x_ref[...]
    hb = _rms(x, g1_ref[...]).astype(bf16)

    qkv = jnp.dot(hb, win_ref[:, 0:V_END], preferred_element_type=f32)
    rc, rs1, rs2 = rc_ref[...], rs1_ref[...], rs2_ref[...]

    def rope(t):
        return t * rc + pltpu.roll(t, 8, 1) * rs1 + pltpu.roll(t, LANES - 8, 1) * rs2

    scale = 1.0 / math.sqrt(HEAD_DIM)
    qb = [(rope(qkv[:, j * LANES:(j + 1) * LANES]) * scale).astype(bf16) for j in range(4)]
    k = rope(qkv[:, Q_END:K_END])
    v = qkv[:, K_END:V_END]
    k_sw = pltpu.roll(k, HEAD_DIM, 1)
    lo = lax.broadcasted_iota(jnp.int32, (ts, LANES), 1) < HEAD_DIM
    zero = jnp.zeros_like(k)
    kl_ref[0, BLOCK:BLOCK + ts, :] = jnp.where(lo, k, zero).astype(bf16)
    kl_ref[1, BLOCK:BLOCK + ts, :] = jnp.where(lo, zero, k_sw).astype(bf16)
    kl_ref[2, BLOCK:BLOCK + ts, :] = jnp.where(lo, k_sw, zero).astype(bf16)
    kl_ref[3, BLOCK:BLOCK + ts, :] = jnp.where(lo, zero, k).astype(bf16)
    vt_ref[:, BLOCK:BLOCK + ts] = v.T.astype(bf16)

    kj = lax.broadcasted_iota(jnp.int32, (2 * BLOCK, 2 * BLOCK), 0)
    qi = lax.broadcasted_iota(jnp.int32, (2 * BLOCK, 2 * BLOCK), 1) % BLOCK
    band = (kj > qi) & (kj <= qi + BLOCK)
    kmin = jnp.where(s_blk == 0, BLOCK, 0)
    bias_first = jnp.where(band & (kj >= kmin), 0.0, NEG_INF).astype(f32)
    bias_rest = jnp.where(band, 0.0, NEG_INF).astype(f32)
    items = [(i, g, p) for i in range(nb) for g in range(N_KV_HEADS) for p in range(2)]

    def scores(n):
        i, g, p = items[n]
        rows = slice(i * BLOCK, (i + 1) * BLOCK)
        qs = jnp.concatenate([qb[2 * g][rows], qb[2 * g + 1][rows]], axis=0)
        st = lax.dot_general(kl_ref[2 * g + p, i * BLOCK:(i + 2) * BLOCK, :], qs, (((1,), (1,)), ((), ())),
                             preferred_element_type=f32)
        return st + (bias_first if i == 0 else bias_rest)

    def finish(n, st):
        i, g, p = items[n]
        rows = slice(i * BLOCK, (i + 1) * BLOCK)
        sink = sink_ref[2 * g + p:2 * g + p + 1, :]
        m = jnp.maximum(jnp.max(st, axis=0, keepdims=True), sink)
        e = jnp.exp(st - m)
        den = jnp.sum(e, axis=0, keepdims=True) + jnp.exp(sink - m)
        ot = jnp.dot(vt_ref[g * HEAD_DIM:(g + 1) * HEAD_DIM, i * BLOCK:(i + 2) * BLOCK], e.astype(bf16),
                     preferred_element_type=f32) * (1.0 / den)
        h0 = 4 * g + p
        attnt_ref[h0 * HEAD_DIM:(h0 + 1) * HEAD_DIM, rows] = ot[:, 0:BLOCK]
        attnt_ref[(h0 + 2) * HEAD_DIM:(h0 + 3) * HEAD_DIM, rows] = ot[:, BLOCK:2 * BLOCK]

    n_chunks = D_MODEL // CHUNK_COLS
    zc, sga, sgg, gmc = [None] * n_chunks, [None] * n_chunks, [None] * n_chunks, [None] * n_chunks
    vn_box = []

    def proj(lo_col, c):
        cols = slice(lo_col + c * CHUNK_COLS, lo_col + (c + 1) * CHUNK_COLS)
        return jnp.dot(hb, win_ref[:, cols], preferred_element_type=f32)

    def z_chunk(c):
        zc[c] = jax.nn.gelu(proj(V_END, c))
        if c == n_chunks - 1:
            v2 = jnp.concatenate(zc[n_chunks // 2:], axis=1)
            mu = jnp.mean(v2, axis=-1, keepdims=True)
            vc = v2 - mu
            var = jnp.mean(vc * vc, axis=-1, keepdims=True)
            vn_box.append((vc * lax.rsqrt(var + EPS) * lng_ref[...] + lnb_ref[...]).astype(bf16))

    def ga_chunk(c):
        sga[c] = jax.nn.sigmoid(proj(Z_END, c))

    def gg_chunk(c):
        sgg[c] = jax.nn.sigmoid(proj(GA_END, c))

    wq = lax.broadcasted_iota(jnp.int32, (BLOCK, 2 * BLOCK), 0)
    wp = lax.broadcasted_iota(jnp.int32, (BLOCK, 2 * BLOCK), 1) % BLOCK
    causal = wp <= wq
    lo_c = lax.broadcasted_iota(jnp.int32, (BLOCK, LANES), 1) < G_GROUP_DIM
    wjs = {}

    def gmlp_dot(j, c):
        if j not in wjs:
            wjs[j] = jnp.where(causal, wcat_ref[j], 0.0).astype(bf16)
        cols = slice(j * LANES, (j + 1) * LANES)
        rhs = []
        for cc in (c, c + 1):
            vp = vn_box[0][cc * BLOCK:(cc + 1) * BLOCK, cols]
            zb = jnp.zeros_like(vp)
            rhs.append(jnp.concatenate([jnp.where(lo_c, vp, zb), jnp.where(lo_c, zb, vp)], axis=0))
        mixed = jnp.dot(wjs[j], jnp.concatenate(rhs, axis=1), preferred_element_type=f32)
        u = zc[j // 2][:, (j % 2) * LANES:(j % 2 + 1) * LANES]
        for q, cc in enumerate((c, c + 1)):
            rows = slice(cc * BLOCK, (cc + 1) * BLOCK)
            gated_ref[rows, cols] = u[rows] * (mixed[:, q * LANES:(q + 1) * LANES] + bsf_ref[:, cols])

    def gm_chunk(c):
        cols = slice(c * CHUNK_COLS, (c + 1) * CHUNK_COLS)
        gmc[c] = jnp.dot(gated_ref[...].astype(bf16), wgb_ref[:, cols], preferred_element_type=f32)

    fillers = [[functools.partial(z_chunk, c)] for c in range(n_chunks)]
    fillers += [[functools.partial(ga_chunk, c)] for c in range(n_chunks)]
    fillers += [[functools.partial(gg_chunk, c)] for c in range(n_chunks)]
    fillers += [[functools.partial(gm_chunk, c)] for c in range(n_chunks)]
    gd = [functools.partial(gmlp_dot, j, c) for j in range(G_GROUPS // 2) for c in range(0, nb, 2)]
    for q in range(len(gd)):
        fillers[n_chunks + q].append(gd[q])
    assert len(fillers) == len(items)

    fillers[0][0]()
    step_route()
    fillers[1][0]()
    step_route()
    step_route()
    fillers = fillers[2:] + [[], []]
    for n in ROUTE_AT:
        fillers[n].append(step_route)
    sts = {n: scores(n) for n in range(LOOKAHEAD)}
    for n in range(len(items)):
        finish(n, sts.pop(n))
        if n + LOOKAHEAD < len(items):
            sts[n + LOOKAHEAD] = scores(n + LOOKAHEAD)
        for f in fillers[n]:
            f()
    kl_ref[:, 0:BLOCK, :] = kl_ref[:, ts:ts + BLOCK, :]
    vt_ref[:, 0:BLOCK] = vt_ref[:, ts:ts + BLOCK]
    attn_b = attnt_ref[...].T.astype(bf16)

    mix = []
    for c in range(n_chunks):
        cols = slice(c * CHUNK_COLS, (c + 1) * CHUNK_COLS)
        a_c = jnp.dot(attn_b, wab_ref[:, cols], preferred_element_type=f32)
        mix.append((sga[c] * a_c + sgg[c] * gmc[c]).astype(bf16))
    mix_ref[cur] = jnp.concatenate(mix, axis=1)
    for _ in route:
        pass


def _const_spec(shape):
    nd = len(shape)
    return pl.BlockSpec(shape, lambda i: (0,) * nd)


def _mixer(x2, g1, win, rc, rs1, rs2, sinks, lng, lnb, wcat, bsf, wab, wgb, wout, g2, wrt, brt, tri, ltri, seq):
    t = x2.shape[0]
    n_blocks = t // TS
    blocks_per_seq = seq // TS
    cur_blk = lambda i: (jnp.minimum(i, n_blocks - 1), 0)
    prev_blk = lambda i: (jnp.maximum(i - 1, 0), 0)
    pos = lambda i: (jnp.minimum(i, n_blocks - 1) % blocks_per_seq, 0)
    consts = (sinks, lng, lnb, wcat, bsf, wab, wgb, wout, g2, wrt, brt, tri, ltri)
    in_specs = [
        pl.BlockSpec((TS, D_MODEL), cur_blk),
        pl.BlockSpec((TS, D_MODEL), prev_blk),
        _const_spec(g1.shape), _const_spec(win.shape),
        pl.BlockSpec((TS, LANES), pos), pl.BlockSpec((TS, LANES), pos), pl.BlockSpec((TS, LANES), pos),
    ] + [_const_spec(c.shape) for c in consts]
    out_shape = (
        jax.ShapeDtypeStruct((t, D_MODEL), jnp.float32),
        jax.ShapeDtypeStruct((n_blocks * GROUPS_PER_STEP * BLOCK_ROWS * ROW_LINES, LANES), jnp.uint32),
        jax.ShapeDtypeStruct((t, LANES), jnp.float32),
        jax.ShapeDtypeStruct((n_blocks * GROUPS_PER_STEP, N_EXPERTS, LANES), jnp.float32),
    )
    out_specs = (
        pl.BlockSpec((TS, D_MODEL), prev_blk),
        pl.BlockSpec((GROUPS_PER_STEP * BLOCK_ROWS * ROW_LINES, LANES), prev_blk),
        pl.BlockSpec((TS, LANES), prev_blk),
        pl.BlockSpec((GROUPS_PER_STEP, N_EXPERTS, LANES), lambda i: (jnp.maximum(i - 1, 0), 0, 0)),
    )
    scratch = [
        pltpu.VMEM((4, BLOCK + TS, LANES), jnp.bfloat16),
        pltpu.VMEM((LANES, BLOCK + TS), jnp.bfloat16),
        pltpu.VMEM((ATTN_WIDTH, TS), jnp.float32),
        pltpu.VMEM((TS, G_WIDTH), jnp.float32),
        pltpu.VMEM((2, TS, D_MODEL), jnp.bfloat16),
    ]
    return pl.pallas_call(
        functools.partial(_mixer_kernel, blocks_per_seq=blocks_per_seq, n_blocks=n_blocks),
        grid=(n_blocks + 1,),
        in_specs=in_specs,
        out_specs=out_specs,
        out_shape=out_shape,
        scratch_shapes=scratch,
        compiler_params=pltpu.CompilerParams(dimension_semantics=("arbitrary",),
                                             vmem_limit_bytes=VMEM_LIMIT),
        name="mixer",
    )(x2, x2, g1, win, rc, rs1, rs2, *consts)


def _expert_kernel(te_ref, tr0_ref, tb0_ref, tb1_ref, trows_ref, nt_ref, cum_ref, src_ref, used_ref,
                   fsrc_ref, fdst_ref, fn_ref,
                   xs_ref, wg_ref, wu_ref, wd_ref, ys_ref,
                   xbuf_ref, ybuf_ref, wgb_ref, wub_ref, wdb_ref, zero_ref, isem, osem, zsem,
                   *, n_blocks):
    j = pl.program_id(0)
    n_tiles = nt_ref[0]
    slot = j % 2
    f32, bf16 = jnp.float32, jnp.bfloat16

    def lines(rows):
        return pl.multiple_of(rows * ROW_LINES, SUBLANES)

    def run_piece(tile, b, ok, fn):
        e, r0 = te_ref[tile], tr0_ref[tile]
        bb = jnp.minimum(b, n_blocks - 1)
        start = cum_ref[e * (n_blocks + 1) + bb]
        lo = jnp.maximum(start, r0)
        n = jnp.minimum(cum_ref[e * (n_blocks + 1) + bb + 1], r0 + TM) - lo

        @pl.when(ok & (b < tb1_ref[tile]) & (n > 0))
        def _():
            fn(lines(src_ref[e * n_blocks + bb] + (lo - start)), lines(lo - r0), lines(n))

    def first_runs(tile, ok, fn):
        for k in range(FAST_RUNS):
            n = fn_ref[tile * FAST_RUNS + k]

            @pl.when(ok & (n > 0))
            def _():
                fn(pl.multiple_of(fsrc_ref[tile * FAST_RUNS + k], SUBLANES),
                   pl.multiple_of(fdst_ref[tile * FAST_RUNS + k], SUBLANES), pl.multiple_of(n, SUBLANES))

    def other_runs(tile, first, fn):
        lax.fori_loop(tb0_ref[tile] + first, tb1_ref[tile], lambda b, c: (run_piece(tile, b, True, fn), c)[1], 0)

    def gather_fn(s):
        return lambda src, dst, n: pltpu.make_async_copy(
            xs_ref.at[pl.ds(src, n)], xbuf_ref.at[s, pl.ds(dst, n)], isem.at[s]).start()

    def scatter_fn(s):
        return lambda src, dst, n: pltpu.make_async_copy(
            ybuf_ref.at[s, pl.ds(dst, n)], ys_ref.at[pl.ds(src, n)], osem.at[s]).start()

    def wait_rows(sem, n):
        @pl.when(n > 0)
        def _():
            pltpu.make_async_copy(xs_ref.at[pl.ds(0, lines(n))], xbuf_ref.at[0, pl.ds(0, lines(n))], sem).wait()

    def zero_tail_copy(b, c):
        first = used_ref[b] + c * TM
        n = lines(jnp.minimum(BLOCK_ROWS - first, TM))
        return pltpu.make_async_copy(zero_ref.at[pl.ds(0, n)], ys_ref.at[pl.ds(lines(b * BLOCK_ROWS + first), n)], zsem)

    @pl.when(j == 0)
    def _():
        xbuf_ref[...] = jnp.zeros_like(xbuf_ref)
        zero_ref[...] = jnp.zeros_like(zero_ref)

        def per_block(b, carry):
            chunks = (BLOCK_ROWS - used_ref[b] + TM - 1) // TM
            lax.fori_loop(0, chunks, lambda c, x: (zero_tail_copy(b, c).start(), x)[1], 0)
            lax.fori_loop(0, chunks, lambda c, x: (zero_tail_copy(b, c).wait(), x)[1], 0)
            return carry

        lax.fori_loop(0, n_blocks, per_block, 0)

        for t0 in range(GATHER_AHEAD):
            @pl.when(t0 < n_tiles)
            def _():
                other_runs(t0, 0, gather_fn(t0))

    prev = jnp.maximum(j - 1, 0)
    xslot = j % (GATHER_AHEAD + 1)
    nslot = (j + GATHER_AHEAD) % (GATHER_AHEAD + 1)

    @pl.when((j == n_tiles) & (j > 0))
    def _():
        other_runs(prev, 0, scatter_fn(1 - slot))

    @pl.when((j >= n_tiles) & (j >= 2) & (j - 2 < n_tiles))
    def _():
        wait_rows(osem.at[slot], trows_ref[jnp.maximum(j - 2, 0)])

    @pl.when(j < n_tiles)
    def _():
        @pl.when((j == 0) | (te_ref[j] != te_ref[prev]))
        def _():
            wgb_ref[...] = wg_ref[0].astype(bf16)
            wub_ref[...] = wu_ref[0].astype(bf16)
            wdb_ref[...] = wd_ref[0].astype(bf16)

        wait_rows(isem.at[xslot], trows_ref[j])
        xb = _unpack_bf16_rows(_load_rows(xbuf_ref.at[xslot], 0, TM))
        nxt = jnp.minimum(j + GATHER_AHEAD, n_tiles - 1)
        more = j + GATHER_AHEAD < n_tiles
        first_runs(prev, j > 0, scatter_fn(1 - slot))
        first_runs(nxt, more, gather_fn(nslot))
        half = D_EXPERT // 2
        g0 = jnp.dot(xb, wgb_ref[:, 0:half], preferred_element_type=f32)
        u0 = jnp.dot(xb, wub_ref[:, 0:half], preferred_element_type=f32)
        g1 = jnp.dot(xb, wgb_ref[:, half:], preferred_element_type=f32)
        u1 = jnp.dot(xb, wub_ref[:, half:], preferred_element_type=f32)
        y = jnp.dot((jax.nn.silu(g0) * u0).astype(bf16), wdb_ref[0:half, :], preferred_element_type=f32)
        y = y + jnp.dot((jax.nn.silu(g1) * u1).astype(bf16), wdb_ref[half:, :], preferred_element_type=f32)
        packed = _pack_bf16_pair(y[:, 0:PACK_COLS], y[:, PACK_COLS:])
        probe = xbuf_ref[xslot, pl.ds(0, SUBLANES, stride=ROW_LINES), :]

        @pl.when(j >= 2)
        def _():
            wait_rows(osem.at[slot], trows_ref[jnp.maximum(j - 2, 0)])

        _store_rows(ybuf_ref.at[slot], 0, packed)
        ybuf_ref[slot, pl.ds(0, SUBLANES, stride=ROW_LINES), :] = packed[0:SUBLANES, 0:LANES] | ((probe >> 16) >> 16)

        @pl.when(more)
        def _():
            other_runs(nxt, FAST_RUNS, gather_fn(nslot))

        @pl.when(j > 0)
        def _():
            other_runs(prev, FAST_RUNS, scatter_fn(1 - slot))


def _experts(tables, xs, wg, wu, wd, n_blocks, grid_tiles):
    wsel = lambda j, te, *_: (te[j], 0, 0)
    grid_spec = pltpu.PrefetchScalarGridSpec(
        num_scalar_prefetch=len(tables),
        grid=(grid_tiles,),
        in_specs=[
            pl.BlockSpec(memory_space=pl.ANY),
            pl.BlockSpec((1, D_MODEL, D_EXPERT), wsel),
            pl.BlockSpec((1, D_MODEL, D_EXPERT), wsel),
            pl.BlockSpec((1, D_EXPERT, D_MODEL), wsel),
        ],
        out_specs=pl.BlockSpec(memory_space=pl.ANY),
        scratch_shapes=[
            pltpu.VMEM((GATHER_AHEAD + 1, TM * ROW_LINES, LANES), jnp.uint32),
            pltpu.VMEM((2, TM * ROW_LINES, LANES), jnp.uint32),
            pltpu.VMEM((D_MODEL, D_EXPERT), jnp.bfloat16),
            pltpu.VMEM((D_MODEL, D_EXPERT), jnp.bfloat16),
            pltpu.VMEM((D_EXPERT, D_MODEL), jnp.bfloat16),
            pltpu.VMEM((TM * ROW_LINES, LANES), jnp.uint32),
            pltpu.SemaphoreType.DMA((GATHER_AHEAD + 1,)),
            pltpu.SemaphoreType.DMA((2,)),
            pltpu.SemaphoreType.DMA,
        ],
    )
    return pl.pallas_call(
        functools.partial(_expert_kernel, n_blocks=n_blocks),
        grid_spec=grid_spec,
        out_shape=jax.ShapeDtypeStruct(xs.shape, jnp.uint32),
        compiler_params=pltpu.CompilerParams(dimension_semantics=("arbitrary",),
                                             vmem_limit_bytes=VMEM_LIMIT),
        name="experts",
    )(*tables, xs, wg, wu, wd)


def _combine_kernel(x1_ref, rw_ref, gf_ref, ys_ref, o_ref):
    rw = rw_ref[...]
    k_pad = -(-BLOCK_ROWS // LANES) * LANES
    io = lax.broadcasted_iota(jnp.int32, (GROUP, k_pad), 1).astype(jnp.float32)
    parts = []
    for gi in range(GROUPS_PER_STEP):
        r = rw[gi * GROUP:(gi + 1) * GROUP]
        wsel = jnp.where(io == r[:, 2:3], r[:, 0:1], 0.0) + jnp.where(io == r[:, 3:4], r[:, 1:2], 0.0)
        ys = _unpack_bf16_rows(_load_rows(ys_ref, gi * BLOCK_ROWS, BLOCK_ROWS))
        ys = jnp.concatenate([ys, jnp.zeros((k_pad - BLOCK_ROWS, D_MODEL), jnp.bfloat16)], axis=0)
        parts.append(jnp.dot(wsel.astype(jnp.bfloat16), ys, preferred_element_type=jnp.float32))
    o_ref[...] = _rms(x1_ref[...] + jnp.concatenate(parts, axis=0), gf_ref[...])


def _combine(x1, rw, gf, ys):
    t = x1.shape[0]
    tok = lambda i: (i, 0)
    return pl.pallas_call(
        _combine_kernel,
        grid=(t // TS,),
        in_specs=[
            pl.BlockSpec((TS, D_MODEL), tok),
            pl.BlockSpec((TS, LANES), tok),
            pl.BlockSpec((1, D_MODEL), lambda i: (0, 0)),
            pl.BlockSpec((GROUPS_PER_STEP * BLOCK_ROWS * ROW_LINES, LANES), tok),
        ],
        out_specs=pl.BlockSpec((TS, D_MODEL), tok),
        out_shape=jax.ShapeDtypeStruct((t, D_MODEL), jnp.float32),
        compiler_params=pltpu.CompilerParams(dimension_semantics=("arbitrary",),
                                             vmem_limit_bytes=VMEM_LIMIT),
        name="combine",
    )(x1, rw, gf, ys)


def _rope_lane_tables(seq):
    inv_freq = ROPE_THETA ** (-jnp.arange(0, ROT_DIM, 2, dtype=jnp.float32) / ROT_DIM)
    ang = jnp.arange(seq, dtype=jnp.float32)[:, None] * inv_freq[None, :]
    cos, sin = jnp.cos(ang), jnp.sin(ang)
    half = ROT_DIM // 2
    ones = jnp.ones((seq, HEAD_DIM - ROT_DIM), jnp.float32)
    zeros = jnp.zeros((seq, HEAD_DIM - ROT_DIM), jnp.float32)
    zh = jnp.zeros((seq, half), jnp.float32)
    c = jnp.concatenate([cos, cos, ones], axis=1)
    s1 = jnp.concatenate([zh, sin, zeros], axis=1)
    s2 = jnp.concatenate([-sin, zh, zeros], axis=1)
    rep = LANES // HEAD_DIM
    return jnp.tile(c, (1, rep)), jnp.tile(s1, (1, rep)), jnp.tile(s2, (1, rep))


def _pick(table, idx):
    return jnp.sum(jnp.where(idx[:, None] == jnp.arange(table.shape[0])[None, :], table[None, :], 0), axis=1)


def _tile_tables(cnt, n_blocks, grid_tiles):
    i32 = jnp.int32
    n = cnt[:, :, 0].astype(i32)
    n8 = (n + ROW_ALIGN - 1) // ROW_ALIGN * ROW_ALIGN
    loc = jnp.cumsum(n8, axis=1) - n8
    used = jnp.sum(n8, axis=1)
    cum = jnp.concatenate([jnp.zeros((1, N_EXPERTS), i32), jnp.cumsum(n8, axis=0)], axis=0).T
    total = cum[:, -1]
    src = (jnp.arange(n_blocks, dtype=i32)[:, None] * BLOCK_ROWS + loc).T
    tiles_e = (total + TM - 1) // TM
    tile_end = jnp.cumsum(tiles_e)
    n_tiles = tile_end[-1]
    j = jnp.arange(grid_tiles, dtype=i32)
    te = jnp.minimum(jnp.sum(tile_end[None, :] <= j[:, None], axis=1), N_EXPERTS - 1).astype(i32)
    live = j < n_tiles
    te = jnp.where(live, te, _pick(te, jnp.maximum(n_tiles - 1, 0)[None])[0])
    r0 = (j - _pick(tile_end - tiles_e, te)) * TM
    cum_t = cum[te]
    b0 = jnp.sum(cum_t[:, 1:] <= r0[:, None], axis=1)
    b1 = jnp.sum(cum_t[:, :-1] < (r0 + TM)[:, None], axis=1)
    rows = jnp.clip(_pick(total, te) - r0, 0, TM)
    zero = jnp.zeros_like(j)

    def at(table, idx):
        hit = idx[:, :, None] == jnp.arange(table.shape[1])[None, None, :]
        return jnp.sum(jnp.where(hit, table[:, None, :], 0), axis=2)

    bk = b0[:, None] + jnp.arange(FAST_RUNS, dtype=i32)[None, :]
    bc = jnp.minimum(bk, n_blocks - 1)
    start, end = at(cum_t, bc), at(cum_t, bc + 1)
    lo = jnp.maximum(start, r0[:, None])
    fn = jnp.minimum(end, (r0 + TM)[:, None]) - lo
    fn = jnp.where(live[:, None] & (bk < b1[:, None]) & (fn > 0), fn, 0)
    fsrc = at(src[te], bc) + lo - start
    fdst = lo - r0[:, None]
    fast = tuple((v * ROW_LINES).reshape(-1).astype(i32) for v in (fsrc, fdst, fn))
    return (te, jnp.where(live, r0, zero), jnp.where(live, b0, zero).astype(i32),
            jnp.where(live, b1, zero).astype(i32), jnp.where(live, rows, zero).astype(i32),
            n_tiles.reshape(1).astype(i32), cum.reshape(-1), src.reshape(-1).astype(i32), used.astype(i32)) + fast


def kernel(x, norm1_g, w_in, attn_sinks, gmlp_ln_g, gmlp_ln_b, gmlp_ws, gmlp_bs, w_attn_branch,
           w_gmlp_branch, w_out, norm2_g, router_group_w, router_group_b, router_expert_w,
           router_expert_b, expert_w_gate, expert_w_up, expert_w_down, final_norm_g):
    b, s, d = x.shape
    assert d == D_MODEL and s % TS == 0 and norm1_g.shape[0] == 1
    t = b * s
    n_blocks = t // TS
    bf16, f32 = jnp.bfloat16, jnp.float32
    x2 = x.reshape(t, d)

    rc, rs1, rs2 = _rope_lane_tables(s)
    sk = attn_sinks[0].astype(f32)
    sinks = jnp.stack([
        jnp.concatenate([jnp.full((BLOCK,), 1.0, f32) * sk[4 * g + p], jnp.full((BLOCK,), 1.0, f32) * sk[4 * g + 2 + p]])
        for g in range(N_KV_HEADS) for p in range(2)])
    ws = gmlp_ws[0]
    wcat = jnp.stack([jnp.concatenate([ws[2 * j], ws[2 * j + 1]], axis=1) for j in range(G_GROUPS // 2)])
    bsf = jnp.repeat(gmlp_bs[0].T, G_GROUP_DIM, axis=1)
    wrt = jnp.concatenate([router_group_w[0].T, jnp.zeros((8 - N_GROUPS, d), f32), router_expert_w[0].T], axis=0)
    brt = jnp.concatenate([router_group_b[0], jnp.zeros((8 - N_GROUPS,), f32), router_expert_b[0]])
    brt = jnp.broadcast_to(brt[:, None], (ROUTER_ROWS, TS))
    tok_ids = jnp.arange(TS)
    tri = ((tok_ids[:, None] < tok_ids[None, :]) & (tok_ids[:, None] // GROUP == tok_ids[None, :] // GROUP)
           ).astype(bf16)
    ltri = (jnp.arange(LANES)[None, :] < jnp.arange(N_EXPERTS)[:, None]).astype(bf16)

    x1, xs, rw, cnt = _mixer(
        x2, norm1_g, w_in[0].astype(bf16), rc, rs1, rs2, sinks, gmlp_ln_g, gmlp_ln_b, wcat, bsf,
        w_attn_branch[0].astype(bf16), w_gmlp_branch[0].astype(bf16), w_out[0].astype(bf16),
        norm2_g, wrt.astype(bf16), brt, tri, ltri, s)

    n_groups = n_blocks * GROUPS_PER_STEP
    max_tiles = (TOP_K * t + n_groups * N_EXPERTS * (ROW_ALIGN - 1)) // TM + N_EXPERTS
    grid_tiles = max_tiles + 2
    tables = _tile_tables(cnt, n_groups, grid_tiles)
    ys = _experts(tables, xs, expert_w_gate[0], expert_w_up[0], expert_w_down[0], n_groups, grid_tiles)
    out = _combine(x1, rw, final_norm_g.reshape(1, d), ys)
    return out.reshape(b, s, d)
```

```python
import functools
import math

import jax
import jax.numpy as jnp
from jax import lax
from jax.experimental import pallas as pl
from jax.experimental.pallas import tpu as pltpu

D_MODEL = 1024
HEAD_DIM = 64
N_HEADS = 8
N_KV_HEADS = 2
BLOCK = 128
ROT_DIM = HEAD_DIM // 4
ROPE_THETA = 500000.0
ATTN_WIDTH = N_HEADS * HEAD_DIM
KV_WIDTH = N_KV_HEADS * HEAD_DIM
G_GROUPS = 8
G_GROUP_DIM = 64
G_WIDTH = G_GROUPS * G_GROUP_DIM
Q_END = ATTN_WIDTH
K_END = Q_END + KV_WIDTH
V_END = K_END + KV_WIDTH
Z_END = V_END + 2 * G_WIDTH
GA_END = Z_END + D_MODEL
IN_COLS = GA_END + D_MODEL
N_GROUPS = 4
EXPERTS_PER_GROUP = 8
N_EXPERTS = N_GROUPS * EXPERTS_PER_GROUP
TOP_K = 2
D_EXPERT = 512
EPS = 1e-5
NEG_INF = -1e30

LANES = 128
SUBLANES = 8
ROUTER_ROWS = 8 + N_EXPERTS
PACK_COLS = D_MODEL // 2
ROW_LINES = PACK_COLS // LANES
ROW_ALIGN = SUBLANES // ROW_LINES

TS = 512
TM = 512
GATHER_AHEAD = 2
FAST_RUNS = 36
GROUP = 256
CHUNK_COLS = 256
LOOKAHEAD = 3
ROUTE_AT = (0, 1)
GROUPS_PER_STEP = TS // GROUP
BLOCK_ROWS = TOP_K * GROUP + N_EXPERTS * (ROW_ALIGN - 1)
VMEM_LIMIT = 58 * 1024 * 1024


def _rms(x, g):
    return x * lax.rsqrt(jnp.mean(x * x, axis=-1, keepdims=True) + EPS) * g


def _pack_bf16_pair(a, b):
    ua = lax.bitcast_convert_type(a.astype(jnp.bfloat16).astype(jnp.float32), jnp.uint32)
    ub = lax.bitcast_convert_type(b.astype(jnp.bfloat16).astype(jnp.float32), jnp.uint32)
    return ub | (ua >> 16)


def _unpack_bf16_rows(w):
    lo = lax.bitcast_convert_type(w << 16, jnp.float32)
    hi = lax.bitcast_convert_type(w & jnp.uint32(0xFFFF0000), jnp.float32)
    return jnp.concatenate([lo.astype(jnp.bfloat16), hi.astype(jnp.bfloat16)], axis=1)


def _load_rows(lines_ref, first_row, rows):
    return jnp.concatenate([lines_ref[pl.ds(first_row * ROW_LINES + c, rows, stride=ROW_LINES), :]
                            for c in range(ROW_LINES)], axis=1)


def _store_rows(lines_ref, first_row, packed):
    for c in range(ROW_LINES):
        lines_ref[pl.ds(first_row * ROW_LINES + c, packed.shape[0], stride=ROW_LINES), :] = \
            packed[:, c * LANES:(c + 1) * LANES]


def _route_and_compact(xnb, wrt_ref, brt_ref, tri_ref, ltri_ref, xs_ref, rw_ref, cnt_ref):
    ts = xnb.shape[0]
    f32, bf16 = jnp.float32, jnp.bfloat16
    lt = lax.dot_general(wrt_ref[...], xnb, (((1,), (1,)), ((), ())),
                         preferred_element_type=f32) + brt_ref[...]
    yield
    r = [lt[i:i + 1, :] for i in range(N_GROUPS)]
    gm_ = jnp.maximum(jnp.maximum(r[0], r[1]), jnp.maximum(r[2], r[3]))
    gsel = jnp.where(r[0] == gm_, 0.0, jnp.where(r[1] == gm_, 1.0, jnp.where(r[2] == gm_, 2.0, 3.0)))
    g_w = 1.0 / (jnp.exp(r[0] - gm_) + jnp.exp(r[1] - gm_) + jnp.exp(r[2] - gm_) + jnp.exp(r[3] - gm_))
    eg = [lt[8 + 8 * kk:16 + 8 * kk, :] for kk in range(N_GROUPS)]
    ein = jnp.where(gsel == 0.0, eg[0], jnp.where(gsel == 1.0, eg[1], jnp.where(gsel == 2.0, eg[2], eg[3])))
    io8 = lax.broadcasted_iota(jnp.int32, (EXPERTS_PER_GROUP, ts), 0).astype(f32)
    m1 = jnp.max(ein, axis=0, keepdims=True)
    i1 = jnp.min(jnp.where(ein == m1, io8, 8.0), axis=0, keepdims=True)
    e2 = jnp.where(io8 == i1, NEG_INF, ein)
    m2 = jnp.max(e2, axis=0, keepdims=True)
    i2 = jnp.min(jnp.where(e2 == m2, io8, 8.0), axis=0, keepdims=True)
    t2 = jnp.exp(m2 - m1)
    w1 = 1.0 / (1.0 + t2)
    wt0, wt1 = g_w * w1, g_w * (t2 * w1)
    eid0, eid1 = gsel * 8.0 + i1, gsel * 8.0 + i2

    io32 = lax.broadcasted_iota(jnp.int32, (N_EXPERTS, ts), 0).astype(f32)
    oh0, oh1 = io32 == eid0, io32 == eid1
    both = jnp.where(oh0, 1.0, jnp.where(oh1, 1.0, 0.0))
    pref = jnp.dot(both.astype(bf16), tri_ref[...], preferred_element_type=f32)
    yield
    gid = lax.broadcasted_iota(jnp.int32, (N_EXPERTS, ts), 1) // GROUP
    units = jnp.zeros((N_EXPERTS, ts), f32)
    for gi in range(GROUPS_PER_STEP):
        n_e = jnp.sum(jnp.where(gid == gi, both, 0.0), axis=1, keepdims=True)
        cnt_ref[gi] = jnp.broadcast_to(n_e, (N_EXPERTS, LANES))
        units = jnp.where(gid == gi, jnp.floor((n_e + (ROW_ALIGN - 1.0)) * (1.0 / ROW_ALIGN)), units)
    units = jnp.concatenate([units, jnp.zeros((LANES - N_EXPERTS, ts), f32)], axis=0).astype(bf16)
    tot = ROW_ALIGN * jnp.dot(ltri_ref[...], units, preferred_element_type=f32) + pref
    yield
    slot0 = jnp.sum(jnp.where(oh0, tot, 0.0), axis=0, keepdims=True)
    slot1 = jnp.sum(jnp.where(oh1, tot, 0.0), axis=0, keepdims=True)

    io128 = lax.broadcasted_iota(jnp.int32, (LANES, ts), 0)
    rw_ref[...] = jnp.where(io128 == 0, wt0, jnp.where(io128 == 1, wt1,
                            jnp.where(io128 == 2, slot0, jnp.where(io128 == 3, slot1, 0.0)))).T

    s0i, s1i = slot0.astype(jnp.int32), slot1.astype(jnp.int32)
    io = lax.broadcasted_iota(jnp.int32, (BLOCK_ROWS, GROUP), 0)
    for gi in range(GROUPS_PER_STEP):
        tok = slice(gi * GROUP, (gi + 1) * GROUP)
        onehot = jnp.where(io == s0i[:, tok], 1.0, jnp.where(io == s1i[:, tok], 1.0, 0.0)).astype(bf16)
        rows = jnp.dot(onehot, xnb[tok, :], preferred_element_type=f32)
        _store_rows(xs_ref, gi * BLOCK_ROWS, _pack_bf16_pair(rows[:, 0:PACK_COLS], rows[:, PACK_COLS:]))
        yield


def _mixer_kernel(x_ref, xp_ref, g1_ref, win_ref, rc_ref, rs1_ref, rs2_ref, sink_ref, lng_ref, lnb_ref,
                  wcat_ref, bsf_ref, wab_ref, wgb_ref, wout_ref, g2_ref, wrt_ref, brt_ref, tri_ref, ltri_ref,
                  x1_ref, xs_ref, rw_ref, cnt_ref,
                  kl_ref, vt_ref, attnt_ref, gated_ref, mix_ref, *, blocks_per_seq, n_blocks):
    step = pl.program_id(0)
    s_blk = jnp.minimum(step, n_blocks - 1) % blocks_per_seq
    cur = step % 2
    ts = x_ref.shape[0]
    nb = ts // BLOCK
    f32, bf16 = jnp.float32, jnp.bfloat16

    @pl.when(step == 0)
    def _():
        mix_ref[1] = jnp.zeros((ts, D_MODEL), bf16)

    @pl.when(s_blk == 0)
    def _():
        kl_ref[:, 0:BLOCK, :] = jnp.zeros((4, BLOCK, LANES), bf16)
        vt_ref[:, 0:BLOCK] = jnp.zeros((LANES, BLOCK), bf16)

    x1 = xp_ref[...] + jnp.dot(mix_ref[1 - cur], wout_ref[...], preferred_element_type=f32)
    x1_ref[...] = x1
    route = _route_and_compact(_rms(x1, g2_ref[...]).astype(bf16), wrt_ref, brt_ref, tri_ref, ltri_ref,
                               xs_ref, rw_ref, cnt_ref)
    step_route = lambda: next(route, None)

    x = x_ref[...]
    hb = _rms(x, g1_ref[...]).astype(bf16)

    qkv = jnp.dot(hb, win_ref[:, 0:V_END], preferred_element_type=f32)
    rc, rs1, rs2 = rc_ref[...], rs1_ref[...], rs2_ref[...]

    def rope(t):
        return t * rc + pltpu.roll(t, 8, 1) * rs1 + pltpu.roll(t, LANES - 8, 1) * rs2

    scale = 1.0 / math.sqrt(HEAD_DIM)
    qb = [(rope(qkv[:, j * LANES:(j + 1) * LANES]) * scale).astype(bf16) for j in range(4)]
    k = rope(qkv[:, Q_END:K_END])
    v = qkv[:, K_END:V_END]
    k_sw = pltpu.roll(k, HEAD_DIM, 1)
    lo = lax.broadcasted_iota(jnp.int32, (ts, LANES), 1) < HEAD_DIM
    zero = jnp.zeros_like(k)
    kl_ref[0, BLOCK:BLOCK + ts, :] = jnp.where(lo, k, zero).astype(bf16)
    kl_ref[1, BLOCK:BLOCK + ts, :] = jnp.where(lo, zero, k_sw).astype(bf16)
    kl_ref[2, BLOCK:BLOCK + ts, :] = jnp.where(lo, k_sw, zero).astype(bf16)
    kl_ref[3, BLOCK:BLOCK + ts, :] = jnp.where(lo, zero, k).astype(bf16)
    vt_ref[:, BLOCK:BLOCK + ts] = v.T.astype(bf16)

    kj = lax.broadcasted_iota(jnp.int32, (2 * BLOCK, 2 * BLOCK), 0)
    qi = lax.broadcasted_iota(jnp.int32, (2 * BLOCK, 2 * BLOCK), 1) % BLOCK
    band = (kj > qi) & (kj <= qi + BLOCK)
    kmin = jnp.where(s_blk == 0, BLOCK, 0)
    bias_first = jnp.where(band & (kj >= kmin), 0.0, NEG_INF).astype(f32)
    bias_rest = jnp.where(band, 0.0, NEG_INF).astype(f32)
    items = [(i, g, p) for i in range(nb) for g in range(N_KV_HEADS) for p in range(2)]

    def scores(n):
        i, g, p = items[n]
        rows = slice(i * BLOCK, (i + 1) * BLOCK)
        qs = jnp.concatenate([qb[2 * g][rows], qb[2 * g + 1][rows]], axis=0)
        st = lax.dot_general(kl_ref[2 * g + p, i * BLOCK:(i + 2) * BLOCK, :], qs, (((1,), (1,)), ((), ())),
                             preferred_element_type=f32)
        return st + (bias_first if i == 0 else bias_rest)

    def finish(n, st):
        i, g, p = items[n]
        rows = slice(i * BLOCK, (i + 1) * BLOCK)
        sink = sink_ref[2 * g + p:2 * g + p + 1, :]
        m = jnp.maximum(jnp.max(st, axis=0, keepdims=True), sink)
        e = jnp.exp(st - m)
        den = jnp.sum(e, axis=0, keepdims=True) + jnp.exp(sink - m)
        ot = jnp.dot(vt_ref[g * HEAD_DIM:(g + 1) * HEAD_DIM, i * BLOCK:(i + 2) * BLOCK], e.astype(bf16),
                     preferred_element_type=f32) * (1.0 / den)
        h0 = 4 * g + p
        attnt_ref[h0 * HEAD_DIM:(h0 + 1) * HEAD_DIM, rows] = ot[:, 0:BLOCK]
        attnt_ref[(h0 + 2) * HEAD_DIM:(h0 + 3) * HEAD_DIM, rows] = ot[:, BLOCK:2 * BLOCK]

    n_chunks = D_MODEL // CHUNK_COLS
    zc, sga, sgg, gmc = [None] * n_chunks, [None] * n_chunks, [None] * n_chunks, [None] * n_chunks
    vn_box = []

    def proj(lo_col, c):
        cols = slice(lo_col + c * CHUNK_COLS, lo_col + (c + 1) * CHUNK_COLS)
        return jnp.dot(hb, win_ref[:, cols], preferred_element_type=f32)

    def z_chunk(c):
        zc[c] = jax.nn.gelu(proj(V_END, c))
        if c == n_chunks - 1:
            v2 = jnp.concatenate(zc[n_chunks // 2:], axis=1)
            mu = jnp.mean(v2, axis=-1, keepdims=True)
            vc = v2 - mu
            var = jnp.mean(vc * vc, axis=-1, keepdims=True)
            vn_box.append((vc * lax.rsqrt(var + EPS) * lng_ref[...] + lnb_ref[...]).astype(bf16))

    def ga_chunk(c):
        sga[c] = jax.nn.sigmoid(proj(Z_END, c))

    def gg_chunk(c):
        sgg[c] = jax.nn.sigmoid(proj(GA_END, c))

    wq = lax.broadcasted_iota(jnp.int32, (BLOCK, 2 * BLOCK), 0)
    wp = lax.broadcasted_iota(jnp.int32, (BLOCK, 2 * BLOCK), 1) % BLOCK
    causal = wp <= wq
    lo_c = lax.broadcasted_iota(jnp.int32, (BLOCK, LANES), 1) < G_GROUP_DIM
    wjs = {}

    def gmlp_dot(j, c):
        if j not in wjs:
            wjs[j] = jnp.where(causal, wcat_ref[j], 0.0).astype(bf16)
        cols = slice(j * LANES, (j + 1) * LANES)
        rhs = []
        for cc in (c, c + 1):
            vp = vn_box[0][cc * BLOCK:(cc + 1) * BLOCK, cols]
            zb = jnp.zeros_like(vp)
            rhs.append(jnp.concatenate([jnp.where(lo_c, vp, zb), jnp.where(lo_c, zb, vp)], axis=0))
        mixed = jnp.dot(wjs[j], jnp.concatenate(rhs, axis=1), preferred_element_type=f32)
        u = zc[j // 2][:, (j % 2) * LANES:(j % 2 + 1) * LANES]
        for q, cc in enumerate((c, c + 1)):
            rows = slice(cc * BLOCK, (cc + 1) * BLOCK)
            gated_ref[rows, cols] = u[rows] * (mixed[:, q * LANES:(q + 1) * LANES] + bsf_ref[:, cols])

    def gm_chunk(c):
        cols = slice(c * CHUNK_COLS, (c + 1) * CHUNK_COLS)
        gmc[c] = jnp.dot(gated_ref[...].astype(bf16), wgb_ref[:, cols], preferred_element_type=f32)

    fillers = [[functools.partial(z_chunk, c)] for c in range(n_chunks)]
    fillers += [[functools.partial(ga_chunk, c)] for c in range(n_chunks)]
    fillers += [[functools.partial(gg_chunk, c)] for c in range(n_chunks)]
    fillers += [[functools.partial(gm_chunk, c)] for c in range(n_chunks)]
    gd = [functools.partial(gmlp_dot, j, c) for j in range(G_GROUPS // 2) for c in range(0, nb, 2)]
    for q in range(len(gd)):
        fillers[n_chunks + q].append(gd[q])
    assert len(fillers) == len(items)

    fillers[0][0]()
    step_route()
    fillers[1][0]()
    step_route()
    step_route()
    fillers = fillers[2:] + [[], []]
    for n in ROUTE_AT:
        fillers[n].append(step_route)
    sts = {n: scores(n) for n in range(LOOKAHEAD)}
    for n in range(len(items)):
        finish(n, sts.pop(n))
        if n + LOOKAHEAD < len(items):
            sts[n + LOOKAHEAD] = scores(n + LOOKAHEAD)
        for f in fillers[n]:
            f()
    kl_ref[:, 0:BLOCK, :] = kl_ref[:, ts:ts + BLOCK, :]
    vt_ref[:, 0:BLOCK] = vt_ref[:, ts:ts + BLOCK]
    attn_b = attnt_ref[...].T.astype(bf16)

    mix = []
    for c in range(n_chunks):
        cols = slice(c * CHUNK_COLS, (c + 1) * CHUNK_COLS)
        a_c = jnp.dot(attn_b, wab_ref[:, cols], preferred_element_type=f32)
        mix.append((sga[c] * a_c + sgg[c] * gmc[c]).astype(bf16))
    mix_ref[cur] = jnp.concatenate(mix, axis=1)
    for _ in route:
        pass


def _const_spec(shape):
    nd = len(shape)
    return pl.BlockSpec(shape, lambda i: (0,) * nd)


def _mixer(x2, g1, win, rc, rs1, rs2, sinks, lng, lnb, wcat, bsf, wab, wgb, wout, g2, wrt, brt, tri, ltri, seq):
    t = x2.shape[0]
    n_blocks = t // TS
    blocks_per_seq = seq // TS
    cur_blk = lambda i: (jnp.minimum(i, n_blocks - 1), 0)
    prev_blk = lambda i: (jnp.maximum(i - 1, 0), 0)
    pos = lambda i: (jnp.minimum(i, n_blocks - 1) % blocks_per_seq, 0)
    consts = (sinks, lng, lnb, wcat, bsf, wab, wgb, wout, g2, wrt, brt, tri, ltri)
    in_specs = [
        pl.BlockSpec((TS, D_MODEL), cur_blk),
        pl.BlockSpec((TS, D_MODEL), prev_blk),
        _const_spec(g1.shape), _const_spec(win.shape),
        pl.BlockSpec((TS, LANES), pos), pl.BlockSpec((TS, LANES), pos), pl.BlockSpec((TS, LANES), pos),
    ] + [_const_spec(c.shape) for c in consts]
    out_shape = (
        jax.ShapeDtypeStruct((t, D_MODEL), jnp.float32),
        jax.ShapeDtypeStruct((n_blocks * GROUPS_PER_STEP * BLOCK_ROWS * ROW_LINES, LANES), jnp.uint32),
        jax.ShapeDtypeStruct((t, LANES), jnp.float32),
        jax.ShapeDtypeStruct((n_blocks * GROUPS_PER_STEP, N_EXPERTS, LANES), jnp.float32),
    )
    out_specs = (
        pl.BlockSpec((TS, D_MODEL), prev_blk),
        pl.BlockSpec((GROUPS_PER_STEP * BLOCK_ROWS * ROW_LINES, LANES), prev_blk),
        pl.BlockSpec((TS, LANES), prev_blk),
        pl.BlockSpec((GROUPS_PER_STEP, N_EXPERTS, LANES), lambda i: (jnp.maximum(i - 1, 0), 0, 0)),
    )
    scratch = [
        pltpu.VMEM((4, BLOCK + TS, LANES), jnp.bfloat16),
        pltpu.VMEM((LANES, BLOCK + TS), jnp.bfloat16),
        pltpu.VMEM((ATTN_WIDTH, TS), jnp.float32),
        pltpu.VMEM((TS, G_WIDTH), jnp.float32),
        pltpu.VMEM((2, TS, D_MODEL), jnp.bfloat16),
    ]
    return pl.pallas_call(
        functools.partial(_mixer_kernel, blocks_per_seq=blocks_per_seq, n_blocks=n_blocks),
        grid=(n_blocks + 1,),
        in_specs=in_specs,
        out_specs=out_specs,
        out_shape=out_shape,
        scratch_shapes=scratch,
        compiler_params=pltpu.CompilerParams(dimension_semantics=("arbitrary",),
                                             vmem_limit_bytes=VMEM_LIMIT),
        name="mixer",
    )(x2, x2, g1, win, rc, rs1, rs2, *consts)


def _expert_kernel(te_ref, tr0_ref, tb0_ref, tb1_ref, trows_ref, nt_ref, cum_ref, src_ref, used_ref,
                   xs_ref, wg_ref, wu_ref, wd_ref, ys_ref,
                   xbuf_ref, ybuf_ref, wgb_ref, wub_ref, wdb_ref, zero_ref, isem, osem, zsem,
                   *, n_blocks):
    j = pl.program_id(0)
    n_tiles = nt_ref[0]
    slot = j % 2
    f32, bf16 = jnp.float32, jnp.bfloat16

    def lines(rows):
        return pl.multiple_of(rows * ROW_LINES, SUBLANES)

    def run_piece(tile, b, ok, fn):
        e, r0 = te_ref[tile], tr0_ref[tile]
        bb = jnp.minimum(b, n_blocks - 1)
        start = cum_ref[e * (n_blocks + 1) + bb]
        lo = jnp.maximum(start, r0)
        n = jnp.minimum(cum_ref[e * (n_blocks + 1) + bb + 1], r0 + TM) - lo

        @pl.when(ok & (b < tb1_ref[tile]) & (n > 0))
        def _():
            fn(lines(src_ref[e * n_blocks + bb] + (lo - start)), lines(lo - r0), lines(n))

    def first_runs(tile, ok, fn):
        for k in range(FAST_RUNS):
            run_piece(tile, tb0_ref[tile] + k, ok, fn)

    def other_runs(tile, first, fn):
        lax.fori_loop(tb0_ref[tile] + first, tb1_ref[tile], lambda b, c: (run_piece(tile, b, True, fn), c)[1], 0)

    def gather_fn(s):
        return lambda src, dst, n: pltpu.make_async_copy(
            xs_ref.at[pl.ds(src, n)], xbuf_ref.at[s, pl.ds(dst, n)], isem.at[s]).start()

    def scatter_fn(s):
        return lambda src, dst, n: pltpu.make_async_copy(
            ybuf_ref.at[s, pl.ds(dst, n)], ys_ref.at[pl.ds(src, n)], osem.at[s]).start()

    def wait_rows(sem, n):
        @pl.when(n > 0)
        def _():
            pltpu.make_async_copy(xs_ref.at[pl.ds(0, lines(n))], xbuf_ref.at[0, pl.ds(0, lines(n))], sem).wait()

    def zero_tail_copy(b, c):
        first = used_ref[b] + c * TM
        n = lines(jnp.minimum(BLOCK_ROWS - first, TM))
        return pltpu.make_async_copy(zero_ref.at[pl.ds(0, n)], ys_ref.at[pl.ds(lines(b * BLOCK_ROWS + first), n)], zsem)

    @pl.when(j == 0)
    def _():
        xbuf_ref[...] = jnp.zeros_like(xbuf_ref)
        zero_ref[...] = jnp.zeros_like(zero_ref)

        def per_block(b, carry):
            chunks = (BLOCK_ROWS - used_ref[b] + TM - 1) // TM
            lax.fori_loop(0, chunks, lambda c, x: (zero_tail_copy(b, c).start(), x)[1], 0)
            lax.fori_loop(0, chunks, lambda c, x: (zero_tail_copy(b, c).wait(), x)[1], 0)
            return carry

        lax.fori_loop(0, n_blocks, per_block, 0)

        for t0 in range(GATHER_AHEAD):
            @pl.when(t0 < n_tiles)
            def _():
                other_runs(t0, 0, gather_fn(t0))

    prev = jnp.maximum(j - 1, 0)
    xslot = j % (GATHER_AHEAD + 1)
    nslot = (j + GATHER_AHEAD) % (GATHER_AHEAD + 1)

    @pl.when((j == n_tiles) & (j > 0))
    def _():
        other_runs(prev, 0, scatter_fn(1 - slot))

    @pl.when((j >= n_tiles) & (j >= 2) & (j - 2 < n_tiles))
    def _():
        wait_rows(osem.at[slot], trows_ref[jnp.maximum(j - 2, 0)])

    @pl.when(j < n_tiles)
    def _():
        @pl.when((j == 0) | (te_ref[j] != te_ref[prev]))
        def _():
            wgb_ref[...] = wg_ref[0].astype(bf16)
            wub_ref[...] = wu_ref[0].astype(bf16)
            wdb_ref[...] = wd_ref[0].astype(bf16)

        wait_rows(isem.at[xslot], trows_ref[j])
        xb = _unpack_bf16_rows(_load_rows(xbuf_ref.at[xslot], 0, TM))
        nxt = jnp.minimum(j + GATHER_AHEAD, n_tiles - 1)
        more = j + GATHER_AHEAD < n_tiles
        first_runs(prev, j > 0, scatter_fn(1 - slot))
        first_runs(nxt, more, gather_fn(nslot))
        half = D_EXPERT // 2
        g0 = jnp.dot(xb, wgb_ref[:, 0:half], preferred_element_type=f32)
        u0 = jnp.dot(xb, wub_ref[:, 0:half], preferred_element_type=f32)
        g1 = jnp.dot(xb, wgb_ref[:, half:], preferred_element_type=f32)
        u1 = jnp.dot(xb, wub_ref[:, half:], preferred_element_type=f32)
        y = jnp.dot((jax.nn.silu(g0) * u0).astype(bf16), wdb_ref[0:half, :], preferred_element_type=f32)
        y = y + jnp.dot((jax.nn.silu(g1) * u1).astype(bf16), wdb_ref[half:, :], preferred_element_type=f32)
        packed = _pack_bf16_pair(y[:, 0:PACK_COLS], y[:, PACK_COLS:])
        probe = xbuf_ref[xslot, pl.ds(0, SUBLANES, stride=ROW_LINES), :]

        @pl.when(j >= 2)
        def _():
            wait_rows(osem.at[slot], trows_ref[jnp.maximum(j - 2, 0)])

        _store_rows(ybuf_ref.at[slot], 0, packed)
        ybuf_ref[slot, pl.ds(0, SUBLANES, stride=ROW_LINES), :] = packed[0:SUBLANES, 0:LANES] | ((probe >> 16) >> 16)

        @pl.when(more)
        def _():
            other_runs(nxt, FAST_RUNS, gather_fn(nslot))

        @pl.when(j > 0)
        def _():
            other_runs(prev, FAST_RUNS, scatter_fn(1 - slot))


def _experts(tables, xs, wg, wu, wd, n_blocks, grid_tiles):
    wsel = lambda j, te, *_: (te[j], 0, 0)
    grid_spec = pltpu.PrefetchScalarGridSpec(
        num_scalar_prefetch=len(tables),
        grid=(grid_tiles,),
        in_specs=[
            pl.BlockSpec(memory_space=pl.ANY),
            pl.BlockSpec((1, D_MODEL, D_EXPERT), wsel),
            pl.BlockSpec((1, D_MODEL, D_EXPERT), wsel),
            pl.BlockSpec((1, D_EXPERT, D_MODEL), wsel),
        ],
        out_specs=pl.BlockSpec(memory_space=pl.ANY),
        scratch_shapes=[
            pltpu.VMEM((GATHER_AHEAD + 1, TM * ROW_LINES, LANES), jnp.uint32),
            pltpu.VMEM((2, TM * ROW_LINES, LANES), jnp.uint32),
            pltpu.VMEM((D_MODEL, D_EXPERT), jnp.bfloat16),
            pltpu.VMEM((D_MODEL, D_EXPERT), jnp.bfloat16),
            pltpu.VMEM((D_EXPERT, D_MODEL), jnp.bfloat16),
            pltpu.VMEM((TM * ROW_LINES, LANES), jnp.uint32),
            pltpu.SemaphoreType.DMA((GATHER_AHEAD + 1,)),
            pltpu.SemaphoreType.DMA((2,)),
            pltpu.SemaphoreType.DMA,
        ],
    )
    return pl.pallas_call(
        functools.partial(_expert_kernel, n_blocks=n_blocks),
        grid_spec=grid_spec,
        out_shape=jax.ShapeDtypeStruct(xs.shape, jnp.uint32),
        compiler_params=pltpu.CompilerParams(dimension_semantics=("arbitrary",),
                                             vmem_limit_bytes=VMEM_LIMIT),
        name="experts",
    )(*tables, xs, wg, wu, wd)


def _combine_kernel(x1_ref, rw_ref, gf_ref, ys_ref, o_ref):
    rw = rw_ref[...]
    k_pad = -(-BLOCK_ROWS // LANES) * LANES
    io = lax.broadcasted_iota(jnp.int32, (GROUP, k_pad), 1).astype(jnp.float32)
    parts = []
    for gi in range(GROUPS_PER_STEP):
        r = rw[gi * GROUP:(gi + 1) * GROUP]
        wsel = jnp.where(io == r[:, 2:3], r[:, 0:1], 0.0) + jnp.where(io == r[:, 3:4], r[:, 1:2], 0.0)
        ys = _unpack_bf16_rows(_load_rows(ys_ref, gi * BLOCK_ROWS, BLOCK_ROWS))
        ys = jnp.concatenate([ys, jnp.zeros((k_pad - BLOCK_ROWS, D_MODEL), jnp.bfloat16)], axis=0)
        parts.append(jnp.dot(wsel.astype(jnp.bfloat16), ys, preferred_element_type=jnp.float32))
    o_ref[...] = _rms(x1_ref[...] + jnp.concatenate(parts, axis=0), gf_ref[...])


def _combine(x1, rw, gf, ys):
    t = x1.shape[0]
    tok = lambda i: (i, 0)
    return pl.pallas_call(
        _combine_kernel,
        grid=(t // TS,),
        in_specs=[
            pl.BlockSpec((TS, D_MODEL), tok),
            pl.BlockSpec((TS, LANES), tok),
            pl.BlockSpec((1, D_MODEL), lambda i: (0, 0)),
            pl.BlockSpec((GROUPS_PER_STEP * BLOCK_ROWS * ROW_LINES, LANES), tok),
        ],
        out_specs=pl.BlockSpec((TS, D_MODEL), tok),
        out_shape=jax.ShapeDtypeStruct((t, D_MODEL), jnp.float32),
        compiler_params=pltpu.CompilerParams(dimension_semantics=("arbitrary",),
                                             vmem_limit_bytes=VMEM_LIMIT),
        name="combine",
    )(x1, rw, gf, ys)


def _rope_lane_tables(seq):
    inv_freq = ROPE_THETA ** (-jnp.arange(0, ROT_DIM, 2, dtype=jnp.float32) / ROT_DIM)
    ang = jnp.arange(seq, dtype=jnp.float32)[:, None] * inv_freq[None, :]
    cos, sin = jnp.cos(ang), jnp.sin(ang)
    half = ROT_DIM // 2
    ones = jnp.ones((seq, HEAD_DIM - ROT_DIM), jnp.float32)
    zeros = jnp.zeros((seq, HEAD_DIM - ROT_DIM), jnp.float32)
    zh = jnp.zeros((seq, half), jnp.float32)
    c = jnp.concatenate([cos, cos, ones], axis=1)
    s1 = jnp.concatenate([zh, sin, zeros], axis=1)
    s2 = jnp.concatenate([-sin, zh, zeros], axis=1)
    rep = LANES // HEAD_DIM
    return jnp.tile(c, (1, rep)), jnp.tile(s1, (1, rep)), jnp.tile(s2, (1, rep))


def _pick(table, idx):
    return jnp.sum(jnp.where(idx[:, None] == jnp.arange(table.shape[0])[None, :], table[None, :], 0), axis=1)


def _tile_tables(cnt, n_blocks, grid_tiles):
    i32 = jnp.int32
    n = cnt[:, :, 0].astype(i32)
    n8 = (n + ROW_ALIGN - 1) // ROW_ALIGN * ROW_ALIGN
    loc = jnp.cumsum(n8, axis=1) - n8
    used = jnp.sum(n8, axis=1)
    cum = jnp.concatenate([jnp.zeros((1, N_EXPERTS), i32), jnp.cumsum(n8, axis=0)], axis=0).T
    total = cum[:, -1]
    src = (jnp.arange(n_blocks, dtype=i32)[:, None] * BLOCK_ROWS + loc).T
    tiles_e = (total + TM - 1) // TM
    tile_end = jnp.cumsum(tiles_e)
    n_tiles = tile_end[-1]
    j = jnp.arange(grid_tiles, dtype=i32)
    te = jnp.minimum(jnp.sum(tile_end[None, :] <= j[:, None], axis=1), N_EXPERTS - 1).astype(i32)
    live = j < n_tiles
    te = jnp.where(live, te, _pick(te, jnp.maximum(n_tiles - 1, 0)[None])[0])
    r0 = (j - _pick(tile_end - tiles_e, te)) * TM
    cum_t = cum[te]
    b0 = jnp.sum(cum_t[:, 1:] <= r0[:, None], axis=1)
    b1 = jnp.sum(cum_t[:, :-1] < (r0 + TM)[:, None], axis=1)
    rows = jnp.clip(_pick(total, te) - r0, 0, TM)
    zero = jnp.zeros_like(j)
    return (te, jnp.where(live, r0, zero), jnp.where(live, b0, zero).astype(i32),
            jnp.where(live, b1, zero).astype(i32), jnp.where(live, rows, zero).astype(i32),
            n_tiles.reshape(1).astype(i32), cum.reshape(-1), src.reshape(-1).astype(i32), used.astype(i32))


def kernel(x, norm1_g, w_in, attn_sinks, gmlp_ln_g, gmlp_ln_b, gmlp_ws, gmlp_bs, w_attn_branch,
           w_gmlp_branch, w_out, norm2_g, router_group_w, router_group_b, router_expert_w,
           router_expert_b, expert_w_gate, expert_w_up, expert_w_down, final_norm_g):
    b, s, d = x.shape
    assert d == D_MODEL and s % TS == 0 and norm1_g.shape[0] == 1
    t = b * s
    n_blocks = t // TS
    bf16, f32 = jnp.bfloat16, jnp.float32
    x2 = x.reshape(t, d)

    rc, rs1, rs2 = _rope_lane_tables(s)
    sk = attn_sinks[0].astype(f32)
    sinks = jnp.stack([
        jnp.concatenate([jnp.full((BLOCK,), 1.0, f32) * sk[4 * g + p], jnp.full((BLOCK,), 1.0, f32) * sk[4 * g + 2 + p]])
        for g in range(N_KV_HEADS) for p in range(2)])
    ws = gmlp_ws[0]
    wcat = jnp.stack([jnp.concatenate([ws[2 * j], ws[2 * j + 1]], axis=1) for j in range(G_GROUPS // 2)])
    bsf = jnp.repeat(gmlp_bs[0].T, G_GROUP_DIM, axis=1)
    wrt = jnp.concatenate([router_group_w[0].T, jnp.zeros((8 - N_GROUPS, d), f32), router_expert_w[0].T], axis=0)
    brt = jnp.concatenate([router_group_b[0], jnp.zeros((8 - N_GROUPS,), f32), router_expert_b[0]])
    brt = jnp.broadcast_to(brt[:, None], (ROUTER_ROWS, TS))
    tok_ids = jnp.arange(TS)
    tri = ((tok_ids[:, None] < tok_ids[None, :]) & (tok_ids[:, None] // GROUP == tok_ids[None, :] // GROUP)
           ).astype(bf16)
    ltri = (jnp.arange(LANES)[None, :] < jnp.arange(N_EXPERTS)[:, None]).astype(bf16)

    x1, xs, rw, cnt = _mixer(
        x2, norm1_g, w_in[0].astype(bf16), rc, rs1, rs2, sinks, gmlp_ln_g, gmlp_ln_b, wcat, bsf,
        w_attn_branch[0].astype(bf16), w_gmlp_branch[0].astype(bf16), w_out[0].astype(bf16),
        norm2_g, wrt.astype(bf16), brt, tri, ltri, s)

    n_groups = n_blocks * GROUPS_PER_STEP
    max_tiles = (TOP_K * t + n_groups * N_EXPERTS * (ROW_ALIGN - 1)) // TM + N_EXPERTS
    grid_tiles = max_tiles + 2
    tables = _tile_tables(cnt, n_groups, grid_tiles)
    ys = _experts(tables, xs, expert_w_gate[0], expert_w_up[0], expert_w_down[0], n_groups, grid_tiles)
    out = _combine(x1, rw, final_norm_g.reshape(1, d), ys)
    return out.reshape(b, s, d)
```

```python
import functools
import math

import jax
import jax.numpy as jnp
from jax import lax
from jax.experimental import pallas as pl
from jax.experimental.pallas import tpu as pltpu

D_MODEL = 1024
HEAD_DIM = 64
N_HEADS = 8
N_KV_HEADS = 2
BLOCK = 128
ROT_DIM = HEAD_DIM // 4
ROPE_THETA = 500000.0
ATTN_WIDTH = N_HEADS * HEAD_DIM
KV_WIDTH = N_KV_HEADS * HEAD_DIM
G_GROUPS = 8
G_GROUP_DIM = 64
G_WIDTH = G_GROUPS * G_GROUP_DIM
Q_END = ATTN_WIDTH
K_END = Q_END + KV_WIDTH
V_END = K_END + KV_WIDTH
Z_END = V_END + 2 * G_WIDTH
GA_END = Z_END + D_MODEL
IN_COLS = GA_END + D_MODEL
N_GROUPS = 4
EXPERTS_PER_GROUP = 8
N_EXPERTS = N_GROUPS * EXPERTS_PER_GROUP
TOP_K = 2
D_EXPERT = 512
EPS = 1e-5
NEG_INF = -1e30

LANES = 128
SUBLANES = 8
ROUTER_ROWS = 8 + N_EXPERTS
PACK_COLS = D_MODEL // 2
ROW_LINES = PACK_COLS // LANES
ROW_ALIGN = SUBLANES // ROW_LINES

TS = 512
TM = 512
GATHER_AHEAD = 2
FAST_RUNS = 36
GROUP = 256
CHUNK_COLS = 256
LOOKAHEAD = 3
ROUTE_AT = (0, 1)
GROUPS_PER_STEP = TS // GROUP
BLOCK_ROWS = TOP_K * GROUP + N_EXPERTS * (ROW_ALIGN - 1)
VMEM_LIMIT = 58 * 1024 * 1024


def _rms(x, g):
    return x * lax.rsqrt(jnp.mean(x * x, axis=-1, keepdims=True) + EPS) * g


def _pack_bf16_pair(a, b):
    ua = lax.bitcast_convert_type(a.astype(jnp.bfloat16).astype(jnp.float32), jnp.uint32)
    ub = lax.bitcast_convert_type(b.astype(jnp.bfloat16).astype(jnp.float32), jnp.uint32)
    return ub | (ua >> 16)


def _unpack_bf16_rows(w):
    lo = lax.bitcast_convert_type(w << 16, jnp.float32)
    hi = lax.bitcast_convert_type(w & jnp.uint32(0xFFFF0000), jnp.float32)
    return jnp.concatenate([lo.astype(jnp.bfloat16), hi.astype(jnp.bfloat16)], axis=1)


def _load_rows(lines_ref, first_row, rows):
    return jnp.concatenate([lines_ref[pl.ds(first_row * ROW_LINES + c, rows, stride=ROW_LINES), :]
                            for c in range(ROW_LINES)], axis=1)


def _store_rows(lines_ref, first_row, packed):
    for c in range(ROW_LINES):
        lines_ref[pl.ds(first_row * ROW_LINES + c, packed.shape[0], stride=ROW_LINES), :] = \
            packed[:, c * LANES:(c + 1) * LANES]


def _route_and_compact(xnb, wrt_ref, brt_ref, tri_ref, ltri_ref, xs_ref, rw_ref, cnt_ref):
    ts = xnb.shape[0]
    f32, bf16 = jnp.float32, jnp.bfloat16
    lt = lax.dot_general(wrt_ref[...], xnb, (((1,), (1,)), ((), ())),
                         preferred_element_type=f32) + brt_ref[...]
    yield
    r = [lt[i:i + 1, :] for i in range(N_GROUPS)]
    gm_ = jnp.maximum(jnp.maximum(r[0], r[1]), jnp.maximum(r[2], r[3]))
    gsel = jnp.where(r[0] == gm_, 0.0, jnp.where(r[1] == gm_, 1.0, jnp.where(r[2] == gm_, 2.0, 3.0)))
    g_w = 1.0 / (jnp.exp(r[0] - gm_) + jnp.exp(r[1] - gm_) + jnp.exp(r[2] - gm_) + jnp.exp(r[3] - gm_))
    eg = [lt[8 + 8 * kk:16 + 8 * kk, :] for kk in range(N_GROUPS)]
    ein = jnp.where(gsel == 0.0, eg[0], jnp.where(gsel == 1.0, eg[1], jnp.where(gsel == 2.0, eg[2], eg[3])))
    io8 = lax.broadcasted_iota(jnp.int32, (EXPERTS_PER_GROUP, ts), 0).astype(f32)
    m1 = jnp.max(ein, axis=0, keepdims=True)
    i1 = jnp.min(jnp.where(ein == m1, io8, 8.0), axis=0, keepdims=True)
    e2 = jnp.where(io8 == i1, NEG_INF, ein)
    m2 = jnp.max(e2, axis=0, keepdims=True)
    i2 = jnp.min(jnp.where(e2 == m2, io8, 8.0), axis=0, keepdims=True)
    t2 = jnp.exp(m2 - m1)
    w1 = 1.0 / (1.0 + t2)
    wt0, wt1 = g_w * w1, g_w * (t2 * w1)
    eid0, eid1 = gsel * 8.0 + i1, gsel * 8.0 + i2

    io32 = lax.broadcasted_iota(jnp.int32, (N_EXPERTS, ts), 0).astype(f32)
    oh0, oh1 = io32 == eid0, io32 == eid1
    both = jnp.where(oh0, 1.0, jnp.where(oh1, 1.0, 0.0))
    pref = jnp.dot(both.astype(bf16), tri_ref[...], preferred_element_type=f32)
    yield
    gid = lax.broadcasted_iota(jnp.int32, (N_EXPERTS, ts), 1) // GROUP
    units = jnp.zeros((N_EXPERTS, ts), f32)
    for gi in range(GROUPS_PER_STEP):
        n_e = jnp.sum(jnp.where(gid == gi, both, 0.0), axis=1, keepdims=True)
        cnt_ref[gi] = jnp.broadcast_to(n_e, (N_EXPERTS, LANES))
        units = jnp.where(gid == gi, jnp.floor((n_e + (ROW_ALIGN - 1.0)) * (1.0 / ROW_ALIGN)), units)
    units = jnp.concatenate([units, jnp.zeros((LANES - N_EXPERTS, ts), f32)], axis=0).astype(bf16)
    tot = ROW_ALIGN * jnp.dot(ltri_ref[...], units, preferred_element_type=f32) + pref
    yield
    slot0 = jnp.sum(jnp.where(oh0, tot, 0.0), axis=0, keepdims=True)
    slot1 = jnp.sum(jnp.where(oh1, tot, 0.0), axis=0, keepdims=True)

    io128 = lax.broadcasted_iota(jnp.int32, (LANES, ts), 0)
    rw_ref[...] = jnp.where(io128 == 0, wt0, jnp.where(io128 == 1, wt1,
                            jnp.where(io128 == 2, slot0, jnp.where(io128 == 3, slot1, 0.0)))).T

    s0i, s1i = slot0.astype(jnp.int32), slot1.astype(jnp.int32)
    io = lax.broadcasted_iota(jnp.int32, (BLOCK_ROWS, GROUP), 0)
    for gi in range(GROUPS_PER_STEP):
        tok = slice(gi * GROUP, (gi + 1) * GROUP)
        onehot = jnp.where(io == s0i[:, tok], 1.0, jnp.where(io == s1i[:, tok], 1.0, 0.0)).astype(bf16)
        rows = jnp.dot(onehot, xnb[tok, :], preferred_element_type=f32)
        _store_rows(xs_ref, gi * BLOCK_ROWS, _pack_bf16_pair(rows[:, 0:PACK_COLS], rows[:, PACK_COLS:]))
        yield


def _mixer_kernel(x_ref, xp_ref, g1_ref, win_ref, rc_ref, rs1_ref, rs2_ref, sink_ref, lng_ref, lnb_ref,
                  wcat_ref, bsf_ref, wab_ref, wgb_ref, wout_ref, g2_ref, wrt_ref, brt_ref, tri_ref, ltri_ref,
                  x1_ref, xs_ref, rw_ref, cnt_ref,
                  kl_ref, vt_ref, attnt_ref, gated_ref, mix_ref, *, blocks_per_seq, n_blocks):
    step = pl.program_id(0)
    s_blk = jnp.minimum(step, n_blocks - 1) % blocks_per_seq
    cur = step % 2
    ts = x_ref.shape[0]
    nb = ts // BLOCK
    f32, bf16 = jnp.float32, jnp.bfloat16

    @pl.when(step == 0)
    def _():
        mix_ref[1] = jnp.zeros((ts, D_MODEL), bf16)

    @pl.when(s_blk == 0)
    def _():
        kl_ref[:, 0:BLOCK, :] = jnp.zeros((4, BLOCK, LANES), bf16)
        vt_ref[:, 0:BLOCK] = jnp.zeros((LANES, BLOCK), bf16)

    x1 = xp_ref[...] + jnp.dot(mix_ref[1 - cur], wout_ref[...], preferred_element_type=f32)
    x1_ref[...] = x1
    route = _route_and_compact(_rms(x1, g2_ref[...]).astype(bf16), wrt_ref, brt_ref, tri_ref, ltri_ref,
                               xs_ref, rw_ref, cnt_ref)
    step_route = lambda: next(route, None)

    x = x_ref[...]
    hb = _rms(x, g1_ref[...]).astype(bf16)

    qkv = jnp.dot(hb, win_ref[:, 0:V_END], preferred_element_type=f32)
    rc, rs1, rs2 = rc_ref[...], rs1_ref[...], rs2_ref[...]

    def rope(t):
        return t * rc + pltpu.roll(t, 8, 1) * rs1 + pltpu.roll(t, LANES - 8, 1) * rs2

    scale = 1.0 / math.sqrt(HEAD_DIM)
    qb = [(rope(qkv[:, j * LANES:(j + 1) * LANES]) * scale).astype(bf16) for j in range(4)]
    k = rope(qkv[:, Q_END:K_END])
    v = qkv[:, K_END:V_END]
    k_sw = pltpu.roll(k, HEAD_DIM, 1)
    lo = lax.broadcasted_iota(jnp.int32, (ts, LANES), 1) < HEAD_DIM
    zero = jnp.zeros_like(k)
    kl_ref[0, BLOCK:BLOCK + ts, :] = jnp.where(lo, k, zero).astype(bf16)
    kl_ref[1, BLOCK:BLOCK + ts, :] = jnp.where(lo, zero, k_sw).astype(bf16)
    kl_ref[2, BLOCK:BLOCK + ts, :] = jnp.where(lo, k_sw, zero).astype(bf16)
    kl_ref[3, BLOCK:BLOCK + ts, :] = jnp.where(lo, zero, k).astype(bf16)
    vt_ref[:, BLOCK:BLOCK + ts] = v.T.astype(bf16)

    kj = lax.broadcasted_iota(jnp.int32, (2 * BLOCK, 2 * BLOCK), 0)
    qi = lax.broadcasted_iota(jnp.int32, (2 * BLOCK, 2 * BLOCK), 1) % BLOCK
    band = (kj > qi) & (kj <= qi + BLOCK)
    kmin = jnp.where(s_blk == 0, BLOCK, 0)
    bias_first = jnp.where(band & (kj >= kmin), 0.0, NEG_INF).astype(f32)
    bias_rest = jnp.where(band, 0.0, NEG_INF).astype(f32)
    items = [(i, g, p) for i in range(nb) for g in range(N_KV_HEADS) for p in range(2)]

    def scores(n):
        i, g, p = items[n]
        rows = slice(i * BLOCK, (i + 1) * BLOCK)
        qs = jnp.concatenate([qb[2 * g][rows], qb[2 * g + 1][rows]], axis=0)
        st = lax.dot_general(kl_ref[2 * g + p, i * BLOCK:(i + 2) * BLOCK, :], qs, (((1,), (1,)), ((), ())),
                             preferred_element_type=f32)
        return st + (bias_first if i == 0 else bias_rest)

    def finish(n, st):
        i, g, p = items[n]
        rows = slice(i * BLOCK, (i + 1) * BLOCK)
        sink = sink_ref[2 * g + p:2 * g + p + 1, :]
        m = jnp.maximum(jnp.max(st, axis=0, keepdims=True), sink)
        e = jnp.exp(st - m)
        den = jnp.sum(e, axis=0, keepdims=True) + jnp.exp(sink - m)
        ot = jnp.dot(vt_ref[g * HEAD_DIM:(g + 1) * HEAD_DIM, i * BLOCK:(i + 2) * BLOCK], e.astype(bf16),
                     preferred_element_type=f32) * (1.0 / den)
        h0 = 4 * g + p
        attnt_ref[h0 * HEAD_DIM:(h0 + 1) * HEAD_DIM, rows] = ot[:, 0:BLOCK]
        attnt_ref[(h0 + 2) * HEAD_DIM:(h0 + 3) * HEAD_DIM, rows] = ot[:, BLOCK:2 * BLOCK]

    n_chunks = D_MODEL // CHUNK_COLS
    zc, sga, sgg, gmc = [None] * n_chunks, [None] * n_chunks, [None] * n_chunks, [None] * n_chunks
    vn_box = []

    def proj(lo_col, c):
        cols = slice(lo_col + c * CHUNK_COLS, lo_col + (c + 1) * CHUNK_COLS)
        return jnp.dot(hb, win_ref[:, cols], preferred_element_type=f32)

    def z_chunk(c):
        zc[c] = jax.nn.gelu(proj(V_END, c))
        if c == n_chunks - 1:
            v2 = jnp.concatenate(zc[n_chunks // 2:], axis=1)
            mu = jnp.mean(v2, axis=-1, keepdims=True)
            vc = v2 - mu
            var = jnp.mean(vc * vc, axis=-1, keepdims=True)
            vn_box.append((vc * lax.rsqrt(var + EPS) * lng_ref[...] + lnb_ref[...]).astype(bf16))

    def ga_chunk(c):
        sga[c] = jax.nn.sigmoid(proj(Z_END, c))

    def gg_chunk(c):
        sgg[c] = jax.nn.sigmoid(proj(GA_END, c))

    wq = lax.broadcasted_iota(jnp.int32, (BLOCK, 2 * BLOCK), 0)
    wp = lax.broadcasted_iota(jnp.int32, (BLOCK, 2 * BLOCK), 1) % BLOCK
    causal = wp <= wq
    lo_c = lax.broadcasted_iota(jnp.int32, (BLOCK, LANES), 1) < G_GROUP_DIM
    wjs = {}

    def gmlp_dot(j, c):
        if j not in wjs:
            wjs[j] = jnp.where(causal, wcat_ref[j], 0.0).astype(bf16)
        cols = slice(j * LANES, (j + 1) * LANES)
        rhs = []
        for cc in (c, c + 1):
            vp = vn_box[0][cc * BLOCK:(cc + 1) * BLOCK, cols]
            zb = jnp.zeros_like(vp)
            rhs.append(jnp.concatenate([jnp.where(lo_c, vp, zb), jnp.where(lo_c, zb, vp)], axis=0))
        mixed = jnp.dot(wjs[j], jnp.concatenate(rhs, axis=1), preferred_element_type=f32)
        u = zc[j // 2][:, (j % 2) * LANES:(j % 2 + 1) * LANES]
        for q, cc in enumerate((c, c + 1)):
            rows = slice(cc * BLOCK, (cc + 1) * BLOCK)
            gated_ref[rows, cols] = u[rows] * (mixed[:, q * LANES:(q + 1) * LANES] + bsf_ref[:, cols])

    def gm_chunk(c):
        cols = slice(c * CHUNK_COLS, (c + 1) * CHUNK_COLS)
        gmc[c] = jnp.dot(gated_ref[...].astype(bf16), wgb_ref[:, cols], preferred_element_type=f32)

    fillers = [[functools.partial(z_chunk, c)] for c in range(n_chunks)]
    fillers += [[functools.partial(ga_chunk, c)] for c in range(n_chunks)]
    fillers += [[functools.partial(gg_chunk, c)] for c in range(n_chunks)]
    fillers += [[functools.partial(gm_chunk, c)] for c in range(n_chunks)]
    gd = [functools.partial(gmlp_dot, j, c) for j in range(G_GROUPS // 2) for c in range(0, nb, 2)]
    for q in range(len(gd)):
        fillers[n_chunks + q].append(gd[q])
    assert len(fillers) == len(items)

    fillers[0][0]()
    step_route()
    fillers[1][0]()
    step_route()
    step_route()
    fillers = fillers[2:] + [[], []]
    for n in ROUTE_AT:
        fillers[n].append(step_route)
    sts = {n: scores(n) for n in range(LOOKAHEAD)}
    for n in range(len(items)):
        finish(n, sts.pop(n))
        if n + LOOKAHEAD < len(items):
            sts[n + LOOKAHEAD] = scores(n + LOOKAHEAD)
        for f in fillers[n]:
            f()
    kl_ref[:, 0:BLOCK, :] = kl_ref[:, ts:ts + BLOCK, :]
    vt_ref[:, 0:BLOCK] = vt_ref[:, ts:ts + BLOCK]
    attn_b = attnt_ref[...].T.astype(bf16)

    mix = []
    for c in range(n_chunks):
        cols = slice(c * CHUNK_COLS, (c + 1) * CHUNK_COLS)
        a_c = jnp.dot(attn_b, wab_ref[:, cols], preferred_element_type=f32)
        mix.append((sga[c] * a_c + sgg[c] * gmc[c]).astype(bf16))
    mix_ref[cur] = jnp.concatenate(mix, axis=1)
    for _ in route:
        pass


def _const_spec(shape):
    nd = len(shape)
    return pl.BlockSpec(shape, lambda i: (0,) * nd)


def _mixer(x2, g1, win, rc, rs1, rs2, sinks, lng, lnb, wcat, bsf, wab, wgb, wout, g2, wrt, brt, tri, ltri, seq):
    t = x2.shape[0]
    n_blocks = t // TS
    blocks_per_seq = seq // TS
    cur_blk = lambda i: (jnp.minimum(i, n_blocks - 1), 0)
    prev_blk = lambda i: (jnp.maximum(i - 1, 0), 0)
    pos = lambda i: (jnp.minimum(i, n_blocks - 1) % blocks_per_seq, 0)
    consts = (sinks, lng, lnb, wcat, bsf, wab, wgb, wout, g2, wrt, brt, tri, ltri)
    in_specs = [
        pl.BlockSpec((TS, D_MODEL), cur_blk),
        pl.BlockSpec((TS, D_MODEL), prev_blk),
        _const_spec(g1.shape), _const_spec(win.shape),
        pl.BlockSpec((TS, LANES), pos), pl.BlockSpec((TS, LANES), pos), pl.BlockSpec((TS, LANES), pos),
    ] + [_const_spec(c.shape) for c in consts]
    out_shape = (
        jax.ShapeDtypeStruct((t, D_MODEL), jnp.float32),
        jax.ShapeDtypeStruct((n_blocks * GROUPS_PER_STEP * BLOCK_ROWS * ROW_LINES, LANES), jnp.uint32),
        jax.ShapeDtypeStruct((t, LANES), jnp.float32),
        jax.ShapeDtypeStruct((n_blocks * GROUPS_PER_STEP, N_EXPERTS, LANES), jnp.float32),
    )
    out_specs = (
        pl.BlockSpec((TS, D_MODEL), prev_blk),
        pl.BlockSpec((GROUPS_PER_STEP * BLOCK_ROWS * ROW_LINES, LANES), prev_blk),
        pl.BlockSpec((TS, LANES), prev_blk),
        pl.BlockSpec((GROUPS_PER_STEP, N_EXPERTS, LANES), lambda i: (jnp.maximum(i - 1, 0), 0, 0)),
    )
    scratch = [
        pltpu.VMEM((4, BLOCK + TS, LANES), jnp.bfloat16),
        pltpu.VMEM((LANES, BLOCK + TS), jnp.bfloat16),
        pltpu.VMEM((ATTN_WIDTH, TS), jnp.float32),
        pltpu.VMEM((TS, G_WIDTH), jnp.float32),
        pltpu.VMEM((2, TS, D_MODEL), jnp.bfloat16),
    ]
    return pl.pallas_call(
        functools.partial(_mixer_kernel, blocks_per_seq=blocks_per_seq, n_blocks=n_blocks),
        grid=(n_blocks + 1,),
        in_specs=in_specs,
        out_specs=out_specs,
        out_shape=out_shape,
        scratch_shapes=scratch,
        compiler_params=pltpu.CompilerParams(dimension_semantics=("arbitrary",),
                                             vmem_limit_bytes=VMEM_LIMIT),
        name="mixer",
    )(x2, x2, g1, win, rc, rs1, rs2, *consts)


def _expert_kernel(te_ref, tr0_ref, tb0_ref, tb1_ref, trows_ref, nt_ref, cum_ref, src_ref, used_ref, word_ref, wnext_ref,
                   xs_ref, wg_ref, wu_ref, wd_ref, ys_ref,
                   xbuf_ref, ybuf_ref, wgs_ref, wus_ref, wds_ref, wgb_ref, wub_ref, wdb_ref, zero_ref,
                   isem, osem, zsem, wsem,
                   *, n_blocks):
    j = pl.program_id(0)
    n_tiles = nt_ref[0]
    slot = j % 2
    f32, bf16 = jnp.float32, jnp.bfloat16

    def lines(rows):
        return pl.multiple_of(rows * ROW_LINES, SUBLANES)

    def run_piece(tile, b, ok, fn):
        e, r0 = te_ref[tile], tr0_ref[tile]
        bb = jnp.minimum(b, n_blocks - 1)
        start = cum_ref[e * (n_blocks + 1) + bb]
        lo = jnp.maximum(start, r0)
        n = jnp.minimum(cum_ref[e * (n_blocks + 1) + bb + 1], r0 + TM) - lo

        @pl.when(ok & (b < tb1_ref[tile]) & (n > 0))
        def _():
            fn(lines(src_ref[e * n_blocks + bb] + (lo - start)), lines(lo - r0), lines(n))

    def first_runs(tile, ok, fn):
        for k in range(FAST_RUNS):
            run_piece(tile, tb0_ref[tile] + k, ok, fn)

    def other_runs(tile, first, fn):
        lax.fori_loop(tb0_ref[tile] + first, tb1_ref[tile], lambda b, c: (run_piece(tile, b, True, fn), c)[1], 0)

    def gather_fn(s):
        return lambda src, dst, n: pltpu.make_async_copy(
            xs_ref.at[pl.ds(src, n)], xbuf_ref.at[s, pl.ds(dst, n)], isem.at[s]).start()

    def scatter_fn(s):
        return lambda src, dst, n: pltpu.make_async_copy(
            ybuf_ref.at[s, pl.ds(dst, n)], ys_ref.at[pl.ds(src, n)], osem.at[s]).start()

    def wait_rows(sem, n):
        @pl.when(n > 0)
        def _():
            pltpu.make_async_copy(xs_ref.at[pl.ds(0, lines(n))], xbuf_ref.at[0, pl.ds(0, lines(n))], sem).wait()

    def weight_copies(e, ws):
        return [pltpu.make_async_copy(src.at[e], dst.at[ws], wsem.at[ws])
                for src, dst in ((wg_ref, wgs_ref), (wu_ref, wus_ref), (wd_ref, wds_ref))]

    def zero_tail_copy(b, c):
        first = used_ref[b] + c * TM
        n = lines(jnp.minimum(BLOCK_ROWS - first, TM))
        return pltpu.make_async_copy(zero_ref.at[pl.ds(0, n)], ys_ref.at[pl.ds(lines(b * BLOCK_ROWS + first), n)], zsem)

    @pl.when(j == 0)
    def _():
        xbuf_ref[...] = jnp.zeros_like(xbuf_ref)
        zero_ref[...] = jnp.zeros_like(zero_ref)

        def per_block(b, carry):
            chunks = (BLOCK_ROWS - used_ref[b] + TM - 1) // TM
            lax.fori_loop(0, chunks, lambda c, x: (zero_tail_copy(b, c).start(), x)[1], 0)
            lax.fori_loop(0, chunks, lambda c, x: (zero_tail_copy(b, c).wait(), x)[1], 0)
            return carry

        lax.fori_loop(0, n_blocks, per_block, 0)

        for t0 in range(GATHER_AHEAD):
            @pl.when(t0 < n_tiles)
            def _():
                other_runs(t0, 0, gather_fn(t0))

        @pl.when(n_tiles > 0)
        def _():
            for cp in weight_copies(te_ref[0], 0):
                cp.start()

    prev = jnp.maximum(j - 1, 0)
    xslot = j % (GATHER_AHEAD + 1)
    nslot = (j + GATHER_AHEAD) % (GATHER_AHEAD + 1)

    @pl.when((j == n_tiles) & (j > 0))
    def _():
        other_runs(prev, 0, scatter_fn(1 - slot))

    @pl.when((j >= n_tiles) & (j >= 2) & (j - 2 < n_tiles))
    def _():
        wait_rows(osem.at[slot], trows_ref[jnp.maximum(j - 2, 0)])

    @pl.when(j < n_tiles)
    def _():
        @pl.when((j == 0) | (te_ref[j] != te_ref[prev]))
        def _():
            ws = word_ref[j] % 2
            for cp in weight_copies(te_ref[j], ws):
                cp.wait()
            wgb_ref[...] = wgs_ref[ws].astype(bf16)
            wub_ref[...] = wus_ref[ws].astype(bf16)
            wdb_ref[...] = wds_ref[ws].astype(bf16)

            @pl.when(wnext_ref[j] >= 0)
            def _():
                for cp in weight_copies(wnext_ref[j], 1 - ws):
                    cp.start()

        wait_rows(isem.at[xslot], trows_ref[j])
        xb = _unpack_bf16_rows(_load_rows(xbuf_ref.at[xslot], 0, TM))
        nxt = jnp.minimum(j + GATHER_AHEAD, n_tiles - 1)
        more = j + GATHER_AHEAD < n_tiles
        first_runs(prev, j > 0, scatter_fn(1 - slot))
        first_runs(nxt, more, gather_fn(nslot))
        half = D_EXPERT // 2
        g0 = jnp.dot(xb, wgb_ref[:, 0:half], preferred_element_type=f32)
        u0 = jnp.dot(xb, wub_ref[:, 0:half], preferred_element_type=f32)
        g1 = jnp.dot(xb, wgb_ref[:, half:], preferred_element_type=f32)
        u1 = jnp.dot(xb, wub_ref[:, half:], preferred_element_type=f32)
        y = jnp.dot((jax.nn.silu(g0) * u0).astype(bf16), wdb_ref[0:half, :], preferred_element_type=f32)
        y = y + jnp.dot((jax.nn.silu(g1) * u1).astype(bf16), wdb_ref[half:, :], preferred_element_type=f32)
        packed = _pack_bf16_pair(y[:, 0:PACK_COLS], y[:, PACK_COLS:])
        probe = xbuf_ref[xslot, pl.ds(0, SUBLANES, stride=ROW_LINES), :]

        @pl.when(j >= 2)
        def _():
            wait_rows(osem.at[slot], trows_ref[jnp.maximum(j - 2, 0)])

        _store_rows(ybuf_ref.at[slot], 0, packed)
        ybuf_ref[slot, pl.ds(0, SUBLANES, stride=ROW_LINES), :] = packed[0:SUBLANES, 0:LANES] | ((probe >> 16) >> 16)

        @pl.when(more)
        def _():
            other_runs(nxt, FAST_RUNS, gather_fn(nslot))

        @pl.when(j > 0)
        def _():
            other_runs(prev, FAST_RUNS, scatter_fn(1 - slot))


def _experts(tables, xs, wg, wu, wd, n_blocks, grid_tiles):
    grid_spec = pltpu.PrefetchScalarGridSpec(
        num_scalar_prefetch=len(tables),
        grid=(grid_tiles,),
        in_specs=[
            pl.BlockSpec(memory_space=pl.ANY),
            pl.BlockSpec(memory_space=pl.ANY),
            pl.BlockSpec(memory_space=pl.ANY),
            pl.BlockSpec(memory_space=pl.ANY),
        ],
        out_specs=pl.BlockSpec(memory_space=pl.ANY),
        scratch_shapes=[
            pltpu.VMEM((GATHER_AHEAD + 1, TM * ROW_LINES, LANES), jnp.uint32),
            pltpu.VMEM((2, TM * ROW_LINES, LANES), jnp.uint32),
            pltpu.VMEM((2, D_MODEL, D_EXPERT), jnp.float32),
            pltpu.VMEM((2, D_MODEL, D_EXPERT), jnp.float32),
            pltpu.VMEM((2, D_EXPERT, D_MODEL), jnp.float32),
            pltpu.VMEM((D_MODEL, D_EXPERT), jnp.bfloat16),
            pltpu.VMEM((D_MODEL, D_EXPERT), jnp.bfloat16),
            pltpu.VMEM((D_EXPERT, D_MODEL), jnp.bfloat16),
            pltpu.VMEM((TM * ROW_LINES, LANES), jnp.uint32),
            pltpu.SemaphoreType.DMA((GATHER_AHEAD + 1,)),
            pltpu.SemaphoreType.DMA((2,)),
            pltpu.SemaphoreType.DMA,
            pltpu.SemaphoreType.DMA((2,)),
        ],
    )
    return pl.pallas_call(
        functools.partial(_expert_kernel, n_blocks=n_blocks),
        grid_spec=grid_spec,
        out_shape=jax.ShapeDtypeStruct(xs.shape, jnp.uint32),
        compiler_params=pltpu.CompilerParams(dimension_semantics=("arbitrary",),
                                             vmem_limit_bytes=VMEM_LIMIT),
        name="experts",
    )(*tables, xs, wg, wu, wd)


def _combine_kernel(x1_ref, rw_ref, gf_ref, ys_ref, o_ref):
    rw = rw_ref[...]
    k_pad = -(-BLOCK_ROWS // LANES) * LANES
    io = lax.broadcasted_iota(jnp.int32, (GROUP, k_pad), 1).astype(jnp.float32)
    parts = []
    for gi in range(GROUPS_PER_STEP):
        r = rw[gi * GROUP:(gi + 1) * GROUP]
        wsel = jnp.where(io == r[:, 2:3], r[:, 0:1], 0.0) + jnp.where(io == r[:, 3:4], r[:, 1:2], 0.0)
        ys = _unpack_bf16_rows(_load_rows(ys_ref, gi * BLOCK_ROWS, BLOCK_ROWS))
        ys = jnp.concatenate([ys, jnp.zeros((k_pad - BLOCK_ROWS, D_MODEL), jnp.bfloat16)], axis=0)
        parts.append(jnp.dot(wsel.astype(jnp.bfloat16), ys, preferred_element_type=jnp.float32))
    o_ref[...] = _rms(x1_ref[...] + jnp.concatenate(parts, axis=0), gf_ref[...])


def _combine(x1, rw, gf, ys):
    t = x1.shape[0]
    tok = lambda i: (i, 0)
    return pl.pallas_call(
        _combine_kernel,
        grid=(t // TS,),
        in_specs=[
            pl.BlockSpec((TS, D_MODEL), tok),
            pl.BlockSpec((TS, LANES), tok),
            pl.BlockSpec((1, D_MODEL), lambda i: (0, 0)),
            pl.BlockSpec((GROUPS_PER_STEP * BLOCK_ROWS * ROW_LINES, LANES), tok),
        ],
        out_specs=pl.BlockSpec((TS, D_MODEL), tok),
        out_shape=jax.ShapeDtypeStruct((t, D_MODEL), jnp.float32),
        compiler_params=pltpu.CompilerParams(dimension_semantics=("arbitrary",),
                                             vmem_limit_bytes=VMEM_LIMIT),
        name="combine",
    )(x1, rw, gf, ys)


def _rope_lane_tables(seq):
    inv_freq = ROPE_THETA ** (-jnp.arange(0, ROT_DIM, 2, dtype=jnp.float32) / ROT_DIM)
    ang = jnp.arange(seq, dtype=jnp.float32)[:, None] * inv_freq[None, :]
    cos, sin = jnp.cos(ang), jnp.sin(ang)
    half = ROT_DIM // 2
    ones = jnp.ones((seq, HEAD_DIM - ROT_DIM), jnp.float32)
    zeros = jnp.zeros((seq, HEAD_DIM - ROT_DIM), jnp.float32)
    zh = jnp.zeros((seq, half), jnp.float32)
    c = jnp.concatenate([cos, cos, ones], axis=1)
    s1 = jnp.concatenate([zh, sin, zeros], axis=1)
    s2 = jnp.concatenate([-sin, zh, zeros], axis=1)
    rep = LANES // HEAD_DIM
    return jnp.tile(c, (1, rep)), jnp.tile(s1, (1, rep)), jnp.tile(s2, (1, rep))


def _pick(table, idx):
    return jnp.sum(jnp.where(idx[:, None] == jnp.arange(table.shape[0])[None, :], table[None, :], 0), axis=1)


def _tile_tables(cnt, n_blocks, grid_tiles):
    i32 = jnp.int32
    n = cnt[:, :, 0].astype(i32)
    n8 = (n + ROW_ALIGN - 1) // ROW_ALIGN * ROW_ALIGN
    loc = jnp.cumsum(n8, axis=1) - n8
    used = jnp.sum(n8, axis=1)
    cum = jnp.concatenate([jnp.zeros((1, N_EXPERTS), i32), jnp.cumsum(n8, axis=0)], axis=0).T
    total = cum[:, -1]
    src = (jnp.arange(n_blocks, dtype=i32)[:, None] * BLOCK_ROWS + loc).T
    tiles_e = (total + TM - 1) // TM
    tile_end = jnp.cumsum(tiles_e)
    n_tiles = tile_end[-1]
    j = jnp.arange(grid_tiles, dtype=i32)
    te = jnp.minimum(jnp.sum(tile_end[None, :] <= j[:, None], axis=1), N_EXPERTS - 1).astype(i32)
    live = j < n_tiles
    te = jnp.where(live, te, _pick(te, jnp.maximum(n_tiles - 1, 0)[None])[0])
    r0 = (j - _pick(tile_end - tiles_e, te)) * TM
    cum_t = cum[te]
    b0 = jnp.sum(cum_t[:, 1:] <= r0[:, None], axis=1)
    b1 = jnp.sum(cum_t[:, :-1] < (r0 + TM)[:, None], axis=1)
    rows = jnp.clip(_pick(total, te) - r0, 0, TM)
    zero = jnp.zeros_like(j)
    has = tiles_e > 0
    ids = jnp.arange(N_EXPERTS, dtype=i32)
    nxt = jnp.min(jnp.where((ids[None, :] > ids[:, None]) & has[None, :], ids[None, :], N_EXPERTS), axis=1)
    nxt = jnp.where(nxt < N_EXPERTS, nxt, -1)
    order = jnp.cumsum(has.astype(i32)) - 1
    return (te, jnp.where(live, r0, zero), jnp.where(live, b0, zero).astype(i32),
            jnp.where(live, b1, zero).astype(i32), jnp.where(live, rows, zero).astype(i32),
            n_tiles.reshape(1).astype(i32), cum.reshape(-1), src.reshape(-1).astype(i32), used.astype(i32),
            _pick(order, te).astype(i32), _pick(nxt, te).astype(i32))


def kernel(x, norm1_g, w_in, attn_sinks, gmlp_ln_g, gmlp_ln_b, gmlp_ws, gmlp_bs, w_attn_branch,
           w_gmlp_branch, w_out, norm2_g, router_group_w, router_group_b, router_expert_w,
           router_expert_b, expert_w_gate, expert_w_up, expert_w_down, final_norm_g):
    b, s, d = x.shape
    assert d == D_MODEL and s % TS == 0 and norm1_g.shape[0] == 1
    t = b * s
    n_blocks = t // TS
    bf16, f32 = jnp.bfloat16, jnp.float32
    x2 = x.reshape(t, d)

    rc, rs1, rs2 = _rope_lane_tables(s)
    sk = attn_sinks[0].astype(f32)
    sinks = jnp.stack([
        jnp.concatenate([jnp.full((BLOCK,), 1.0, f32) * sk[4 * g + p], jnp.full((BLOCK,), 1.0, f32) * sk[4 * g + 2 + p]])
        for g in range(N_KV_HEADS) for p in range(2)])
    ws = gmlp_ws[0]
    wcat = jnp.stack([jnp.concatenate([ws[2 * j], ws[2 * j + 1]], axis=1) for j in range(G_GROUPS // 2)])
    bsf = jnp.repeat(gmlp_bs[0].T, G_GROUP_DIM, axis=1)
    wrt = jnp.concatenate([router_group_w[0].T, jnp.zeros((8 - N_GROUPS, d), f32), router_expert_w[0].T], axis=0)
    brt = jnp.concatenate([router_group_b[0], jnp.zeros((8 - N_GROUPS,), f32), router_expert_b[0]])
    brt = jnp.broadcast_to(brt[:, None], (ROUTER_ROWS, TS))
    tok_ids = jnp.arange(TS)
    tri = ((tok_ids[:, None] < tok_ids[None, :]) & (tok_ids[:, None] // GROUP == tok_ids[None, :] // GROUP)
           ).astype(bf16)
    ltri = (jnp.arange(LANES)[None, :] < jnp.arange(N_EXPERTS)[:, None]).astype(bf16)

    x1, xs, rw, cnt = _mixer(
        x2, norm1_g, w_in[0].astype(bf16), rc, rs1, rs2, sinks, gmlp_ln_g, gmlp_ln_b, wcat, bsf,
        w_attn_branch[0].astype(bf16), w_gmlp_branch[0].astype(bf16), w_out[0].astype(bf16),
        norm2_g, wrt.astype(bf16), brt, tri, ltri, s)

    n_groups = n_blocks * GROUPS_PER_STEP
    max_tiles = (TOP_K * t + n_groups * N_EXPERTS * (ROW_ALIGN - 1)) // TM + N_EXPERTS
    grid_tiles = max_tiles + 2
    tables = _tile_tables(cnt, n_groups, grid_tiles)
    ys = _experts(tables, xs, expert_w_gate[0], expert_w_up[0], expert_w_down[0], n_groups, grid_tiles)
    out = _combine(x1, rw, final_norm_g.reshape(1, d), ys)
    return out.reshape(b, s, d)
```

```python
import functools
import math

import jax
import jax.numpy as jnp
from jax import lax
from jax.experimental import pallas as pl
from jax.experimental.pallas import tpu as pltpu

D_MODEL = 1024
HEAD_DIM = 64
N_HEADS = 8
N_KV_HEADS = 2
BLOCK = 128
ROT_DIM = HEAD_DIM // 4
ROPE_THETA = 500000.0
ATTN_WIDTH = N_HEADS * HEAD_DIM
KV_WIDTH = N_KV_HEADS * HEAD_DIM
G_GROUPS = 8
G_GROUP_DIM = 64
G_WIDTH = G_GROUPS * G_GROUP_DIM
Q_END = ATTN_WIDTH
K_END = Q_END + KV_WIDTH
V_END = K_END + KV_WIDTH
Z_END = V_END + 2 * G_WIDTH
GA_END = Z_END + D_MODEL
IN_COLS = GA_END + D_MODEL
N_GROUPS = 4
EXPERTS_PER_GROUP = 8
N_EXPERTS = N_GROUPS * EXPERTS_PER_GROUP
TOP_K = 2
D_EXPERT = 512
EPS = 1e-5
NEG_INF = -1e30

LANES = 128
SUBLANES = 8
ROUTER_ROWS = 8 + N_EXPERTS
PACK_COLS = D_MODEL // 2
ROW_LINES = PACK_COLS // LANES
ROW_ALIGN = SUBLANES // ROW_LINES

TS = 512
TM = 512
GATHER_AHEAD = 2
FAST_RUNS = 36
GROUP = 256
CHUNK_COLS = 256
LOOKAHEAD = 3
ROUTE_AT = (0, 1)
GROUPS_PER_STEP = TS // GROUP
BLOCK_ROWS = TOP_K * GROUP + N_EXPERTS * (ROW_ALIGN - 1)
VMEM_LIMIT = 58 * 1024 * 1024


def _rms(x, g):
    return x * lax.rsqrt(jnp.mean(x * x, axis=-1, keepdims=True) + EPS) * g


def _pack_bf16_pair(a, b):
    ua = lax.bitcast_convert_type(a.astype(jnp.bfloat16).astype(jnp.float32), jnp.uint32)
    ub = lax.bitcast_convert_type(b.astype(jnp.bfloat16).astype(jnp.float32), jnp.uint32)
    return ub | (ua >> 16)


def _unpack_bf16_rows(w):
    lo = lax.bitcast_convert_type(w << 16, jnp.float32)
    hi = lax.bitcast_convert_type(w & jnp.uint32(0xFFFF0000), jnp.float32)
    return jnp.concatenate([lo.astype(jnp.bfloat16), hi.astype(jnp.bfloat16)], axis=1)


def _load_rows(lines_ref, first_row, rows):
    return jnp.concatenate([lines_ref[pl.ds(first_row * ROW_LINES + c, rows, stride=ROW_LINES), :]
                            for c in range(ROW_LINES)], axis=1)


def _store_rows(lines_ref, first_row, packed):
    for c in range(ROW_LINES):
        lines_ref[pl.ds(first_row * ROW_LINES + c, packed.shape[0], stride=ROW_LINES), :] = \
            packed[:, c * LANES:(c + 1) * LANES]


def _route_and_compact(xnb, wrt_ref, brt_ref, tri_ref, ltri_ref, xs_ref, rw_ref, cnt_ref):
    ts = xnb.shape[0]
    f32, bf16 = jnp.float32, jnp.bfloat16
    lt = lax.dot_general(wrt_ref[...], xnb, (((1,), (1,)), ((), ())),
                         preferred_element_type=f32) + brt_ref[...]
    yield
    r = [lt[i:i + 1, :] for i in range(N_GROUPS)]
    gm_ = jnp.maximum(jnp.maximum(r[0], r[1]), jnp.maximum(r[2], r[3]))
    gsel = jnp.where(r[0] == gm_, 0.0, jnp.where(r[1] == gm_, 1.0, jnp.where(r[2] == gm_, 2.0, 3.0)))
    g_w = 1.0 / (jnp.exp(r[0] - gm_) + jnp.exp(r[1] - gm_) + jnp.exp(r[2] - gm_) + jnp.exp(r[3] - gm_))
    eg = [lt[8 + 8 * kk:16 + 8 * kk, :] for kk in range(N_GROUPS)]
    ein = jnp.where(gsel == 0.0, eg[0], jnp.where(gsel == 1.0, eg[1], jnp.where(gsel == 2.0, eg[2], eg[3])))
    io8 = lax.broadcasted_iota(jnp.int32, (EXPERTS_PER_GROUP, ts), 0).astype(f32)
    m1 = jnp.max(ein, axis=0, keepdims=True)
    i1 = jnp.min(jnp.where(ein == m1, io8, 8.0), axis=0, keepdims=True)
    e2 = jnp.where(io8 == i1, NEG_INF, ein)
    m2 = jnp.max(e2, axis=0, keepdims=True)
    i2 = jnp.min(jnp.where(e2 == m2, io8, 8.0), axis=0, keepdims=True)
    t2 = jnp.exp(m2 - m1)
    w1 = 1.0 / (1.0 + t2)
    wt0, wt1 = g_w * w1, g_w * (t2 * w1)
    eid0, eid1 = gsel * 8.0 + i1, gsel * 8.0 + i2

    io32 = lax.broadcasted_iota(jnp.int32, (N_EXPERTS, ts), 0).astype(f32)
    oh0, oh1 = io32 == eid0, io32 == eid1
    both = jnp.where(oh0, 1.0, jnp.where(oh1, 1.0, 0.0))
    pref = jnp.dot(both.astype(bf16), tri_ref[...], preferred_element_type=f32)
    yield
    gid = lax.broadcasted_iota(jnp.int32, (N_EXPERTS, ts), 1) // GROUP
    units = jnp.zeros((N_EXPERTS, ts), f32)
    for gi in range(GROUPS_PER_STEP):
        n_e = jnp.sum(jnp.where(gid == gi, both, 0.0), axis=1, keepdims=True)
        cnt_ref[gi] = jnp.broadcast_to(n_e, (N_EXPERTS, LANES))
        units = jnp.where(gid == gi, jnp.floor((n_e + (ROW_ALIGN - 1.0)) * (1.0 / ROW_ALIGN)), units)
    units = jnp.concatenate([units, jnp.zeros((LANES - N_EXPERTS, ts), f32)], axis=0).astype(bf16)
    tot = ROW_ALIGN * jnp.dot(ltri_ref[...], units, preferred_element_type=f32) + pref
    yield
    slot0 = jnp.sum(jnp.where(oh0, tot, 0.0), axis=0, keepdims=True)
    slot1 = jnp.sum(jnp.where(oh1, tot, 0.0), axis=0, keepdims=True)

    io128 = lax.broadcasted_iota(jnp.int32, (LANES, ts), 0)
    rw_ref[...] = jnp.where(io128 == 0, wt0, jnp.where(io128 == 1, wt1,
                            jnp.where(io128 == 2, slot0, jnp.where(io128 == 3, slot1, 0.0)))).T

    s0i, s1i = slot0.astype(jnp.int32), slot1.astype(jnp.int32)
    io = lax.broadcasted_iota(jnp.int32, (BLOCK_ROWS, GROUP), 0)
    for gi in range(GROUPS_PER_STEP):
        tok = slice(gi * GROUP, (gi + 1) * GROUP)
        onehot = jnp.where(io == s0i[:, tok], 1.0, jnp.where(io == s1i[:, tok], 1.0, 0.0)).astype(bf16)
        rows = jnp.dot(onehot, xnb[tok, :], preferred_element_type=f32)
        _store_rows(xs_ref, gi * BLOCK_ROWS, _pack_bf16_pair(rows[:, 0:PACK_COLS], rows[:, PACK_COLS:]))
        yield


def _mixer_kernel(x_ref, xp_ref, g1_ref, win_ref, rc_ref, rs1_ref, rs2_ref, sink_ref, lng_ref, lnb_ref,
                  wcat_ref, bsf_ref, wab_ref, wgb_ref, wout_ref, g2_ref, wrt_ref, brt_ref, tri_ref, ltri_ref,
                  x1_ref, xs_ref, rw_ref, cnt_ref,
                  kl_ref, vt_ref, attnt_ref, gated_ref, mix_ref, *, blocks_per_seq, n_blocks):
    step = pl.program_id(0)
    s_blk = jnp.minimum(step, n_blocks - 1) % blocks_per_seq
    cur = step % 2
    ts = x_ref.shape[0]
    nb = ts // BLOCK
    f32, bf16 = jnp.float32, jnp.bfloat16

    @pl.when(step == 0)
    def _():
        mix_ref[1] = jnp.zeros((ts, D_MODEL), bf16)

    @pl.when(s_blk == 0)
    def _():
        kl_ref[:, 0:BLOCK, :] = jnp.zeros((4, BLOCK, LANES), bf16)
        vt_ref[:, 0:BLOCK] = jnp.zeros((LANES, BLOCK), bf16)

    x1 = xp_ref[...] + jnp.dot(mix_ref[1 - cur], wout_ref[...], preferred_element_type=f32)
    x1_ref[...] = x1
    route = _route_and_compact(_rms(x1, g2_ref[...]).astype(bf16), wrt_ref, brt_ref, tri_ref, ltri_ref,
                               xs_ref, rw_ref, cnt_ref)
    step_route = lambda: next(route, None)

    x = x_ref[...]
    hb = _rms(x, g1_ref[...]).astype(bf16)

    qkv = jnp.dot(hb, win_ref[:, 0:V_END], preferred_element_type=f32)
    rc, rs1, rs2 = rc_ref[...], rs1_ref[...], rs2_ref[...]

    def rope(t):
        return t * rc + pltpu.roll(t, 8, 1) * rs1 + pltpu.roll(t, LANES - 8, 1) * rs2

    scale = 1.0 / math.sqrt(HEAD_DIM)
    qb = [(rope(qkv[:, j * LANES:(j + 1) * LANES]) * scale).astype(bf16) for j in range(4)]
    k = rope(qkv[:, Q_END:K_END])
    v = qkv[:, K_END:V_END]
    k_sw = pltpu.roll(k, HEAD_DIM, 1)
    lo = lax.broadcasted_iota(jnp.int32, (ts, LANES), 1) < HEAD_DIM
    zero = jnp.zeros_like(k)
    kl_ref[0, BLOCK:BLOCK + ts, :] = jnp.where(lo, k, zero).astype(bf16)
    kl_ref[1, BLOCK:BLOCK + ts, :] = jnp.where(lo, zero, k_sw).astype(bf16)
    kl_ref[2, BLOCK:BLOCK + ts, :] = jnp.where(lo, k_sw, zero).astype(bf16)
    kl_ref[3, BLOCK:BLOCK + ts, :] = jnp.where(lo, zero, k).astype(bf16)
    vt_ref[:, BLOCK:BLOCK + ts] = v.T.astype(bf16)

    kj = lax.broadcasted_iota(jnp.int32, (2 * BLOCK, 2 * BLOCK), 0)
    qi = lax.broadcasted_iota(jnp.int32, (2 * BLOCK, 2 * BLOCK), 1) % BLOCK
    band = (kj > qi) & (kj <= qi + BLOCK)
    kmin = jnp.where(s_blk == 0, BLOCK, 0)
    bias_first = jnp.where(band & (kj >= kmin), 0.0, NEG_INF).astype(f32)
    bias_rest = jnp.where(band, 0.0, NEG_INF).astype(f32)
    items = [(i, g, p) for i in range(nb) for g in range(N_KV_HEADS) for p in range(2)]

    def scores(n):
        i, g, p = items[n]
        rows = slice(i * BLOCK, (i + 1) * BLOCK)
        qs = jnp.concatenate([qb[2 * g][rows], qb[2 * g + 1][rows]], axis=0)
        st = lax.dot_general(kl_ref[2 * g + p, i * BLOCK:(i + 2) * BLOCK, :], qs, (((1,), (1,)), ((), ())),
                             preferred_element_type=f32)
        return st + (bias_first if i == 0 else bias_rest)

    def finish(n, st):
        i, g, p = items[n]
        rows = slice(i * BLOCK, (i + 1) * BLOCK)
        sink = sink_ref[2 * g + p:2 * g + p + 1, :]
        m = jnp.maximum(jnp.max(st, axis=0, keepdims=True), sink)
        e = jnp.exp(st - m)
        den = jnp.sum(e, axis=0, keepdims=True) + jnp.exp(sink - m)
        ot = jnp.dot(vt_ref[g * HEAD_DIM:(g + 1) * HEAD_DIM, i * BLOCK:(i + 2) * BLOCK], e.astype(bf16),
                     preferred_element_type=f32) * (1.0 / den)
        h0 = 4 * g + p
        attnt_ref[h0 * HEAD_DIM:(h0 + 1) * HEAD_DIM, rows] = ot[:, 0:BLOCK]
        attnt_ref[(h0 + 2) * HEAD_DIM:(h0 + 3) * HEAD_DIM, rows] = ot[:, BLOCK:2 * BLOCK]

    n_chunks = D_MODEL // CHUNK_COLS
    zc, sga, sgg, gmc = [None] * n_chunks, [None] * n_chunks, [None] * n_chunks, [None] * n_chunks
    vn_box = []

    def proj(lo_col, c):
        cols = slice(lo_col + c * CHUNK_COLS, lo_col + (c + 1) * CHUNK_COLS)
        return jnp.dot(hb, win_ref[:, cols], preferred_element_type=f32)

    def z_chunk(c):
        zc[c] = jax.nn.gelu(proj(V_END, c))
        if c == n_chunks - 1:
            v2 = jnp.concatenate(zc[n_chunks // 2:], axis=1)
            mu = jnp.mean(v2, axis=-1, keepdims=True)
            vc = v2 - mu
            var = jnp.mean(vc * vc, axis=-1, keepdims=True)
            vn_box.append((vc * lax.rsqrt(var + EPS) * lng_ref[...] + lnb_ref[...]).astype(bf16))

    def ga_chunk(c):
        sga[c] = jax.nn.sigmoid(proj(Z_END, c))

    def gg_chunk(c):
        sgg[c] = jax.nn.sigmoid(proj(GA_END, c))

    wq = lax.broadcasted_iota(jnp.int32, (BLOCK, 2 * BLOCK), 0)
    wp = lax.broadcasted_iota(jnp.int32, (BLOCK, 2 * BLOCK), 1) % BLOCK
    causal = wp <= wq
    lo_c = lax.broadcasted_iota(jnp.int32, (BLOCK, LANES), 1) < G_GROUP_DIM
    wjs = {}

    def gmlp_dot(j, c):
        if j not in wjs:
            wjs[j] = jnp.where(causal, wcat_ref[j], 0.0).astype(bf16)
        cols = slice(j * LANES, (j + 1) * LANES)
        rhs = []
        for cc in (c, c + 1):
            vp = vn_box[0][cc * BLOCK:(cc + 1) * BLOCK, cols]
            zb = jnp.zeros_like(vp)
            rhs.append(jnp.concatenate([jnp.where(lo_c, vp, zb), jnp.where(lo_c, zb, vp)], axis=0))
        mixed = jnp.dot(wjs[j], jnp.concatenate(rhs, axis=1), preferred_element_type=f32)
        u = zc[j // 2][:, (j % 2) * LANES:(j % 2 + 1) * LANES]
        for q, cc in enumerate((c, c + 1)):
            rows = slice(cc * BLOCK, (cc + 1) * BLOCK)
            gated_ref[rows, cols] = u[rows] * (mixed[:, q * LANES:(q + 1) * LANES] + bsf_ref[:, cols])

    def gm_chunk(c):
        cols = slice(c * CHUNK_COLS, (c + 1) * CHUNK_COLS)
        gmc[c] = jnp.dot(gated_ref[...].astype(bf16), wgb_ref[:, cols], preferred_element_type=f32)

    fillers = [[functools.partial(z_chunk, c)] for c in range(n_chunks)]
    fillers += [[functools.partial(ga_chunk, c)] for c in range(n_chunks)]
    fillers += [[functools.partial(gg_chunk, c)] for c in range(n_chunks)]
    fillers += [[functools.partial(gm_chunk, c)] for c in range(n_chunks)]
    gd = [functools.partial(gmlp_dot, j, c) for j in range(G_GROUPS // 2) for c in range(0, nb, 2)]
    for q in range(len(gd)):
        fillers[n_chunks + q].append(gd[q])
    assert len(fillers) == len(items)

    fillers[0][0]()
    step_route()
    fillers[1][0]()
    step_route()
    step_route()
    fillers = fillers[2:] + [[], []]
    for n in ROUTE_AT:
        fillers[n].append(step_route)
    sts = {n: scores(n) for n in range(LOOKAHEAD)}
    for n in range(len(items)):
        finish(n, sts.pop(n))
        if n + LOOKAHEAD < len(items):
            sts[n + LOOKAHEAD] = scores(n + LOOKAHEAD)
        for f in fillers[n]:
            f()
    kl_ref[:, 0:BLOCK, :] = kl_ref[:, ts:ts + BLOCK, :]
    vt_ref[:, 0:BLOCK] = vt_ref[:, ts:ts + BLOCK]
    attn_b = attnt_ref[...].T.astype(bf16)

    mix = []
    for c in range(n_chunks):
        cols = slice(c * CHUNK_COLS, (c + 1) * CHUNK_COLS)
        a_c = jnp.dot(attn_b, wab_ref[:, cols], preferred_element_type=f32)
        mix.append((sga[c] * a_c + sgg[c] * gmc[c]).astype(bf16))
    mix_ref[cur] = jnp.concatenate(mix, axis=1)
    for _ in route:
        pass


def _const_spec(shape):
    nd = len(shape)
    return pl.BlockSpec(shape, lambda i: (0,) * nd)


def _mixer(x2, g1, win, rc, rs1, rs2, sinks, lng, lnb, wcat, bsf, wab, wgb, wout, g2, wrt, brt, tri, ltri, seq):
    t = x2.shape[0]
    n_blocks = t // TS
    blocks_per_seq = seq // TS
    cur_blk = lambda i: (jnp.minimum(i, n_blocks - 1), 0)
    prev_blk = lambda i: (jnp.maximum(i - 1, 0), 0)
    pos = lambda i: (jnp.minimum(i, n_blocks - 1) % blocks_per_seq, 0)
    consts = (sinks, lng, lnb, wcat, bsf, wab, wgb, wout, g2, wrt, brt, tri, ltri)
    in_specs = [
        pl.BlockSpec((TS, D_MODEL), cur_blk),
        pl.BlockSpec((TS, D_MODEL), prev_blk),
        _const_spec(g1.shape), _const_spec(win.shape),
        pl.BlockSpec((TS, LANES), pos), pl.BlockSpec((TS, LANES), pos), pl.BlockSpec((TS, LANES), pos),
    ] + [_const_spec(c.shape) for c in consts]
    out_shape = (
        jax.ShapeDtypeStruct((t, D_MODEL), jnp.float32),
        jax.ShapeDtypeStruct((n_blocks * GROUPS_PER_STEP * BLOCK_ROWS * ROW_LINES, LANES), jnp.uint32),
        jax.ShapeDtypeStruct((t, LANES), jnp.float32),
        jax.ShapeDtypeStruct((n_blocks * GROUPS_PER_STEP, N_EXPERTS, LANES), jnp.float32),
    )
    out_specs = (
        pl.BlockSpec((TS, D_MODEL), prev_blk),
        pl.BlockSpec((GROUPS_PER_STEP * BLOCK_ROWS * ROW_LINES, LANES), prev_blk),
        pl.BlockSpec((TS, LANES), prev_blk),
        pl.BlockSpec((GROUPS_PER_STEP, N_EXPERTS, LANES), lambda i: (jnp.maximum(i - 1, 0), 0, 0)),
    )
    scratch = [
        pltpu.VMEM((4, BLOCK + TS, LANES), jnp.bfloat16),
        pltpu.VMEM((LANES, BLOCK + TS), jnp.bfloat16),
        pltpu.VMEM((ATTN_WIDTH, TS), jnp.float32),
        pltpu.VMEM((TS, G_WIDTH), jnp.float32),
        pltpu.VMEM((2, TS, D_MODEL), jnp.bfloat16),
    ]
    return pl.pallas_call(
        functools.partial(_mixer_kernel, blocks_per_seq=blocks_per_seq, n_blocks=n_blocks),
        grid=(n_blocks + 1,),
        in_specs=in_specs,
        out_specs=out_specs,
        out_shape=out_shape,
        scratch_shapes=scratch,
        compiler_params=pltpu.CompilerParams(dimension_semantics=("arbitrary",),
                                             vmem_limit_bytes=VMEM_LIMIT),
        name="mixer",
    )(x2, x2, g1, win, rc, rs1, rs2, *consts)


def _expert_kernel(te_ref, tr0_ref, tb0_ref, tb1_ref, trows_ref, nt_ref, cum_ref, src_ref, used_ref, word_ref, wnext_ref,
                   xs_ref, wg_ref, wu_ref, wd_ref, ys_ref,
                   xbuf_ref, ybuf_ref, wgs_ref, wus_ref, wds_ref, wgb_ref, wub_ref, wdb_ref, zero_ref,
                   isem, osem, zsem, wsem,
                   *, n_blocks):
    j = pl.program_id(0)
    n_tiles = nt_ref[0]
    slot = j % 2
    f32, bf16 = jnp.float32, jnp.bfloat16

    def lines(rows):
        return pl.multiple_of(rows * ROW_LINES, SUBLANES)

    def run_piece(tile, b, ok, fn):
        e, r0 = te_ref[tile], tr0_ref[tile]
        bb = jnp.minimum(b, n_blocks - 1)
        start = cum_ref[e * (n_blocks + 1) + bb]
        lo = jnp.maximum(start, r0)
        n = jnp.minimum(cum_ref[e * (n_blocks + 1) + bb + 1], r0 + TM) - lo

        @pl.when(ok & (b < tb1_ref[tile]) & (n > 0))
        def _():
            fn(lines(src_ref[e * n_blocks + bb] + (lo - start)), lines(lo - r0), lines(n))

    def first_runs(tile, ok, fn):
        for k in range(FAST_RUNS):
            run_piece(tile, tb0_ref[tile] + k, ok, fn)

    def other_runs(tile, first, fn):
        lax.fori_loop(tb0_ref[tile] + first, tb1_ref[tile], lambda b, c: (run_piece(tile, b, True, fn), c)[1], 0)

    def gather_fn(s):
        return lambda src, dst, n: pltpu.make_async_copy(
            xs_ref.at[pl.ds(src, n)], xbuf_ref.at[s, pl.ds(dst, n)], isem.at[s]).start()

    def scatter_fn(s):
        return lambda src, dst, n: pltpu.make_async_copy(
            ybuf_ref.at[s, pl.ds(dst, n)], ys_ref.at[pl.ds(src, n)], osem.at[s]).start()

    def wait_rows(sem, n):
        @pl.when(n > 0)
        def _():
            pltpu.make_async_copy(xs_ref.at[pl.ds(0, lines(n))], xbuf_ref.at[0, pl.ds(0, lines(n))], sem).wait()

    def weight_copies(e, ws):
        return [pltpu.make_async_copy(src.at[e], dst.at[ws], wsem.at[ws])
                for src, dst in ((wg_ref, wgs_ref), (wu_ref, wus_ref), (wd_ref, wds_ref))]

    def zero_tail_copy(b, c):
        first = used_ref[b] + c * TM
        n = lines(jnp.minimum(BLOCK_ROWS - first, TM))
        return pltpu.make_async_copy(zero_ref.at[pl.ds(0, n)], ys_ref.at[pl.ds(lines(b * BLOCK_ROWS + first), n)], zsem)

    @pl.when(j == 0)
    def _():
        xbuf_ref[...] = jnp.zeros_like(xbuf_ref)
        zero_ref[...] = jnp.zeros_like(zero_ref)

        def per_block(b, carry):
            chunks = (BLOCK_ROWS - used_ref[b] + TM - 1) // TM
            lax.fori_loop(0, chunks, lambda c, x: (zero_tail_copy(b, c).start(), x)[1], 0)
            lax.fori_loop(0, chunks, lambda c, x: (zero_tail_copy(b, c).wait(), x)[1], 0)
            return carry

        lax.fori_loop(0, n_blocks, per_block, 0)

        for t0 in range(GATHER_AHEAD):
            @pl.when(t0 < n_tiles)
            def _():
                other_runs(t0, 0, gather_fn(t0))

        @pl.when(n_tiles > 0)
        def _():
            for cp in weight_copies(te_ref[0], 0):
                cp.start()

    prev = jnp.maximum(j - 1, 0)
    xslot = j % (GATHER_AHEAD + 1)
    nslot = (j + GATHER_AHEAD) % (GATHER_AHEAD + 1)

    @pl.when((j == n_tiles) & (j > 0))
    def _():
        other_runs(prev, 0, scatter_fn(1 - slot))

    @pl.when((j >= n_tiles) & (j >= 2) & (j - 2 < n_tiles))
    def _():
        wait_rows(osem.at[slot], trows_ref[jnp.maximum(j - 2, 0)])

    @pl.when(j < n_tiles)
    def _():
        @pl.when((j == 0) | (te_ref[j] != te_ref[prev]))
        def _():
            ws = word_ref[j] % 2
            for cp in weight_copies(te_ref[j], ws):
                cp.wait()
            wgb_ref[...] = wgs_ref[ws].astype(bf16)
            wub_ref[...] = wus_ref[ws].astype(bf16)
            wdb_ref[...] = wds_ref[ws].astype(bf16)

            @pl.when(wnext_ref[j] >= 0)
            def _():
                for cp in weight_copies(wnext_ref[j], 1 - ws):
                    cp.start()

        wait_rows(isem.at[xslot], trows_ref[j])
        nxt = jnp.minimum(j + GATHER_AHEAD, n_tiles - 1)
        more = j + GATHER_AHEAD < n_tiles

        def tile_compute(rows):
            xb = _unpack_bf16_rows(_load_rows(xbuf_ref.at[xslot], 0, rows))
            first_runs(prev, j > 0, scatter_fn(1 - slot))
            first_runs(nxt, more, gather_fn(nslot))
            half = D_EXPERT // 2
            g0 = jnp.dot(xb, wgb_ref[:, 0:half], preferred_element_type=f32)
            u0 = jnp.dot(xb, wub_ref[:, 0:half], preferred_element_type=f32)
            g1 = jnp.dot(xb, wgb_ref[:, half:], preferred_element_type=f32)
            u1 = jnp.dot(xb, wub_ref[:, half:], preferred_element_type=f32)
            y = jnp.dot((jax.nn.silu(g0) * u0).astype(bf16), wdb_ref[0:half, :], preferred_element_type=f32)
            y = y + jnp.dot((jax.nn.silu(g1) * u1).astype(bf16), wdb_ref[half:, :], preferred_element_type=f32)
            packed = _pack_bf16_pair(y[:, 0:PACK_COLS], y[:, PACK_COLS:])
            probe = xbuf_ref[xslot, pl.ds(0, SUBLANES, stride=ROW_LINES), :]

            @pl.when(j >= 2)
            def _():
                wait_rows(osem.at[slot], trows_ref[jnp.maximum(j - 2, 0)])

            _store_rows(ybuf_ref.at[slot], 0, packed)
            ybuf_ref[slot, pl.ds(0, SUBLANES, stride=ROW_LINES), :] = (
                packed[0:SUBLANES, 0:LANES] | ((probe >> 16) >> 16))

        @pl.when(trows_ref[j] > TM // 2)
        def _():
            tile_compute(TM)

        @pl.when(trows_ref[j] <= TM // 2)
        def _():
            tile_compute(TM // 2)

        @pl.when(more)
        def _():
            other_runs(nxt, FAST_RUNS, gather_fn(nslot))

        @pl.when(j > 0)
        def _():
            other_runs(prev, FAST_RUNS, scatter_fn(1 - slot))


def _experts(tables, xs, wg, wu, wd, n_blocks, grid_tiles):
    grid_spec = pltpu.PrefetchScalarGridSpec(
        num_scalar_prefetch=len(tables),
        grid=(grid_tiles,),
        in_specs=[
            pl.BlockSpec(memory_space=pl.ANY),
            pl.BlockSpec(memory_space=pl.ANY),
            pl.BlockSpec(memory_space=pl.ANY),
            pl.BlockSpec(memory_space=pl.ANY),
        ],
        out_specs=pl.BlockSpec(memory_space=pl.ANY),
        scratch_shapes=[
            pltpu.VMEM((GATHER_AHEAD + 1, TM * ROW_LINES, LANES), jnp.uint32),
            pltpu.VMEM((2, TM * ROW_LINES, LANES), jnp.uint32),
            pltpu.VMEM((2, D_MODEL, D_EXPERT), jnp.float32),
            pltpu.VMEM((2, D_MODEL, D_EXPERT), jnp.float32),
            pltpu.VMEM((2, D_EXPERT, D_MODEL), jnp.float32),
            pltpu.VMEM((D_MODEL, D_EXPERT), jnp.bfloat16),
            pltpu.VMEM((D_MODEL, D_EXPERT), jnp.bfloat16),
            pltpu.VMEM((D_EXPERT, D_MODEL), jnp.bfloat16),
            pltpu.VMEM((TM * ROW_LINES, LANES), jnp.uint32),
            pltpu.SemaphoreType.DMA((GATHER_AHEAD + 1,)),
            pltpu.SemaphoreType.DMA((2,)),
            pltpu.SemaphoreType.DMA,
            pltpu.SemaphoreType.DMA((2,)),
        ],
    )
    return pl.pallas_call(
        functools.partial(_expert_kernel, n_blocks=n_blocks),
        grid_spec=grid_spec,
        out_shape=jax.ShapeDtypeStruct(xs.shape, jnp.uint32),
        compiler_params=pltpu.CompilerParams(dimension_semantics=("arbitrary",),
                                             vmem_limit_bytes=VMEM_LIMIT),
        name="experts",
    )(*tables, xs, wg, wu, wd)


def _combine_kernel(x1_ref, rw_ref, gf_ref, ys_ref, o_ref):
    rw = rw_ref[...]
    k_pad = -(-BLOCK_ROWS // LANES) * LANES
    io = lax.broadcasted_iota(jnp.int32, (GROUP, k_pad), 1).astype(jnp.float32)
    parts = []
    for gi in range(GROUPS_PER_STEP):
        r = rw[gi * GROUP:(gi + 1) * GROUP]
        wsel = jnp.where(io == r[:, 2:3], r[:, 0:1], 0.0) + jnp.where(io == r[:, 3:4], r[:, 1:2], 0.0)
        ys = _unpack_bf16_rows(_load_rows(ys_ref, gi * BLOCK_ROWS, BLOCK_ROWS))
        ys = jnp.concatenate([ys, jnp.zeros((k_pad - BLOCK_ROWS, D_MODEL), jnp.bfloat16)], axis=0)
        parts.append(jnp.dot(wsel.astype(jnp.bfloat16), ys, preferred_element_type=jnp.float32))
    o_ref[...] = _rms(x1_ref[...] + jnp.concatenate(parts, axis=0), gf_ref[...])


def _combine(x1, rw, gf, ys):
    t = x1.shape[0]
    tok = lambda i: (i, 0)
    return pl.pallas_call(
        _combine_kernel,
        grid=(t // TS,),
        in_specs=[
            pl.BlockSpec((TS, D_MODEL), tok),
            pl.BlockSpec((TS, LANES), tok),
            pl.BlockSpec((1, D_MODEL), lambda i: (0, 0)),
            pl.BlockSpec((GROUPS_PER_STEP * BLOCK_ROWS * ROW_LINES, LANES), tok),
        ],
        out_specs=pl.BlockSpec((TS, D_MODEL), tok),
        out_shape=jax.ShapeDtypeStruct((t, D_MODEL), jnp.float32),
        compiler_params=pltpu.CompilerParams(dimension_semantics=("arbitrary",),
                                             vmem_limit_bytes=VMEM_LIMIT),
        name="combine",
    )(x1, rw, gf, ys)


def _rope_lane_tables(seq):
    inv_freq = ROPE_THETA ** (-jnp.arange(0, ROT_DIM, 2, dtype=jnp.float32) / ROT_DIM)
    ang = jnp.arange(seq, dtype=jnp.float32)[:, None] * inv_freq[None, :]
    cos, sin = jnp.cos(ang), jnp.sin(ang)
    half = ROT_DIM // 2
    ones = jnp.ones((seq, HEAD_DIM - ROT_DIM), jnp.float32)
    zeros = jnp.zeros((seq, HEAD_DIM - ROT_DIM), jnp.float32)
    zh = jnp.zeros((seq, half), jnp.float32)
    c = jnp.concatenate([cos, cos, ones], axis=1)
    s1 = jnp.concatenate([zh, sin, zeros], axis=1)
    s2 = jnp.concatenate([-sin, zh, zeros], axis=1)
    rep = LANES // HEAD_DIM
    return jnp.tile(c, (1, rep)), jnp.tile(s1, (1, rep)), jnp.tile(s2, (1, rep))


def _pick(table, idx):
    return jnp.sum(jnp.where(idx[:, None] == jnp.arange(table.shape[0])[None, :], table[None, :], 0), axis=1)


def _tile_tables(cnt, n_blocks, grid_tiles):
    i32 = jnp.int32
    n = cnt[:, :, 0].astype(i32)
    n8 = (n + ROW_ALIGN - 1) // ROW_ALIGN * ROW_ALIGN
    loc = jnp.cumsum(n8, axis=1) - n8
    used = jnp.sum(n8, axis=1)
    cum = jnp.concatenate([jnp.zeros((1, N_EXPERTS), i32), jnp.cumsum(n8, axis=0)], axis=0).T
    total = cum[:, -1]
    src = (jnp.arange(n_blocks, dtype=i32)[:, None] * BLOCK_ROWS + loc).T
    tiles_e = (total + TM - 1) // TM
    tile_end = jnp.cumsum(tiles_e)
    n_tiles = tile_end[-1]
    j = jnp.arange(grid_tiles, dtype=i32)
    te = jnp.minimum(jnp.sum(tile_end[None, :] <= j[:, None], axis=1), N_EXPERTS - 1).astype(i32)
    live = j < n_tiles
    te = jnp.where(live, te, _pick(te, jnp.maximum(n_tiles - 1, 0)[None])[0])
    r0 = (j - _pick(tile_end - tiles_e, te)) * TM
    cum_t = cum[te]
    b0 = jnp.sum(cum_t[:, 1:] <= r0[:, None], axis=1)
    b1 = jnp.sum(cum_t[:, :-1] < (r0 + TM)[:, None], axis=1)
    rows = jnp.clip(_pick(total, te) - r0, 0, TM)
    zero = jnp.zeros_like(j)
    has = tiles_e > 0
    ids = jnp.arange(N_EXPERTS, dtype=i32)
    nxt = jnp.min(jnp.where((ids[None, :] > ids[:, None]) & has[None, :], ids[None, :], N_EXPERTS), axis=1)
    nxt = jnp.where(nxt < N_EXPERTS, nxt, -1)
    order = jnp.cumsum(has.astype(i32)) - 1
    return (te, jnp.where(live, r0, zero), jnp.where(live, b0, zero).astype(i32),
            jnp.where(live, b1, zero).astype(i32), jnp.where(live, rows, zero).astype(i32),
            n_tiles.reshape(1).astype(i32), cum.reshape(-1), src.reshape(-1).astype(i32), used.astype(i32),
            _pick(order, te).astype(i32), _pick(nxt, te).astype(i32))


def kernel(x, norm1_g, w_in, attn_sinks, gmlp_ln_g, gmlp_ln_b, gmlp_ws, gmlp_bs, w_attn_branch,
           w_gmlp_branch, w_out, norm2_g, router_group_w, router_group_b, router_expert_w,
           router_expert_b, expert_w_gate, expert_w_up, expert_w_down, final_norm_g):
    b, s, d = x.shape
    assert d == D_MODEL and s % TS == 0 and norm1_g.shape[0] == 1
    t = b * s
    n_blocks = t // TS
    bf16, f32 = jnp.bfloat16, jnp.float32
    x2 = x.reshape(t, d)

    rc, rs1, rs2 = _rope_lane_tables(s)
    sk = attn_sinks[0].astype(f32)
    sinks = jnp.stack([
        jnp.concatenate([jnp.full((BLOCK,), 1.0, f32) * sk[4 * g + p], jnp.full((BLOCK,), 1.0, f32) * sk[4 * g + 2 + p]])
        for g in range(N_KV_HEADS) for p in range(2)])
    ws = gmlp_ws[0]
    wcat = jnp.stack([jnp.concatenate([ws[2 * j], ws[2 * j + 1]], axis=1) for j in range(G_GROUPS // 2)])
    bsf = jnp.repeat(gmlp_bs[0].T, G_GROUP_DIM, axis=1)
    wrt = jnp.concatenate([router_group_w[0].T, jnp.zeros((8 - N_GROUPS, d), f32), router_expert_w[0].T], axis=0)
    brt = jnp.concatenate([router_group_b[0], jnp.zeros((8 - N_GROUPS,), f32), router_expert_b[0]])
    brt = jnp.broadcast_to(brt[:, None], (ROUTER_ROWS, TS))
    tok_ids = jnp.arange(TS)
    tri = ((tok_ids[:, None] < tok_ids[None, :]) & (tok_ids[:, None] // GROUP == tok_ids[None, :] // GROUP)
           ).astype(bf16)
    ltri = (jnp.arange(LANES)[None, :] < jnp.arange(N_EXPERTS)[:, None]).astype(bf16)

    x1, xs, rw, cnt = _mixer(
        x2, norm1_g, w_in[0].astype(bf16), rc, rs1, rs2, sinks, gmlp_ln_g, gmlp_ln_b, wcat, bsf,
        w_attn_branch[0].astype(bf16), w_gmlp_branch[0].astype(bf16), w_out[0].astype(bf16),
        norm2_g, wrt.astype(bf16), brt, tri, ltri, s)

    n_groups = n_blocks * GROUPS_PER_STEP
    max_tiles = (TOP_K * t + n_groups * N_EXPERTS * (ROW_ALIGN - 1)) // TM + N_EXPERTS
    grid_tiles = max_tiles + 2
    tables = _tile_tables(cnt, n_groups, grid_tiles)
    ys = _experts(tables, xs, expert_w_gate[0], expert_w_up[0], expert_w_down[0], n_groups, grid_tiles)
    out = _combine(x1, rw, final_norm_g.reshape(1, d), ys)
    return out.reshape(b, s, d)
```

```python
import functools
import math

import jax
import jax.numpy as jnp
from jax import lax
from jax.experimental import pallas as pl
from jax.experimental.pallas import tpu as pltpu

D_MODEL = 1024
HEAD_DIM = 64
N_HEADS = 8
N_KV_HEADS = 2
BLOCK = 128
ROT_DIM = HEAD_DIM // 4
ROPE_THETA = 500000.0
ATTN_WIDTH = N_HEADS * HEAD_DIM
KV_WIDTH = N_KV_HEADS * HEAD_DIM
G_GROUPS = 8
G_GROUP_DIM = 64
G_WIDTH = G_GROUPS * G_GROUP_DIM
Q_END = ATTN_WIDTH
K_END = Q_END + KV_WIDTH
V_END = K_END + KV_WIDTH
Z_END = V_END + 2 * G_WIDTH
GA_END = Z_END + D_MODEL
IN_COLS = GA_END + D_MODEL
N_GROUPS = 4
EXPERTS_PER_GROUP = 8
N_EXPERTS = N_GROUPS * EXPERTS_PER_GROUP
TOP_K = 2
D_EXPERT = 512
EPS = 1e-5
NEG_INF = -1e30

LANES = 128
SUBLANES = 8
ROUTER_ROWS = 8 + N_EXPERTS
PACK_COLS = D_MODEL // 2
ROW_LINES = PACK_COLS // LANES
ROW_ALIGN = SUBLANES // ROW_LINES

TS = 512
TM = 512
GATHER_AHEAD = 2
FAST_RUNS = 36
GROUP = 256
CHUNK_COLS = 256
LOOKAHEAD = 3
ROUTE_AT = (0, 1)
GROUPS_PER_STEP = TS // GROUP
BLOCK_ROWS = TOP_K * GROUP + N_EXPERTS * (ROW_ALIGN - 1)
VMEM_LIMIT = 58 * 1024 * 1024


def _rms(x, g):
    return x * lax.rsqrt(jnp.mean(x * x, axis=-1, keepdims=True) + EPS) * g


def _pack_bf16_pair(a, b):
    ua = lax.bitcast_convert_type(a.astype(jnp.bfloat16).astype(jnp.float32), jnp.uint32)
    ub = lax.bitcast_convert_type(b.astype(jnp.bfloat16).astype(jnp.float32), jnp.uint32)
    return ub | (ua >> 16)


def _unpack_bf16_rows(w):
    lo = lax.bitcast_convert_type(w << 16, jnp.float32)
    hi = lax.bitcast_convert_type(w & jnp.uint32(0xFFFF0000), jnp.float32)
    return jnp.concatenate([lo.astype(jnp.bfloat16), hi.astype(jnp.bfloat16)], axis=1)


def _load_rows(lines_ref, first_row, rows):
    return jnp.concatenate([lines_ref[pl.ds(first_row * ROW_LINES + c, rows, stride=ROW_LINES), :]
                            for c in range(ROW_LINES)], axis=1)


def _store_rows(lines_ref, first_row, packed):
    for c in range(ROW_LINES):
        lines_ref[pl.ds(first_row * ROW_LINES + c, packed.shape[0], stride=ROW_LINES), :] = \
            packed[:, c * LANES:(c + 1) * LANES]


def _route_and_compact(xnb, wrt_ref, brt_ref, tri_ref, ltri_ref, xs_ref, rw_ref, cnt_ref):
    ts = xnb.shape[0]
    f32, bf16 = jnp.float32, jnp.bfloat16
    lt = lax.dot_general(wrt_ref[...], xnb, (((1,), (1,)), ((), ())),
                         preferred_element_type=f32) + brt_ref[...]
    yield
    r = [lt[i:i + 1, :] for i in range(N_GROUPS)]
    gm_ = jnp.maximum(jnp.maximum(r[0], r[1]), jnp.maximum(r[2], r[3]))
    gsel = jnp.where(r[0] == gm_, 0.0, jnp.where(r[1] == gm_, 1.0, jnp.where(r[2] == gm_, 2.0, 3.0)))
    g_w = 1.0 / (jnp.exp(r[0] - gm_) + jnp.exp(r[1] - gm_) + jnp.exp(r[2] - gm_) + jnp.exp(r[3] - gm_))
    eg = [lt[8 + 8 * kk:16 + 8 * kk, :] for kk in range(N_GROUPS)]
    ein = jnp.where(gsel == 0.0, eg[0], jnp.where(gsel == 1.0, eg[1], jnp.where(gsel == 2.0, eg[2], eg[3])))
    io8 = lax.broadcasted_iota(jnp.int32, (EXPERTS_PER_GROUP, ts), 0).astype(f32)
    m1 = jnp.max(ein, axis=0, keepdims=True)
    i1 = jnp.min(jnp.where(ein == m1, io8, 8.0), axis=0, keepdims=True)
    e2 = jnp.where(io8 == i1, NEG_INF, ein)
    m2 = jnp.max(e2, axis=0, keepdims=True)
    i2 = jnp.min(jnp.where(e2 == m2, io8, 8.0), axis=0, keepdims=True)
    t2 = jnp.exp(m2 - m1)
    w1 = 1.0 / (1.0 + t2)
    wt0, wt1 = g_w * w1, g_w * (t2 * w1)
    eid0, eid1 = gsel * 8.0 + i1, gsel * 8.0 + i2

    io32 = lax.broadcasted_iota(jnp.int32, (N_EXPERTS, ts), 0).astype(f32)
    oh0, oh1 = io32 == eid0, io32 == eid1
    both = jnp.where(oh0, 1.0, jnp.where(oh1, 1.0, 0.0))
    pref = jnp.dot(both.astype(bf16), tri_ref[...], preferred_element_type=f32)
    yield
    gid = lax.broadcasted_iota(jnp.int32, (N_EXPERTS, ts), 1) // GROUP
    units = jnp.zeros((N_EXPERTS, ts), f32)
    for gi in range(GROUPS_PER_STEP):
        n_e = jnp.sum(jnp.where(gid == gi, both, 0.0), axis=1, keepdims=True)
        cnt_ref[gi] = jnp.broadcast_to(n_e, (N_EXPERTS, LANES))
        units = jnp.where(gid == gi, jnp.floor((n_e + (ROW_ALIGN - 1.0)) * (1.0 / ROW_ALIGN)), units)
    units = jnp.concatenate([units, jnp.zeros((LANES - N_EXPERTS, ts), f32)], axis=0).astype(bf16)
    tot = ROW_ALIGN * jnp.dot(ltri_ref[...], units, preferred_element_type=f32) + pref
    yield
    slot0 = jnp.sum(jnp.where(oh0, tot, 0.0), axis=0, keepdims=True)
    slot1 = jnp.sum(jnp.where(oh1, tot, 0.0), axis=0, keepdims=True)

    io128 = lax.broadcasted_iota(jnp.int32, (LANES, ts), 0)
    rw_ref[...] = jnp.where(io128 == 0, wt0, jnp.where(io128 == 1, wt1,
                            jnp.where(io128 == 2, slot0, jnp.where(io128 == 3, slot1, 0.0)))).T

    s0i, s1i = slot0.astype(jnp.int32), slot1.astype(jnp.int32)
    io = lax.broadcasted_iota(jnp.int32, (BLOCK_ROWS, GROUP), 0)
    for gi in range(GROUPS_PER_STEP):
        tok = slice(gi * GROUP, (gi + 1) * GROUP)
        onehot = jnp.where(io == s0i[:, tok], 1.0, jnp.where(io == s1i[:, tok], 1.0, 0.0)).astype(bf16)
        rows = jnp.dot(onehot, xnb[tok, :], preferred_element_type=f32)
        _store_rows(xs_ref, gi * BLOCK_ROWS, _pack_bf16_pair(rows[:, 0:PACK_COLS], rows[:, PACK_COLS:]))
        yield


def _mixer_kernel(x_ref, xp_ref, g1_ref, win_ref, rc_ref, rs1_ref, rs2_ref, sink_ref, lng_ref, lnb_ref,
                  wcat_ref, bsf_ref, wab_ref, wgb_ref, wout_ref, g2_ref, wrt_ref, brt_ref, tri_ref, ltri_ref,
                  x1_ref, xs_ref, rw_ref, cnt_ref,
                  kl_ref, vt_ref, attnt_ref, gated_ref, mix_ref, *, blocks_per_seq, n_blocks):
    step = pl.program_id(0)
    s_blk = jnp.minimum(step, n_blocks - 1) % blocks_per_seq
    cur = step % 2
    ts = x_ref.shape[0]
    nb = ts // BLOCK
    f32, bf16 = jnp.float32, jnp.bfloat16

    @pl.when(step == 0)
    def _():
        mix_ref[1] = jnp.zeros((ts, D_MODEL), bf16)

    @pl.when(s_blk == 0)
    def _():
        kl_ref[:, 0:BLOCK, :] = jnp.zeros((4, BLOCK, LANES), bf16)
        vt_ref[:, 0:BLOCK] = jnp.zeros((LANES, BLOCK), bf16)

    x1 = xp_ref[...] + jnp.dot(mix_ref[1 - cur], wout_ref[...], preferred_element_type=f32)
    x1_ref[...] = x1
    route = _route_and_compact(_rms(x1, g2_ref[...]).astype(bf16), wrt_ref, brt_ref, tri_ref, ltri_ref,
                               xs_ref, rw_ref, cnt_ref)
    step_route = lambda: next(route, None)

    x = x_ref[...]
    hb = _rms(x, g1_ref[...]).astype(bf16)

    qkv = jnp.dot(hb, win_ref[:, 0:V_END], preferred_element_type=f32)
    rc, rs1, rs2 = rc_ref[...], rs1_ref[...], rs2_ref[...]

    def rope(t):
        return t * rc + pltpu.roll(t, 8, 1) * rs1 + pltpu.roll(t, LANES - 8, 1) * rs2

    scale = 1.0 / math.sqrt(HEAD_DIM)
    qb = [(rope(qkv[:, j * LANES:(j + 1) * LANES]) * scale).astype(bf16) for j in range(4)]
    k = rope(qkv[:, Q_END:K_END])
    v = qkv[:, K_END:V_END]
    k_sw = pltpu.roll(k, HEAD_DIM, 1)
    lo = lax.broadcasted_iota(jnp.int32, (ts, LANES), 1) < HEAD_DIM
    zero = jnp.zeros_like(k)
    kl_ref[0, BLOCK:BLOCK + ts, :] = jnp.where(lo, k, zero).astype(bf16)
    kl_ref[1, BLOCK:BLOCK + ts, :] = jnp.where(lo, zero, k_sw).astype(bf16)
    kl_ref[2, BLOCK:BLOCK + ts, :] = jnp.where(lo, k_sw, zero).astype(bf16)
    kl_ref[3, BLOCK:BLOCK + ts, :] = jnp.where(lo, zero, k).astype(bf16)
    vt_ref[:, BLOCK:BLOCK + ts] = v.T.astype(bf16)

    kj = lax.broadcasted_iota(jnp.int32, (2 * BLOCK, 2 * BLOCK), 0)
    qi = lax.broadcasted_iota(jnp.int32, (2 * BLOCK, 2 * BLOCK), 1) % BLOCK
    band = (kj > qi) & (kj <= qi + BLOCK)
    kmin = jnp.where(s_blk == 0, BLOCK, 0)
    bias_first = jnp.where(band & (kj >= kmin), 0.0, NEG_INF).astype(f32)
    bias_rest = jnp.where(band, 0.0, NEG_INF).astype(f32)
    items = [(i, g, p) for i in range(nb) for g in range(N_KV_HEADS) for p in range(2)]

    def scores(n):
        i, g, p = items[n]
        rows = slice(i * BLOCK, (i + 1) * BLOCK)
        qs = jnp.concatenate([qb[2 * g][rows], qb[2 * g + 1][rows]], axis=0)
        st = lax.dot_general(kl_ref[2 * g + p, i * BLOCK:(i + 2) * BLOCK, :], qs, (((1,), (1,)), ((), ())),
                             preferred_element_type=f32)
        return st + (bias_first if i == 0 else bias_rest)

    def finish(n, st):
        i, g, p = items[n]
        rows = slice(i * BLOCK, (i + 1) * BLOCK)
        sink = sink_ref[2 * g + p:2 * g + p + 1, :]
        m = jnp.maximum(jnp.max(st, axis=0, keepdims=True), sink)
        e = jnp.exp(st - m)
        den = jnp.sum(e, axis=0, keepdims=True) + jnp.exp(sink - m)
        ot = jnp.dot(vt_ref[g * HEAD_DIM:(g + 1) * HEAD_DIM, i * BLOCK:(i + 2) * BLOCK], e.astype(bf16),
                     preferred_element_type=f32) * (1.0 / den)
        h0 = 4 * g + p
        attnt_ref[h0 * HEAD_DIM:(h0 + 1) * HEAD_DIM, rows] = ot[:, 0:BLOCK]
        attnt_ref[(h0 + 2) * HEAD_DIM:(h0 + 3) * HEAD_DIM, rows] = ot[:, BLOCK:2 * BLOCK]

    n_chunks = D_MODEL // CHUNK_COLS
    zc, sga, sgg, gmc = [None] * n_chunks, [None] * n_chunks, [None] * n_chunks, [None] * n_chunks
    vn_box = []

    def proj(lo_col, c):
        cols = slice(lo_col + c * CHUNK_COLS, lo_col + (c + 1) * CHUNK_COLS)
        return jnp.dot(hb, win_ref[:, cols], preferred_element_type=f32)

    def z_chunk(c):
        zc[c] = jax.nn.gelu(proj(V_END, c))
        if c == n_chunks - 1:
            v2 = jnp.concatenate(zc[n_chunks // 2:], axis=1)
            mu = jnp.mean(v2, axis=-1, keepdims=True)
            vc = v2 - mu
            var = jnp.mean(vc * vc, axis=-1, keepdims=True)
            vn_box.append((vc * lax.rsqrt(var + EPS) * lng_ref[...] + lnb_ref[...]).astype(bf16))

    def ga_chunk(c):
        sga[c] = jax.nn.sigmoid(proj(Z_END, c))

    def gg_chunk(c):
        sgg[c] = jax.nn.sigmoid(proj(GA_END, c))

    wq = lax.broadcasted_iota(jnp.int32, (BLOCK, 2 * BLOCK), 0)
    wp = lax.broadcasted_iota(jnp.int32, (BLOCK, 2 * BLOCK), 1) % BLOCK
    causal = wp <= wq
    lo_c = lax.broadcasted_iota(jnp.int32, (BLOCK, LANES), 1) < G_GROUP_DIM
    wjs = {}

    def gmlp_dot(j, c):
        if j not in wjs:
            wjs[j] = jnp.where(causal, wcat_ref[j], 0.0).astype(bf16)
        cols = slice(j * LANES, (j + 1) * LANES)
        rhs = []
        for cc in (c, c + 1):
            vp = vn_box[0][cc * BLOCK:(cc + 1) * BLOCK, cols]
            zb = jnp.zeros_like(vp)
            rhs.append(jnp.concatenate([jnp.where(lo_c, vp, zb), jnp.where(lo_c, zb, vp)], axis=0))
        mixed = jnp.dot(wjs[j], jnp.concatenate(rhs, axis=1), preferred_element_type=f32)
        u = zc[j // 2][:, (j % 2) * LANES:(j % 2 + 1) * LANES]
        for q, cc in enumerate((c, c + 1)):
            rows = slice(cc * BLOCK, (cc + 1) * BLOCK)
            gated_ref[rows, cols] = u[rows] * (mixed[:, q * LANES:(q + 1) * LANES] + bsf_ref[:, cols])

    def gm_chunk(c):
        cols = slice(c * CHUNK_COLS, (c + 1) * CHUNK_COLS)
        gmc[c] = jnp.dot(gated_ref[...].astype(bf16), wgb_ref[:, cols], preferred_element_type=f32)

    fillers = [[functools.partial(z_chunk, c)] for c in range(n_chunks)]
    fillers += [[functools.partial(ga_chunk, c)] for c in range(n_chunks)]
    fillers += [[functools.partial(gg_chunk, c)] for c in range(n_chunks)]
    fillers += [[functools.partial(gm_chunk, c)] for c in range(n_chunks)]
    gd = [functools.partial(gmlp_dot, j, c) for j in range(G_GROUPS // 2) for c in range(0, nb, 2)]
    for q in range(len(gd)):
        fillers[n_chunks + q].append(gd[q])
    assert len(fillers) == len(items)

    fillers[0][0]()
    step_route()
    fillers[1][0]()
    step_route()
    step_route()
    fillers = fillers[2:] + [[], []]
    for n in ROUTE_AT:
        fillers[n].append(step_route)
    sts = {n: scores(n) for n in range(LOOKAHEAD)}
    for n in range(len(items)):
        finish(n, sts.pop(n))
        if n + LOOKAHEAD < len(items):
            sts[n + LOOKAHEAD] = scores(n + LOOKAHEAD)
        for f in fillers[n]:
            f()
    kl_ref[:, 0:BLOCK, :] = kl_ref[:, ts:ts + BLOCK, :]
    vt_ref[:, 0:BLOCK] = vt_ref[:, ts:ts + BLOCK]
    attn_b = attnt_ref[...].T.astype(bf16)

    mix = []
    for c in range(n_chunks):
        cols = slice(c * CHUNK_COLS, (c + 1) * CHUNK_COLS)
        a_c = jnp.dot(attn_b, wab_ref[:, cols], preferred_element_type=f32)
        mix.append((sga[c] * a_c + sgg[c] * gmc[c]).astype(bf16))
    mix_ref[cur] = jnp.concatenate(mix, axis=1)
    for _ in route:
        pass


def _const_spec(shape):
    nd = len(shape)
    return pl.BlockSpec(shape, lambda i: (0,) * nd)


def _mixer(x2, g1, win, rc, rs1, rs2, sinks, lng, lnb, wcat, bsf, wab, wgb, wout, g2, wrt, brt, tri, ltri, seq):
    t = x2.shape[0]
    n_blocks = t // TS
    blocks_per_seq = seq // TS
    cur_blk = lambda i: (jnp.minimum(i, n_blocks - 1), 0)
    prev_blk = lambda i: (jnp.maximum(i - 1, 0), 0)
    pos = lambda i: (jnp.minimum(i, n_blocks - 1) % blocks_per_seq, 0)
    consts = (sinks, lng, lnb, wcat, bsf, wab, wgb, wout, g2, wrt, brt, tri, ltri)
    in_specs = [
        pl.BlockSpec((TS, D_MODEL), cur_blk),
        pl.BlockSpec((TS, D_MODEL), prev_blk),
        _const_spec(g1.shape), _const_spec(win.shape),
        pl.BlockSpec((TS, LANES), pos), pl.BlockSpec((TS, LANES), pos), pl.BlockSpec((TS, LANES), pos),
    ] + [_const_spec(c.shape) for c in consts]
    out_shape = (
        jax.ShapeDtypeStruct((t, D_MODEL), jnp.float32),
        jax.ShapeDtypeStruct((n_blocks * GROUPS_PER_STEP * BLOCK_ROWS * ROW_LINES, LANES), jnp.uint32),
        jax.ShapeDtypeStruct((t, LANES), jnp.float32),
        jax.ShapeDtypeStruct((n_blocks * GROUPS_PER_STEP, N_EXPERTS, LANES), jnp.float32),
    )
    out_specs = (
        pl.BlockSpec((TS, D_MODEL), prev_blk),
        pl.BlockSpec((GROUPS_PER_STEP * BLOCK_ROWS * ROW_LINES, LANES), prev_blk),
        pl.BlockSpec((TS, LANES), prev_blk),
        pl.BlockSpec((GROUPS_PER_STEP, N_EXPERTS, LANES), lambda i: (jnp.maximum(i - 1, 0), 0, 0)),
    )
    scratch = [
        pltpu.VMEM((4, BLOCK + TS, LANES), jnp.bfloat16),
        pltpu.VMEM((LANES, BLOCK + TS), jnp.bfloat16),
        pltpu.VMEM((ATTN_WIDTH, TS), jnp.float32),
        pltpu.VMEM((TS, G_WIDTH), jnp.float32),
        pltpu.VMEM((2, TS, D_MODEL), jnp.bfloat16),
    ]
    return pl.pallas_call(
        functools.partial(_mixer_kernel, blocks_per_seq=blocks_per_seq, n_blocks=n_blocks),
        grid=(n_blocks + 1,),
        in_specs=in_specs,
        out_specs=out_specs,
        out_shape=out_shape,
        scratch_shapes=scratch,
        compiler_params=pltpu.CompilerParams(dimension_semantics=("arbitrary",),
                                             vmem_limit_bytes=VMEM_LIMIT),
        name="mixer",
    )(x2, x2, g1, win, rc, rs1, rs2, *consts)


def _expert_kernel(te_ref, tr0_ref, tb0_ref, tb1_ref, trows_ref, nt_ref, cum_ref, src_ref, used_ref, word_ref, wnext_ref,
                   xs_ref, wg_ref, wu_ref, wd_ref, ys_ref,
                   xbuf_ref, ybuf_ref, wgs_ref, wus_ref, wds_ref, wgb_ref, wub_ref, wdb_ref, zero_ref,
                   isem, osem, zsem, wsem,
                   *, n_blocks):
    j = pl.program_id(0)
    n_tiles = nt_ref[0]
    slot = j % 2
    f32, bf16 = jnp.float32, jnp.bfloat16

    def lines(rows):
        return pl.multiple_of(rows * ROW_LINES, SUBLANES)

    def run_piece(tile, b, ok, fn):
        e, r0 = te_ref[tile], tr0_ref[tile]
        bb = jnp.minimum(b, n_blocks - 1)
        start = cum_ref[e * (n_blocks + 1) + bb]
        lo = jnp.maximum(start, r0)
        n = jnp.minimum(cum_ref[e * (n_blocks + 1) + bb + 1], r0 + TM) - lo

        @pl.when(ok & (b < tb1_ref[tile]) & (n > 0))
        def _():
            fn(lines(src_ref[e * n_blocks + bb] + (lo - start)), lines(lo - r0), lines(n))

    def first_runs(tile, ok, fn):
        for k in range(FAST_RUNS):
            run_piece(tile, tb0_ref[tile] + k, ok, fn)

    def other_runs(tile, first, fn):
        lax.fori_loop(tb0_ref[tile] + first, tb1_ref[tile], lambda b, c: (run_piece(tile, b, True, fn), c)[1], 0)

    def gather_fn(s):
        return lambda src, dst, n: pltpu.make_async_copy(
            xs_ref.at[pl.ds(src, n)], xbuf_ref.at[s, pl.ds(dst, n)], isem.at[s]).start()

    def scatter_fn(s):
        return lambda src, dst, n: pltpu.make_async_copy(
            ybuf_ref.at[s, pl.ds(dst, n)], ys_ref.at[pl.ds(src, n)], osem.at[s]).start()

    def wait_rows(sem, n):
        @pl.when(n > 0)
        def _():
            pltpu.make_async_copy(xs_ref.at[pl.ds(0, lines(n))], xbuf_ref.at[0, pl.ds(0, lines(n))], sem).wait()

    def weight_copies(e, ws):
        return [pltpu.make_async_copy(src.at[e], dst.at[ws], wsem.at[ws])
                for src, dst in ((wg_ref, wgs_ref), (wu_ref, wus_ref), (wd_ref, wds_ref))]

    def zero_tail_copy(b, c):
        first = used_ref[b] + c * TM
        n = lines(jnp.minimum(BLOCK_ROWS - first, TM))
        return pltpu.make_async_copy(zero_ref.at[pl.ds(0, n)], ys_ref.at[pl.ds(lines(b * BLOCK_ROWS + first), n)], zsem)

    @pl.when(j == 0)
    def _():
        xbuf_ref[...] = jnp.zeros_like(xbuf_ref)
        zero_ref[...] = jnp.zeros_like(zero_ref)

        def per_block(b, carry):
            chunks = (BLOCK_ROWS - used_ref[b] + TM - 1) // TM
            lax.fori_loop(0, chunks, lambda c, x: (zero_tail_copy(b, c).start(), x)[1], 0)
            lax.fori_loop(0, chunks, lambda c, x: (zero_tail_copy(b, c).wait(), x)[1], 0)
            return carry

        lax.fori_loop(0, n_blocks, per_block, 0)

        for t0 in range(GATHER_AHEAD):
            @pl.when(t0 < n_tiles)
            def _():
                other_runs(t0, 0, gather_fn(t0))

        @pl.when(n_tiles > 0)
        def _():
            for cp in weight_copies(te_ref[0], 0):
                cp.start()

    prev = jnp.maximum(j - 1, 0)
    xslot = j % (GATHER_AHEAD + 1)
    nslot = (j + GATHER_AHEAD) % (GATHER_AHEAD + 1)

    @pl.when((j == n_tiles) & (j > 0))
    def _():
        other_runs(prev, 0, scatter_fn(1 - slot))

    @pl.when((j >= n_tiles) & (j >= 2) & (j - 2 < n_tiles))
    def _():
        wait_rows(osem.at[slot], trows_ref[jnp.maximum(j - 2, 0)])

    @pl.when(j < n_tiles)
    def _():
        @pl.when((j == 0) | (te_ref[j] != te_ref[prev]))
        def _():
            ws = word_ref[j] % 2
            for cp in weight_copies(te_ref[j], ws):
                cp.wait()
            wgb_ref[...] = wgs_ref[ws].astype(bf16)
            wub_ref[...] = wus_ref[ws].astype(bf16)
            wdb_ref[...] = wds_ref[ws].astype(bf16)

            @pl.when(wnext_ref[j] >= 0)
            def _():
                for cp in weight_copies(wnext_ref[j], 1 - ws):
                    cp.start()

        wait_rows(isem.at[xslot], trows_ref[j])
        nxt = jnp.minimum(j + GATHER_AHEAD, n_tiles - 1)
        more = j + GATHER_AHEAD < n_tiles

        def tile_compute(rows):
            xb = _unpack_bf16_rows(_load_rows(xbuf_ref.at[xslot], 0, rows))
            first_runs(prev, j > 0, scatter_fn(1 - slot))
            first_runs(nxt, more, gather_fn(nslot))
            half = D_EXPERT // 2
            g0 = jnp.dot(xb, wgb_ref[:, 0:half], preferred_element_type=f32)
            u0 = jnp.dot(xb, wub_ref[:, 0:half], preferred_element_type=f32)
            g1 = jnp.dot(xb, wgb_ref[:, half:], preferred_element_type=f32)
            u1 = jnp.dot(xb, wub_ref[:, half:], preferred_element_type=f32)
            y = jnp.dot((jax.nn.silu(g0) * u0).astype(bf16), wdb_ref[0:half, :], preferred_element_type=f32)
            y = y + jnp.dot((jax.nn.silu(g1) * u1).astype(bf16), wdb_ref[half:, :], preferred_element_type=f32)
            packed = _pack_bf16_pair(y[:, 0:PACK_COLS], y[:, PACK_COLS:])
            probe = xbuf_ref[xslot, pl.ds(0, SUBLANES, stride=ROW_LINES), :]

            @pl.when(j >= 2)
            def _():
                wait_rows(osem.at[slot], trows_ref[jnp.maximum(j - 2, 0)])

            _store_rows(ybuf_ref.at[slot], 0, packed)
            ybuf_ref[slot, pl.ds(0, SUBLANES, stride=ROW_LINES), :] = (
                packed[0:SUBLANES, 0:LANES] | ((probe >> 16) >> 16))

        @pl.when(trows_ref[j] > TM // 2)
        def _():
            tile_compute(TM)

        @pl.when(trows_ref[j] <= TM // 2)
        def _():
            tile_compute(TM // 2)

        @pl.when(more)
        def _():
            other_runs(nxt, FAST_RUNS, gather_fn(nslot))

        @pl.when(j > 0)
        def _():
            other_runs(prev, FAST_RUNS, scatter_fn(1 - slot))


def _experts(tables, xs, wg, wu, wd, n_blocks, grid_tiles):
    grid_spec = pltpu.PrefetchScalarGridSpec(
        num_scalar_prefetch=len(tables),
        grid=(grid_tiles,),
        in_specs=[
            pl.BlockSpec(memory_space=pl.ANY),
            pl.BlockSpec(memory_space=pl.ANY),
            pl.BlockSpec(memory_space=pl.ANY),
            pl.BlockSpec(memory_space=pl.ANY),
        ],
        out_specs=pl.BlockSpec(memory_space=pl.ANY),
        scratch_shapes=[
            pltpu.VMEM((GATHER_AHEAD + 1, TM * ROW_LINES, LANES), jnp.uint32),
            pltpu.VMEM((2, TM * ROW_LINES, LANES), jnp.uint32),
            pltpu.VMEM((2, D_MODEL, D_EXPERT), jnp.float32),
            pltpu.VMEM((2, D_MODEL, D_EXPERT), jnp.float32),
            pltpu.VMEM((2, D_EXPERT, D_MODEL), jnp.float32),
            pltpu.VMEM((D_MODEL, D_EXPERT), jnp.bfloat16),
            pltpu.VMEM((D_MODEL, D_EXPERT), jnp.bfloat16),
            pltpu.VMEM((D_EXPERT, D_MODEL), jnp.bfloat16),
            pltpu.VMEM((TM * ROW_LINES, LANES), jnp.uint32),
            pltpu.SemaphoreType.DMA((GATHER_AHEAD + 1,)),
            pltpu.SemaphoreType.DMA((2,)),
            pltpu.SemaphoreType.DMA,
            pltpu.SemaphoreType.DMA((2,)),
        ],
    )
    return pl.pallas_call(
        functools.partial(_expert_kernel, n_blocks=n_blocks),
        grid_spec=grid_spec,
        out_shape=jax.ShapeDtypeStruct(xs.shape, jnp.uint32),
        compiler_params=pltpu.CompilerParams(dimension_semantics=("arbitrary",),
                                             vmem_limit_bytes=VMEM_LIMIT),
        name="experts",
    )(*tables, xs, wg, wu, wd)


COMBINE_BUFS = 3
COMBINE_LINES = GROUPS_PER_STEP * BLOCK_ROWS * ROW_LINES


def _combine_kernel(rw_ref, gf_ref, x1_hbm, ys_hbm, o_ref, xbuf_ref, ybuf_ref, sem):
    i = pl.program_id(0)
    n_steps = pl.num_programs(0)

    def copies(step, slot):
        return (pltpu.make_async_copy(x1_hbm.at[pl.ds(pl.multiple_of(step * TS, TS), TS)],
                                      xbuf_ref.at[slot], sem.at[0, slot]),
                pltpu.make_async_copy(ys_hbm.at[pl.ds(pl.multiple_of(step * COMBINE_LINES, SUBLANES), COMBINE_LINES)],
                                      ybuf_ref.at[slot], sem.at[1, slot]))

    @pl.when(i == 0)
    def _():
        for s in range(COMBINE_BUFS - 1):
            @pl.when(s < n_steps)
            def _():
                for cp in copies(s, s):
                    cp.start()

    ahead = i + COMBINE_BUFS - 1

    @pl.when(ahead < n_steps)
    def _():
        for cp in copies(ahead, ahead % COMBINE_BUFS):
            cp.start()

    slot = i % COMBINE_BUFS
    for cp in copies(i, slot):
        cp.wait()
    x1_ref, ys_ref = xbuf_ref.at[slot], ybuf_ref.at[slot]
    rw = rw_ref[...]
    k_pad = -(-BLOCK_ROWS // LANES) * LANES
    io = lax.broadcasted_iota(jnp.int32, (GROUP, k_pad), 1).astype(jnp.float32)
    parts = []
    for gi in range(GROUPS_PER_STEP):
        r = rw[gi * GROUP:(gi + 1) * GROUP]
        wsel = jnp.where(io == r[:, 2:3], r[:, 0:1], 0.0) + jnp.where(io == r[:, 3:4], r[:, 1:2], 0.0)
        ys = _unpack_bf16_rows(_load_rows(ys_ref, gi * BLOCK_ROWS, BLOCK_ROWS))
        ys = jnp.concatenate([ys, jnp.zeros((k_pad - BLOCK_ROWS, D_MODEL), jnp.bfloat16)], axis=0)
        parts.append(jnp.dot(wsel.astype(jnp.bfloat16), ys, preferred_element_type=jnp.float32))
    o_ref[...] = _rms(x1_ref[...] + jnp.concatenate(parts, axis=0), gf_ref[...])


def _combine(x1, rw, gf, ys):
    t = x1.shape[0]
    tok = lambda i: (i, 0)
    return pl.pallas_call(
        _combine_kernel,
        grid=(t // TS,),
        in_specs=[
            pl.BlockSpec((TS, LANES), tok),
            pl.BlockSpec((1, D_MODEL), lambda i: (0, 0)),
            pl.BlockSpec(memory_space=pl.ANY),
            pl.BlockSpec(memory_space=pl.ANY),
        ],
        out_specs=pl.BlockSpec((TS, D_MODEL), tok),
        out_shape=jax.ShapeDtypeStruct((t, D_MODEL), jnp.float32),
        scratch_shapes=[
            pltpu.VMEM((COMBINE_BUFS, TS, D_MODEL), jnp.float32),
            pltpu.VMEM((COMBINE_BUFS, COMBINE_LINES, LANES), jnp.uint32),
            pltpu.SemaphoreType.DMA((2, COMBINE_BUFS)),
        ],
        compiler_params=pltpu.CompilerParams(dimension_semantics=("arbitrary",),
                                             vmem_limit_bytes=VMEM_LIMIT),
        name="combine",
    )(rw, gf, x1, ys)


def _rope_lane_tables(seq):
    inv_freq = ROPE_THETA ** (-jnp.arange(0, ROT_DIM, 2, dtype=jnp.float32) / ROT_DIM)
    ang = jnp.arange(seq, dtype=jnp.float32)[:, None] * inv_freq[None, :]
    cos, sin = jnp.cos(ang), jnp.sin(ang)
    half = ROT_DIM // 2
    ones = jnp.ones((seq, HEAD_DIM - ROT_DIM), jnp.float32)
    zeros = jnp.zeros((seq, HEAD_DIM - ROT_DIM), jnp.float32)
    zh = jnp.zeros((seq, half), jnp.float32)
    c = jnp.concatenate([cos, cos, ones], axis=1)
    s1 = jnp.concatenate([zh, sin, zeros], axis=1)
    s2 = jnp.concatenate([-sin, zh, zeros], axis=1)
    rep = LANES // HEAD_DIM
    return jnp.tile(c, (1, rep)), jnp.tile(s1, (1, rep)), jnp.tile(s2, (1, rep))


def _pick(table, idx):
    return jnp.sum(jnp.where(idx[:, None] == jnp.arange(table.shape[0])[None, :], table[None, :], 0), axis=1)


def _tile_tables(cnt, n_blocks, grid_tiles):
    i32 = jnp.int32
    n = cnt[:, :, 0].astype(i32)
    n8 = (n + ROW_ALIGN - 1) // ROW_ALIGN * ROW_ALIGN
    loc = jnp.cumsum(n8, axis=1) - n8
    used = jnp.sum(n8, axis=1)
    cum = jnp.concatenate([jnp.zeros((1, N_EXPERTS), i32), jnp.cumsum(n8, axis=0)], axis=0).T
    total = cum[:, -1]
    src = (jnp.arange(n_blocks, dtype=i32)[:, None] * BLOCK_ROWS + loc).T
    tiles_e = (total + TM - 1) // TM
    tile_end = jnp.cumsum(tiles_e)
    n_tiles = tile_end[-1]
    j = jnp.arange(grid_tiles, dtype=i32)
    te = jnp.minimum(jnp.sum(tile_end[None, :] <= j[:, None], axis=1), N_EXPERTS - 1).astype(i32)
    live = j < n_tiles
    te = jnp.where(live, te, _pick(te, jnp.maximum(n_tiles - 1, 0)[None])[0])
    r0 = (j - _pick(tile_end - tiles_e, te)) * TM
    cum_t = cum[te]
    b0 = jnp.sum(cum_t[:, 1:] <= r0[:, None], axis=1)
    b1 = jnp.sum(cum_t[:, :-1] < (r0 + TM)[:, None], axis=1)
    rows = jnp.clip(_pick(total, te) - r0, 0, TM)
    zero = jnp.zeros_like(j)
    has = tiles_e > 0
    ids = jnp.arange(N_EXPERTS, dtype=i32)
    nxt = jnp.min(jnp.where((ids[None, :] > ids[:, None]) & has[None, :], ids[None, :], N_EXPERTS), axis=1)
    nxt = jnp.where(nxt < N_EXPERTS, nxt, -1)
    order = jnp.cumsum(has.astype(i32)) - 1
    return (te, jnp.where(live, r0, zero), jnp.where(live, b0, zero).astype(i32),
            jnp.where(live, b1, zero).astype(i32), jnp.where(live, rows, zero).astype(i32),
            n_tiles.reshape(1).astype(i32), cum.reshape(-1), src.reshape(-1).astype(i32), used.astype(i32),
            _pick(order, te).astype(i32), _pick(nxt, te).astype(i32))


def kernel(x, norm1_g, w_in, attn_sinks, gmlp_ln_g, gmlp_ln_b, gmlp_ws, gmlp_bs, w_attn_branch,
           w_gmlp_branch, w_out, norm2_g, router_group_w, router_group_b, router_expert_w,
           router_expert_b, expert_w_gate, expert_w_up, expert_w_down, final_norm_g):
    b, s, d = x.shape
    assert d == D_MODEL and s % TS == 0 and norm1_g.shape[0] == 1
    t = b * s
    n_blocks = t // TS
    bf16, f32 = jnp.bfloat16, jnp.float32
    x2 = x.reshape(t, d)

    rc, rs1, rs2 = _rope_lane_tables(s)
    sk = attn_sinks[0].astype(f32)
    sinks = jnp.stack([
        jnp.concatenate([jnp.full((BLOCK,), 1.0, f32) * sk[4 * g + p], jnp.full((BLOCK,), 1.0, f32) * sk[4 * g + 2 + p]])
        for g in range(N_KV_HEADS) for p in range(2)])
    ws = gmlp_ws[0]
    wcat = jnp.stack([jnp.concatenate([ws[2 * j], ws[2 * j + 1]], axis=1) for j in range(G_GROUPS // 2)])
    bsf = jnp.repeat(gmlp_bs[0].T, G_GROUP_DIM, axis=1)
    wrt = jnp.concatenate([router_group_w[0].T, jnp.zeros((8 - N_GROUPS, d), f32), router_expert_w[0].T], axis=0)
    brt = jnp.concatenate([router_group_b[0], jnp.zeros((8 - N_GROUPS,), f32), router_expert_b[0]])
    brt = jnp.broadcast_to(brt[:, None], (ROUTER_ROWS, TS))
    tok_ids = jnp.arange(TS)
    tri = ((tok_ids[:, None] < tok_ids[None, :]) & (tok_ids[:, None] // GROUP == tok_ids[None, :] // GROUP)
           ).astype(bf16)
    ltri = (jnp.arange(LANES)[None, :] < jnp.arange(N_EXPERTS)[:, None]).astype(bf16)

    x1, xs, rw, cnt = _mixer(
        x2, norm1_g, w_in[0].astype(bf16), rc, rs1, rs2, sinks, gmlp_ln_g, gmlp_ln_b, wcat, bsf,
        w_attn_branch[0].astype(bf16), w_gmlp_branch[0].astype(bf16), w_out[0].astype(bf16),
        norm2_g, wrt.astype(bf16), brt, tri, ltri, s)

    n_groups = n_blocks * GROUPS_PER_STEP
    max_tiles = (TOP_K * t + n_groups * N_EXPERTS * (ROW_ALIGN - 1)) // TM + N_EXPERTS
    grid_tiles = max_tiles + 2
    tables = _tile_tables(cnt, n_groups, grid_tiles)
    ys = _experts(tables, xs, expert_w_gate[0], expert_w_up[0], expert_w_down[0], n_groups, grid_tiles)
    out = _combine(x1, rw, final_norm_g.reshape(1, d), ys)
    return out.reshape(b, s, d)
```
